```python
import math
import jax, jax.numpy as jnp
from jax import lax
import numpy as np

D_MODEL = 1024
BATCH = 16
SEQ = 2048
DEPTH = 1

D_MIX = D_MODEL
SB_HEADS = 8
SB_HEAD_DIM = 64
SB_WIDTH = SB_HEADS * SB_HEAD_DIM
RW_HEADS = 8
RW_HEAD_DIM = 64
RW_WIDTH = RW_HEADS * RW_HEAD_DIM
LORA_W = 64
LORA_A = 64
LORA_G = 128
RW_IN = 3 * RW_WIDTH + LORA_W + LORA_A + LORA_G
SB_IN = 3 * SB_WIDTH
N_IN = SB_IN + RW_IN
Q_BLOCK = 128
N_GROUPS = 4
EXPERTS_PER_GROUP = 8
N_EXPERTS = N_GROUPS * EXPERTS_PER_GROUP
TOP_K_IN_GROUP = 2
D_EXPERT = 256
RMS_EPS = 1e-6
GN_EPS = 64e-5

kernel_name = "hybrid_sbattn_rwkv7_hiermoe"


def rmsnorm(x, g):
    xf = x.astype(jnp.float32)
    y = xf * lax.rsqrt(jnp.mean(xf * xf, axis=-1, keepdims=True) + RMS_EPS)
    return (y * g.astype(jnp.float32)).astype(x.dtype)


def token_shift(u):
    return jnp.pad(u, ((0, 0), (1, 0), (0, 0)))[:, :-1]


def stick_breaking_attention(q, k, v):
    B, H, S, dh = q.shape
    scale = 1.0 / math.sqrt(dh)
    outs = []
    for blk in range(S // Q_BLOCK):
        t0 = blk * Q_BLOCK
        kv_len = t0 + Q_BLOCK
        qb = q[:, :, t0:kv_len].astype(jnp.float32)
        kb = k[:, :, :kv_len].astype(jnp.float32)
        vb = v[:, :, :kv_len].astype(jnp.float32)
        z = jnp.einsum('bhqd,bhkd->bhqk', qb, kb) * scale
        t_pos = t0 + jnp.arange(Q_BLOCK)[:, None]
        s_pos = jnp.arange(kv_len)[None, :]
        strict = s_pos < t_pos
        log_rest = jnp.where(strict, jax.nn.log_sigmoid(-z), 0.0)
        suffix = lax.cumsum(log_rest, axis=3, reverse=True) - log_rest
        weights = jnp.where(strict, jnp.exp(jax.nn.log_sigmoid(z) + suffix), 0.0)
        outs.append(jnp.einsum('bhqk,bhkd->bhqd', weights, vb))
    return jnp.concatenate(outs, axis=2)


def rwkv7_scan(r, w, k, v, kk, a):
    B, S, H, N = r.shape
    xs = tuple(jnp.moveaxis(t, 1, 0) for t in (r, w, k, v, kk, a))

    def step(state, inp):
        r_t, w_t, k_t, v_t, kk_t, a_t = inp
        sa = jnp.einsum('bhij,bhj->bhi', state, -kk_t)
        state = (state * w_t[:, :, None, :]
                 + sa[..., None] * (kk_t * a_t)[:, :, None, :]
                 + v_t[..., None] * k_t[:, :, None, :])
        out = jnp.einsum('bhij,bhj->bhi', state, r_t)
        return state, out

    state0 = jnp.zeros((B, H, N, N), jnp.float32)
    _, outs = lax.scan(step, state0, xs)
    return jnp.moveaxis(outs, 0, 1)


def rwkv7_group(u_rw, shift_mu, w0, w2, a0, a2, g2, k_k, k_a, r_k, ln_w, ln_b):
    B, S, _ = u_rw.shape
    uf = u_rw.astype(jnp.float32)
    um = uf + (token_shift(uf) - uf) * shift_mu.astype(jnp.float32)
    o = 0
    r = um[..., o:o + RW_WIDTH]; o += RW_WIDTH
    k = um[..., o:o + RW_WIDTH]; o += RW_WIDTH
    v = um[..., o:o + RW_WIDTH]; o += RW_WIDTH
    xw = um[..., o:o + LORA_W]; o += LORA_W
    xa = um[..., o:o + LORA_A]; o += LORA_A
    xg = um[..., o:o + LORA_G]
    w_log = -jax.nn.softplus(-(w0 + jnp.tanh(xw) @ w2)) - 0.5
    decay = jnp.exp(-jnp.exp(w_log))
    a = jax.nn.sigmoid(a0 + xa @ a2)
    g = jax.nn.sigmoid(xg) @ g2
    hs = lambda t: t.reshape(B, S, RW_HEADS, RW_HEAD_DIM)
    kk = hs(k * k_k)
    kk = kk * lax.rsqrt(jnp.maximum(jnp.sum(kk * kk, axis=-1, keepdims=True), 1e-24))
    k = k * (1.0 + (a - 1.0) * k_a)
    r_h, k_h, v_h, w_h, a_h = hs(r), hs(k), hs(v), hs(decay), hs(a)
    out = rwkv7_scan(r_h, w_h, k_h, v_h, kk, a_h)
    mean = jnp.mean(out, axis=-1, keepdims=True)
    var = jnp.mean(jnp.square(out - mean), axis=-1, keepdims=True)
    out = ((out - mean) * lax.rsqrt(var + GN_EPS)).reshape(B, S, RW_WIDTH) * ln_w + ln_b
    bonus = jnp.sum(r_h * k_h * r_k, axis=-1, keepdims=True) * v_h
    out = (out + bonus.reshape(B, S, RW_WIDTH)) * g
    return out.astype(u_rw.dtype)


def sb_group(u_sb, sb_out_g):
    B, S, _ = u_sb.shape
    heads = lambda t: t.reshape(B, S, SB_HEADS, SB_HEAD_DIM).transpose(0, 2, 1, 3)
    q = heads(u_sb[..., :SB_WIDTH])
    k = heads(u_sb[..., SB_WIDTH:2 * SB_WIDTH])
    v = heads(u_sb[..., 2 * SB_WIDTH:])
    o = stick_breaking_attention(q, k, v).transpose(0, 2, 1, 3)
    o = o * lax.rsqrt(jnp.mean(o * o, axis=-1, keepdims=True) + RMS_EPS)
    return (o.reshape(B, S, SB_WIDTH) * sb_out_g).astype(u_sb.dtype)


def hier_moe(x2d, w_grp, b_grp, w_exp, b_exp, w_gate, w_up, w_down):
    T = x2d.shape[0]
    xf = x2d.astype(jnp.float32)
    grp_prob = jax.nn.softmax(xf @ w_grp.astype(jnp.float32) + b_grp.astype(jnp.float32), axis=-1)
    g_val, g_idx = lax.top_k(grp_prob, 1)
    exp_logits = (xf @ w_exp.astype(jnp.float32) + b_exp.astype(jnp.float32)).reshape(T, N_GROUPS, EXPERTS_PER_GROUP)
    sel = jnp.take_along_axis(exp_logits, g_idx[:, :, None], axis=1)[:, 0]
    e_val, e_idx = lax.top_k(jax.nn.softmax(sel, axis=-1), TOP_K_IN_GROUP)
    weights = g_val * e_val / jnp.sum(e_val, axis=-1, keepdims=True)
    global_idx = g_idx * EXPERTS_PER_GROUP + e_idx
    combine = jnp.sum(jax.nn.one_hot(global_idx, N_EXPERTS, dtype=jnp.float32) * weights[..., None], axis=1)
    combine = combine.astype(x2d.dtype)
    y = jnp.zeros_like(x2d)
    for e in range(N_EXPERTS):
        h = jax.nn.silu(x2d @ w_gate[e]) * (x2d @ w_up[e])
        y = y + combine[:, e:e + 1] * (h @ w_down[e])
    return y


def setup_inputs(seed: int = 0) -> dict:
    key = jax.random.key(seed)
    ks = jax.random.split(key, 28)
    f32 = jnp.float32
    nrm = lambda k, shape, s: jax.random.normal(k, shape, f32) * s
    L = DEPTH
    return {
        "x": jax.random.normal(ks[0], (BATCH, SEQ, D_MODEL), f32),
        "norm_mix_g": 1.0 + nrm(ks[1], (L, D_MODEL), 0.02),
        "w_in": nrm(ks[2], (L, D_MODEL, N_IN), D_MODEL ** -0.5),
        "shift_mu": jax.random.uniform(ks[3], (L, RW_IN), f32, 0.1, 0.9),
        "sb_out_g": 1.0 + nrm(ks[4], (L, SB_WIDTH), 0.02),
        "rw_w0": jax.random.uniform(ks[5], (L, RW_WIDTH), f32, -6.5, -1.5),
        "rw_w2": nrm(ks[6], (L, LORA_W, RW_WIDTH), 0.5 * LORA_W ** -0.5),
        "rw_a0": nrm(ks[7], (L, RW_WIDTH), 0.1),
        "rw_a2": nrm(ks[8], (L, LORA_A, RW_WIDTH), LORA_A ** -0.5),
        "rw_g2": nrm(ks[9], (L, LORA_G, RW_WIDTH), LORA_G ** -0.5),
        "rw_k_k": 0.85 + nrm(ks[10], (L, RW_WIDTH), 0.02),
        "rw_k_a": 1.0 + nrm(ks[11], (L, RW_WIDTH), 0.02),
        "rw_r_k": nrm(ks[12], (L, RW_HEADS, RW_HEAD_DIM), 0.1),
        "rw_ln_w": 1.0 + nrm(ks[13], (L, RW_WIDTH), 0.02),
        "rw_ln_b": nrm(ks[14], (L, RW_WIDTH), 0.02),
        "w_out": nrm(ks[15], (L, D_MIX, D_MODEL), D_MIX ** -0.5),
        "norm_ffn_g": 1.0 + nrm(ks[16], (L, D_MODEL), 0.02),
        "router_grp_w": nrm(ks[17], (L, D_MODEL, N_GROUPS), D_MODEL ** -0.5),
        "router_grp_b": nrm(ks[18], (L, N_GROUPS), 0.01),
        "router_exp_w": nrm(ks[19], (L, D_MODEL, N_EXPERTS), D_MODEL ** -0.5),
        "router_exp_b": nrm(ks[20], (L, N_EXPERTS), 0.01),
        "exp_w_gate": nrm(ks[21], (L, N_EXPERTS, D_MODEL, D_EXPERT), D_MODEL ** -0.5),
        "exp_w_up": nrm(ks[22], (L, N_EXPERTS, D_MODEL, D_EXPERT), D_MODEL ** -0.5),
        "exp_w_down": nrm(ks[23], (L, N_EXPERTS, D_EXPERT, D_MODEL), D_EXPERT ** -0.5),
        "final_norm_g": 1.0 + nrm(ks[24], (D_MODEL,), 0.02),
    }


def reference(x, norm_mix_g, w_in, shift_mu, sb_out_g, rw_w0, rw_w2, rw_a0, rw_a2, rw_g2,
              rw_k_k, rw_k_a, rw_r_k, rw_ln_w, rw_ln_b, w_out, norm_ffn_g,
              router_grp_w, router_grp_b, router_exp_w, router_exp_b,
              exp_w_gate, exp_w_up, exp_w_down, final_norm_g):
    B, S, D = x.shape
    h = x
    for l in range(DEPTH):
        u = rmsnorm(h, norm_mix_g[l]) @ w_in[l]
        o_sb = sb_group(u[..., :SB_IN], sb_out_g[l])
        o_rw = rwkv7_group(u[..., SB_IN:], shift_mu[l], rw_w0[l], rw_w2[l], rw_a0[l], rw_a2[l],
                           rw_g2[l], rw_k_k[l], rw_k_a[l], rw_r_k[l], rw_ln_w[l], rw_ln_b[l])
        h = h + jnp.concatenate([o_sb, o_rw], axis=-1) @ w_out[l]
        xn = rmsnorm(h, norm_ffn_g[l]).reshape(B * S, D)
        y = hier_moe(xn, router_grp_w[l], router_grp_b[l], router_exp_w[l], router_exp_b[l],
                     exp_w_gate[l], exp_w_up[l], exp_w_down[l])
        h = h + y.reshape(B, S, D)
    return rmsnorm(h, final_norm_g)
```

```python
import functools
import math

import jax
import jax.numpy as jnp
from jax import lax
from jax.experimental import pallas as pl
from jax.experimental.pallas import tpu as pltpu

F32 = jnp.float32
BF16 = jnp.bfloat16

D_MODEL = 1024
SB_HEADS = 8
HEAD_DIM = 64
SB_WIDTH = SB_HEADS * HEAD_DIM
RW_HEADS = 8
RW_WIDTH = RW_HEADS * HEAD_DIM
LORA_W = 64
LORA_A = 64
LORA_G = 128
SB_IN = 3 * SB_WIDTH
RW_IN = 3 * RW_WIDTH + LORA_W + LORA_A + LORA_G
N_GROUPS = 4
EXPERTS_PER_GROUP = 8
N_EXPERTS = N_GROUPS * EXPERTS_PER_GROUP
D_EXPERT = 256
RMS_EPS = 1e-6
GN_EPS = 64e-5

LANES = 128
MXU_DIM = 256
VMEM_LIMIT = 48 * 1024 * 1024


def _cparams(sem):
    return pltpu.CompilerParams(dimension_semantics=sem, vmem_limit_bytes=VMEM_LIMIT)


def _dot(a, b):
    return jnp.dot(a, b, preferred_element_type=F32)


def _dot_nt(a, b):
    return lax.dot_general(a, b, (((1,), (1,)), ((), ())), preferred_element_type=F32)


def _dot_tn(a, b):
    return lax.dot_general(a, b, (((0,), (0,)), ((), ())), preferred_element_type=F32)


IN_TM = 512


def _in_proj_body(x_ref, g_ref, wsb_ref, wrw_ref, usb_ref, urw_ref):
    x = x_ref[...]
    ms = jnp.mean(x * x, axis=-1, keepdims=True)
    xn = (x * lax.rsqrt(ms + RMS_EPS) * g_ref[...]).astype(BF16)
    usb_ref[...] = _dot(xn, wsb_ref[...]).astype(BF16)
    urw_ref[...] = _dot(xn, wrw_ref[...]).astype(BF16)


def in_proj(x2d, g, w_sb, w_rw):
    T = x2d.shape[0]
    tm = min(IN_TM, T)
    return pl.pallas_call(
        _in_proj_body,
        grid=(T // tm,),
        in_specs=[
            pl.BlockSpec((tm, D_MODEL), lambda i: (i, 0)),
            pl.BlockSpec((1, D_MODEL), lambda i: (0, 0)),
            pl.BlockSpec((D_MODEL, SB_IN), lambda i: (0, 0)),
            pl.BlockSpec((D_MODEL, RW_IN), lambda i: (0, 0)),
        ],
        out_specs=[
            pl.BlockSpec((tm, SB_IN), lambda i: (i, 0)),
            pl.BlockSpec((tm, RW_IN), lambda i: (i, 0)),
        ],
        out_shape=[
            jax.ShapeDtypeStruct((T, SB_IN), BF16),
            jax.ShapeDtypeStruct((T, RW_IN), BF16),
        ],
        compiler_params=_cparams(("arbitrary",)),
        name="in_proj",
    )(x2d, g, w_sb, w_rw)


SB_BLK = 256
SB_SKIP = -100.0


def _neg_softplus(z):
    return -(jnp.maximum(z, 0.0) + jnp.log(1.0 + jnp.exp(-jnp.abs(z))))


def _sb_attn_body(q_ref, k_ref, v_ref, g_ref, o_ref, *, blk):
    qb = pl.program_id(2)
    lane = lax.broadcasted_iota(jnp.int32, (1, LANES), 1)
    head_masks = [lane < HEAD_DIM, lane >= HEAD_DIM]
    q = q_ref[0].astype(F32) * (1.0 / math.sqrt(HEAD_DIM))
    qh = [jnp.where(m, q, 0.0).astype(BF16) for m in head_masks]

    row = lax.broadcasted_iota(jnp.int32, (blk, blk), 0)
    col = lax.broadcasted_iota(jnp.int32, (blk, blk), 1)
    strict = col < row
    later = (row > col).astype(BF16)

    def chunk(c, carry, acc, diag):
        start = pl.multiple_of(c * blk, blk)
        kc = k_ref[0, pl.ds(start, blk), :]
        vc = v_ref[0, pl.ds(start, blk), :]
        new_carry = []
        for h in range(2):
            z = _dot_nt(qh[h], kc)
            ls_neg = _neg_softplus(z)
            log_rest = jnp.where(strict, ls_neg, 0.0) if diag else ls_neg
            suffix = _dot(log_rest.astype(BF16), later) + carry[h]
            w = jnp.exp(ls_neg + z + suffix)
            if diag:
                w = jnp.where(strict, w, 0.0)
            pv = _dot(w.astype(BF16), vc)
            acc = acc + jnp.where(head_masks[h], pv, 0.0)
            new_carry.append(carry[h] + jnp.sum(log_rest, axis=-1, keepdims=True))
        return new_carry, acc

    zero_c = jnp.zeros((blk, 1), F32)
    carry, acc = chunk(qb, [zero_c, zero_c], jnp.zeros((blk, LANES), F32), True)

    def alive_of(carry):
        return jnp.max(jnp.maximum(carry[0], carry[1])) > SB_SKIP

    def cond(st):
        c, alive = st[0], st[1]
        return jnp.logical_and(c >= 0, alive)

    def body(st):
        c, _, c0, c1, acc = st
        carry, acc = chunk(c, [c0, c1], acc, False)
        return c - 1, alive_of(carry), carry[0], carry[1], acc

    st = lax.while_loop(cond, body, (qb - 1, alive_of(carry), carry[0], carry[1], acc))
    acc = st[4]

    sq = acc * acc
    s_lo = jnp.sum(jnp.where(head_masks[0], sq, 0.0), axis=-1, keepdims=True)
    s_all = jnp.sum(sq, axis=-1, keepdims=True)
    ms = jnp.where(head_masks[0], s_lo, s_all - s_lo) * (1.0 / HEAD_DIM)
    o_ref[0] = (acc * lax.rsqrt(ms + RMS_EPS) * g_ref[...]).astype(BF16)


def sb_attn(u_sb, sb_out_g, B, S):
    blk = min(SB_BLK, S)
    u3 = u_sb.reshape(B, S, SB_IN)
    n_pairs = SB_WIDTH // LANES
    return pl.pallas_call(
        functools.partial(_sb_attn_body, blk=blk),
        grid=(B, n_pairs, S // blk),
        in_specs=[
            pl.BlockSpec((1, blk, LANES), lambda b, p, i: (b, i, p)),
            pl.BlockSpec((1, S, LANES), lambda b, p, i: (b, 0, n_pairs + p)),
            pl.BlockSpec((1, S, LANES), lambda b, p, i: (b, 0, 2 * n_pairs + p)),
            pl.BlockSpec((1, LANES), lambda b, p, i: (0, p)),
        ],
        out_specs=pl.BlockSpec((1, blk, LANES), lambda b, p, i: (b, i, p)),
        out_shape=jax.ShapeDtypeStruct((B, S, SB_WIDTH), BF16),
        compiler_params=_cparams(("arbitrary", "arbitrary", "arbitrary")),
        name="sb_attn",
    )(u3, u3, u3, sb_out_g)


RW_CHUNK = 64
RW_GROUP = MXU_DIM
RW_GROUP_HEADS = RW_GROUP // HEAD_DIM


def _softplus(y):
    return jnp.maximum(y, 0.0) + jnp.log(1.0 + jnp.exp(-jnp.abs(y)))


def _sigmoid(y):
    return 1.0 / (1.0 + jnp.exp(-y))


def _split_bf16(x):
    hi = x.astype(BF16)
    lo = (x - hi.astype(F32)).astype(BF16)
    return hi, lo


def _rwkv_body(u_ref, mu_ref, w0_ref, w2_ref, a0_ref, a2_ref, g2_ref, kk_ref, ka_ref, rk_ref,
               lnw_ref, lnb_ref, o_ref, prev_ref, state_ref, *, C):
    c = pl.program_id(1)
    G, GH = RW_GROUP, RW_GROUP_HEADS
    n_groups = RW_WIDTH // G

    @pl.when(c == 0)
    def _():
        prev_ref[...] = jnp.zeros_like(prev_ref)
        state_ref[...] = jnp.zeros_like(state_ref)

    u = u_ref[0].astype(F32)
    row_id = lax.broadcasted_iota(jnp.int32, (C, 1), 0)
    shifted = jnp.where(row_id == 0, prev_ref[...], pltpu.roll(u, 1, axis=0))
    prev_ref[...] = u[C - 1:C, :]
    um = u + (shifted - u) * mu_ref[...]

    r = um[:, 0:RW_WIDTH]
    k = um[:, RW_WIDTH:2 * RW_WIDTH]
    v = um[:, 2 * RW_WIDTH:3 * RW_WIDTH]
    xwa = um[:, 3 * RW_WIDTH:3 * RW_WIDTH + LORA_W + LORA_A]
    xg = um[:, 3 * RW_WIDTH + LORA_W + LORA_A:]

    lora_w = _dot(jnp.tanh(xwa).astype(BF16), w2_ref[...])
    lora_a = _dot(xwa.astype(BF16), a2_ref[...])
    gate = _dot(_sigmoid(xg).astype(BF16), g2_ref[...])
    w_log = -_softplus(-(w0_ref[...] + lora_w)) - 0.5
    logdec = -jnp.exp(w_log)
    lr = _sigmoid(a0_ref[...] + lora_a)

    head_bd = (lax.broadcasted_iota(jnp.int32, (G, G), 0) // HEAD_DIM
               == lax.broadcasted_iota(jnp.int32, (G, G), 1) // HEAD_DIM)
    ones_bd = head_bd.astype(BF16)
    stack_mask = (lax.broadcasted_iota(jnp.int32, (GH * C, G), 0) // C
                  == lax.broadcasted_iota(jnp.int32, (GH * C, G), 1) // HEAD_DIM)
    side_bd = (lax.broadcasted_iota(jnp.int32, (GH * C, GH * C), 0) // C
               == lax.broadcasted_iota(jnp.int32, (GH * C, GH * C), 1) // C)
    tt = lax.broadcasted_iota(jnp.int32, (C, GH * C), 0)
    ss = lax.broadcasted_iota(jnp.int32, (C, GH * C), 1) % C
    strict = ss < tt
    incl = ss <= tt
    tri_incl = (lax.broadcasted_iota(jnp.int32, (C, C), 1)
                <= lax.broadcasted_iota(jnp.int32, (C, C), 0)).astype(BF16)

    def head_sum(x):
        return _dot(x.astype(BF16), ones_bd)

    def stack(x):
        return jnp.where(stack_mask, jnp.concatenate([x] * GH, axis=0), 0.0).astype(BF16)

    def block_diag(x):
        return jnp.where(side_bd, jnp.concatenate([x] * GH, axis=0), 0.0).astype(BF16)

    ld_hi, ld_lo = _split_bf16(logdec)
    cum = _dot(tri_incl, ld_hi) + _dot(tri_incl, ld_lo)
    p_incl = jnp.exp(cum)
    p_prev = jnp.exp(cum - logdec)
    p_inv = jnp.exp(-cum)
    p_last = p_incl[C - 1:C, :]

    kk = k * kk_ref[...]
    k_adj = k * (1.0 + (lr - 1.0) * ka_ref[...])
    rk_prod = r * k_adj * rk_ref[...]

    outs = []
    for gi in range(n_groups):
        sl = slice(gi * G, (gi + 1) * G)
        kk_g = kk[:, sl]
        kkn = kk_g * lax.rsqrt(jnp.maximum(head_sum(kk_g * kk_g), 1e-24))
        v_g = v[:, sl]
        at = -kkn * p_prev[:, sl]
        bt = kkn * lr[:, sl] * p_inv[:, sl]
        kt = k_adj[:, sl] * p_inv[:, sl]
        rt = r[:, sl] * p_incl[:, sl]

        lhs2 = jnp.concatenate([at, rt], axis=0).astype(BF16)
        ab = _dot_nt(lhs2, stack(bt))
        ak = _dot_nt(lhs2, stack(kt))
        a_ab = jnp.where(strict, ab[:C], 0.0)
        q_ab = jnp.where(incl, ab[C:], 0.0)
        a_ak = jnp.where(strict, ak[:C], 0.0)
        q_ak = jnp.where(incl, ak[C:], 0.0)

        vs = stack(v_g)
        y1 = at
        y2 = _dot(a_ak.astype(BF16), vs)
        lp = a_ab
        n_steps = max(1, (C - 1).bit_length())
        for step in range(n_steps):
            last = step == n_steps - 1
            parts = [stack(y1), stack(y2)] + ([] if last else [block_diag(lp)])
            prod = _dot(lp.astype(BF16), jnp.concatenate(parts, axis=1))
            y1 = y1 + prod[:, 0:G]
            y2 = y2 + prod[:, G:2 * G]
            if not last:
                lp = prod[:, 2 * G:]

        s0 = state_ref[gi]
        s0b = s0.astype(BF16)
        uu = _dot_nt(y1.astype(BF16), s0b) + y2
        o_g = (_dot_nt(rt.astype(BF16), s0b)
               + _dot(q_ab.astype(BF16), stack(uu))
               + _dot(q_ak.astype(BF16), vs))
        upd = _dot_tn(jnp.concatenate([uu, v_g], axis=0).astype(BF16),
                      jnp.concatenate([bt * p_last[:, sl], kt * p_last[:, sl]], axis=0).astype(BF16))
        state_ref[gi] = s0 * p_last[:, sl] + jnp.where(head_bd, upd, 0.0)

        mean = head_sum(o_g) * (1.0 / HEAD_DIM)
        d = o_g - mean
        var = head_sum(d * d) * (1.0 / HEAD_DIM)
        gn = d * lax.rsqrt(var + GN_EPS) * lnw_ref[:, sl] + lnb_ref[:, sl]
        bonus = head_sum(rk_prod[:, sl]) * v_g
        outs.append((gn + bonus) * gate[:, sl])

    o_ref[0] = jnp.concatenate(outs, axis=1).astype(BF16)


def rwkv(u_rw, shift_mu, w0, w2_pad, a0, a2_pad, g2, k_k, k_a, r_k, ln_w, ln_b, B, S):
    C = min(RW_CHUNK, S)
    u3 = u_rw.reshape(B, S, RW_IN)
    vec = lambda n: pl.BlockSpec((1, n), lambda b, c: (0, 0))
    mat = lambda m, n: pl.BlockSpec((m, n), lambda b, c: (0, 0))
    return pl.pallas_call(
        functools.partial(_rwkv_body, C=C),
        grid=(B, S // C),
        in_specs=[
            pl.BlockSpec((1, C, RW_IN), lambda b, c: (b, c, 0)),
            vec(RW_IN), vec(RW_WIDTH), mat(LORA_W + LORA_A, RW_WIDTH), vec(RW_WIDTH),
            mat(LORA_W + LORA_A, RW_WIDTH), mat(LORA_G, RW_WIDTH),
            vec(RW_WIDTH), vec(RW_WIDTH), vec(RW_WIDTH), vec(RW_WIDTH), vec(RW_WIDTH),
        ],
        out_specs=pl.BlockSpec((1, C, RW_WIDTH), lambda b, c: (b, c, 0)),
        out_shape=jax.ShapeDtypeStruct((B, S, RW_WIDTH), BF16),
        scratch_shapes=[
            pltpu.VMEM((1, RW_IN), F32),
            pltpu.VMEM((RW_WIDTH // RW_GROUP, RW_GROUP, RW_GROUP), F32),
        ],
        compiler_params=_cparams(("arbitrary", "arbitrary")),
        name="rwkv",
    )(u3, shift_mu, w0, w2_pad, a0, a2_pad, g2, k_k, k_a, r_k, ln_w, ln_b)


OR_TM = 512
N_PAIRS = EXPERTS_PER_GROUP * (EXPERTS_PER_GROUP - 1) // 2
N_CLASSES = N_GROUPS * N_PAIRS
CLS_ROWS = LANES
ROUTER_ROWS = 48
EXT_W = D_MODEL + LANES
SUBLANES = 8


def _first_index_of(vals, target, row_f):
    return jnp.min(jnp.where(vals == target, row_f, 1e9), axis=0, keepdims=True)


def _out_router_body(x_ref, osb_ref, orw_ref, wo_ref, g_ref, wr_ref, wrhi_ref, br_ref,
                     hext_ref, meta_ref, hist_ref, carry_ref, *, tm):
    i = pl.program_id(0)

    @pl.when(i == 0)
    def _():
        carry_ref[...] = jnp.zeros_like(carry_ref)

    h = (x_ref[...] + _dot(osb_ref[...], wo_ref[0:SB_WIDTH, :])
         + _dot(orw_ref[...], wo_ref[SB_WIDTH:, :]))
    ms = jnp.mean(h * h, axis=-1, keepdims=True)
    xn = h * lax.rsqrt(ms + RMS_EPS) * g_ref[...]

    xn_hi, xn_lo = _split_bf16(xn)
    both = _dot_nt(wr_ref[...], xn_hi)
    logits = both[:ROUTER_ROWS] + both[ROUTER_ROWS:] + _dot_nt(wrhi_ref[...], xn_lo) + br_ref[...][:, 0:1]

    row_f = lax.broadcasted_iota(jnp.int32, (SUBLANES, tm), 0).astype(F32)
    lg = jnp.where(row_f < N_GROUPS, logits[N_EXPERTS:N_EXPERTS + SUBLANES], -jnp.inf)
    eg = jnp.exp(lg - jnp.max(lg, axis=0, keepdims=True))
    pg = eg / jnp.sum(eg, axis=0, keepdims=True)
    g_val = jnp.max(pg, axis=0, keepdims=True)
    g_idx = _first_index_of(pg, g_val, row_f)
    sel = jnp.zeros((SUBLANES, tm), F32)
    for g in range(N_GROUPS):
        sel = jnp.where(g_idx == g, logits[g * EXPERTS_PER_GROUP:(g + 1) * EXPERTS_PER_GROUP], sel)
    ee = jnp.exp(sel - jnp.max(sel, axis=0, keepdims=True))
    pe = ee / jnp.sum(ee, axis=0, keepdims=True)
    e1 = jnp.max(pe, axis=0, keepdims=True)
    i1 = _first_index_of(pe, e1, row_f)
    pe2 = jnp.where(row_f == i1, -1.0, pe)
    e2 = jnp.max(pe2, axis=0, keepdims=True)
    i2 = _first_index_of(pe2, e2, row_f)
    den = e1 + e2
    wt1 = g_val * e1 / den
    wt2 = g_val * e2 / den
    first_lo = i1 < i2
    lo = jnp.where(first_lo, i1, i2)
    hi = jnp.where(first_lo, i2, i1)
    w_lo = jnp.where(first_lo, wt1, wt2)
    w_hi = jnp.where(first_lo, wt2, wt1)
    pair = lo * (2 * EXPERTS_PER_GROUP - 1 - lo) * 0.5 + (hi - lo - 1.0)
    cls = g_idx * N_PAIRS + pair

    cls_row = lax.broadcasted_iota(jnp.int32, (CLS_ROWS, tm), 0).astype(F32)
    onehot = cls_row == cls
    onehot_b = jnp.where(onehot, 1.0, 0.0).astype(BF16)
    upto = (lax.broadcasted_iota(jnp.int32, (tm, tm), 0)
            <= lax.broadcasted_iota(jnp.int32, (tm, tm), 1)).astype(BF16)
    cum = _dot(onehot_b, upto)
    tot = _dot(onehot_b, jnp.ones((tm, LANES), BF16))
    carry = carry_ref[...]
    before = jnp.concatenate([carry] * (tm // LANES), axis=1)
    rank = jnp.sum(jnp.where(onehot, cum - 1.0 + before, 0.0), axis=0, keepdims=True)
    carry_ref[...] = carry + tot
    hist_ref[...] = carry + tot

    meta = jnp.concatenate([cls, rank, jnp.zeros((SUBLANES - 2, tm), F32)], axis=0)
    meta_ref[0] = meta.astype(jnp.int32)

    w_rows = jnp.concatenate([w_lo, w_hi, jnp.zeros((LANES - 2, tm), F32)], axis=0)
    hext_ref[:, 0:D_MODEL] = h
    hext_ref[:, D_MODEL:] = jnp.transpose(w_rows)


def out_router(x2d, o_sb, o_rw, w_out, g_ffn, wr_both, wr_hi, b_r):
    T = x2d.shape[0]
    tm = min(OR_TM, T)
    nt = T // tm
    const = lambda *shape: pl.BlockSpec(shape, lambda i: (0,) * len(shape))
    return pl.pallas_call(
        functools.partial(_out_router_body, tm=tm),
        grid=(nt,),
        in_specs=[
            pl.BlockSpec((tm, D_MODEL), lambda i: (i, 0)),
            pl.BlockSpec((tm, SB_WIDTH), lambda i: (i, 0)),
            pl.BlockSpec((tm, RW_WIDTH), lambda i: (i, 0)),
            const(D_MODEL, D_MODEL), const(1, D_MODEL),
            const(2 * ROUTER_ROWS, D_MODEL), const(ROUTER_ROWS, D_MODEL), const(ROUTER_ROWS, LANES),
        ],
        out_specs=[
            pl.BlockSpec((tm, EXT_W), lambda i: (i, 0)),
            pl.BlockSpec((1, SUBLANES, tm), lambda i: (i, 0, 0)),
            const(CLS_ROWS, LANES),
        ],
        out_shape=[
            jax.ShapeDtypeStruct((T, EXT_W), F32),
            jax.ShapeDtypeStruct((nt, SUBLANES, tm), jnp.int32),
            jax.ShapeDtypeStruct((CLS_ROWS, LANES), F32),
        ],
        scratch_shapes=[pltpu.VMEM((CLS_ROWS, LANES), F32)],
        compiler_params=_cparams(("arbitrary",)),
        name="out_router",
    )(x2d, o_sb, o_rw, w_out, g_ffn, wr_both, wr_hi, b_r)


PERM_ROWS_PER_STEP = 1024
PERM_UNROLL = 8


def _row_permute_body(idx_ref, src_ref, dst_ref, sem, *, rows, scatter):
    s = pl.program_id(0)
    n_steps = pl.num_programs(0)

    def row_copy(src_row, dst_row):
        return pltpu.make_async_copy(src_ref.at[pl.ds(src_row, 1)], dst_ref.at[pl.ds(dst_row, 1)], sem)

    def issue(j, _):
        t = s * rows + j
        p = idx_ref[0, 0, j]
        row_copy(t, p).start() if scatter else row_copy(p, t).start()
        return _

    lax.fori_loop(0, rows, issue, 0, unroll=PERM_UNROLL)

    def drain(j, _):
        row_copy(0, 0).wait()
        return _

    @pl.when(s > 0)
    def _():
        lax.fori_loop(0, rows, drain, 0, unroll=PERM_UNROLL)

    @pl.when(s == n_steps - 1)
    def _():
        lax.fori_loop(0, rows, drain, 0, unroll=PERM_UNROLL)


def row_permute(src, idx, scatter):
    n, w = src.shape
    rows = min(PERM_ROWS_PER_STEP, n)
    steps = n // rows
    return pl.pallas_call(
        functools.partial(_row_permute_body, rows=rows, scatter=scatter),
        grid=(steps,),
        in_specs=[
            pl.BlockSpec((1, 1, rows), lambda s: (s, 0, 0), memory_space=pltpu.SMEM),
            pl.BlockSpec(memory_space=pl.ANY),
        ],
        out_specs=pl.BlockSpec(memory_space=pl.ANY),
        out_shape=jax.ShapeDtypeStruct((n, w), src.dtype),
        scratch_shapes=[pltpu.SemaphoreType.DMA(())],
        compiler_params=_cparams(("arbitrary",)),
        name="row_scatter" if scatter else "row_gather",
    )(idx.reshape(steps, 1, rows), src)


EX_TM = 128


def _experts_body(tile_ref, lo_ref, hi_ref, start_ref, end_ref,
                  hs_ref, gffn_ref, gfin_ref, wg_lo, wu_lo, wd_lo, wg_hi, wu_hi, wd_hi, o_ref, *, tm):
    i = pl.program_id(0)
    tile = tile_ref[i]
    prev_tile = tile_ref[jnp.maximum(i - 1, 0)]
    start = start_ref[i]
    end = end_ref[i]

    @pl.when(jnp.logical_or(i == 0, tile != prev_tile))
    def _():
        o_ref[...] = jnp.zeros_like(o_ref)

    @pl.when(end > start)
    def _():
        h = hs_ref[:, 0:D_MODEL]
        w_lo = hs_ref[:, D_MODEL:D_MODEL + 1]
        w_hi = hs_ref[:, D_MODEL + 1:D_MODEL + 2]
        ms = jnp.mean(h * h, axis=-1, keepdims=True)
        xn = (h * lax.rsqrt(ms + RMS_EPS) * gffn_ref[...]).astype(BF16)

        def mlp(wg, wu, wd):
            gate = _dot(xn, wg[0])
            up = _dot(xn, wu[0])
            act = gate * _sigmoid(gate) * up
            return _dot(act.astype(BF16), wd[0])

        y = w_lo * mlp(wg_lo, wu_lo, wd_lo) + w_hi * mlp(wg_hi, wu_hi, wd_hi)
        hf = h + y
        ms2 = jnp.mean(hf * hf, axis=-1, keepdims=True)
        res = hf * lax.rsqrt(ms2 + RMS_EPS) * gfin_ref[...]
        rows = tile * tm + lax.broadcasted_iota(jnp.int32, (tm, 1), 0)
        mine = jnp.logical_and(rows >= start, rows < end)
        o_ref[...] = jnp.where(mine, res, o_ref[...])


def experts(h_sorted, g_ffn, g_final, w_gate, w_up, w_down, item_tile, item_lo, item_hi, item_start, item_end):
    T = h_sorted.shape[0]
    tm = min(EX_TM, T)
    n_items = item_tile.shape[0]
    gate_spec = lambda which: pl.BlockSpec(
        (1, D_MODEL, D_EXPERT), lambda i, tl, lo, hi, st, en: ((lo, hi)[which][i], 0, 0))
    down_spec = lambda which: pl.BlockSpec(
        (1, D_EXPERT, D_MODEL), lambda i, tl, lo, hi, st, en: ((lo, hi)[which][i], 0, 0))
    vec_spec = pl.BlockSpec((1, D_MODEL), lambda i, tl, lo, hi, st, en: (0, 0))
    grid_spec = pltpu.PrefetchScalarGridSpec(
        num_scalar_prefetch=5,
        grid=(n_items,),
        in_specs=[
            pl.BlockSpec((tm, EXT_W), lambda i, tl, lo, hi, st, en: (tl[i], 0)),
            vec_spec, vec_spec,
            gate_spec(0), gate_spec(0), down_spec(0),
            gate_spec(1), gate_spec(1), down_spec(1),
        ],
        out_specs=pl.BlockSpec((tm, D_MODEL), lambda i, tl, lo, hi, st, en: (tl[i], 0)),
    )
    return pl.pallas_call(
        functools.partial(_experts_body, tm=tm),
        grid_spec=grid_spec,
        out_shape=jax.ShapeDtypeStruct((T, D_MODEL), F32),
        compiler_params=_cparams(("arbitrary",)),
        name="experts",
    )(item_tile, item_lo, item_hi, item_start, item_end,
      h_sorted, g_ffn, g_final, w_gate, w_up, w_down, w_gate, w_up, w_down)


def _class_tables():
    lo, hi = [], []
    for a in range(EXPERTS_PER_GROUP):
        for b in range(a + 1, EXPERTS_PER_GROUP):
            lo.append(a)
            hi.append(b)
    return jnp.array(lo, jnp.int32), jnp.array(hi, jnp.int32)


def _work_items(hist, meta, T, tm):
    counts = hist[:N_CLASSES, 0].astype(jnp.int32)
    offs = jnp.cumsum(counts) - counts
    cls = meta[:, 0, :].reshape(T)
    rank = meta[:, 1, :].reshape(T)
    pos = offs[cls] + rank
    n_tiles = T // tm
    starts = jnp.sort(jnp.concatenate([jnp.arange(n_tiles, dtype=jnp.int32) * tm, offs]))
    ends = jnp.concatenate([starts[1:], jnp.array([T], jnp.int32)])
    item_tile = jnp.minimum(starts // tm, n_tiles - 1)
    item_cls = jnp.clip(jnp.searchsorted(offs, starts, side="right").astype(jnp.int32) - 1, 0, N_CLASSES - 1)
    pair_lo, pair_hi = _class_tables()
    grp = item_cls // N_PAIRS
    item_lo = grp * EXPERTS_PER_GROUP + pair_lo[item_cls % N_PAIRS]
    item_hi = grp * EXPERTS_PER_GROUP + pair_hi[item_cls % N_PAIRS]
    return pos, item_tile, item_lo, item_hi, starts, ends


def kernel(x, norm_mix_g, w_in, shift_mu, sb_out_g, rw_w0, rw_w2, rw_a0, rw_a2, rw_g2, rw_k_k, rw_k_a, rw_r_k, rw_ln_w, rw_ln_b, w_out, norm_ffn_g, router_grp_w, router_grp_b, router_exp_w, router_exp_b, exp_w_gate, exp_w_up, exp_w_down, final_norm_g):
    B, S, D = x.shape
    T = B * S
    assert D == D_MODEL and w_in.shape[0] == 1, "one layer of width 1024 is what these kernels implement"
    l = 0
    row = lambda a: a.reshape(1, -1)
    x2d = x.reshape(T, D)

    w_in_b = w_in[l].astype(BF16)
    u_sb, u_rw = in_proj(x2d, row(norm_mix_g[l]), w_in_b[:, :SB_IN], w_in_b[:, SB_IN:])
    o_sb = sb_attn(u_sb, row(sb_out_g[l]), B, S)
    w2_pad = jnp.concatenate([rw_w2[l], jnp.zeros_like(rw_a2[l])], axis=0).astype(BF16)
    a2_pad = jnp.concatenate([jnp.zeros_like(rw_w2[l]), rw_a2[l]], axis=0).astype(BF16)
    o_rw = rwkv(u_rw, row(shift_mu[l]), row(rw_w0[l]), w2_pad, row(rw_a0[l]), a2_pad, rw_g2[l].astype(BF16),
                row(rw_k_k[l]), row(rw_k_a[l]), row(rw_r_k[l]), row(rw_ln_w[l]), row(rw_ln_b[l]), B, S)

    pad_rows = ROUTER_ROWS - N_EXPERTS - N_GROUPS
    wr = jnp.concatenate([router_exp_w[l].T, router_grp_w[l].T, jnp.zeros((pad_rows, D), F32)], axis=0)
    wr_hi, wr_lo = _split_bf16(wr)
    b_r = jnp.concatenate([router_exp_b[l], router_grp_b[l], jnp.zeros((pad_rows,), F32)])
    b_r = jnp.broadcast_to(b_r[:, None], (ROUTER_ROWS, LANES))
    h_ext, meta, hist = out_router(x2d, o_sb.reshape(T, SB_WIDTH), o_rw.reshape(T, RW_WIDTH),
                                   w_out[l].astype(BF16), row(norm_ffn_g[l]),
                                   jnp.concatenate([wr_hi, wr_lo], axis=0), wr_hi, b_r)

    tm = min(EX_TM, T)
    pos, item_tile, item_lo, item_hi, item_start, item_end = _work_items(hist, meta, T, tm)
    h_sorted = row_permute(h_ext, pos, scatter=True)
    out_sorted = experts(h_sorted, row(norm_ffn_g[l]), row(final_norm_g),
                         exp_w_gate[l].astype(BF16), exp_w_up[l].astype(BF16), exp_w_down[l].astype(BF16),
                         item_tile, item_lo, item_hi, item_start, item_end)
    out = row_permute(out_sorted, pos, scatter=False)
    return out.reshape(B, S, D)
```

```python
import functools
import math

import jax
import jax.numpy as jnp
from jax import lax
from jax.experimental import pallas as pl
from jax.experimental.pallas import tpu as pltpu

F32 = jnp.float32
BF16 = jnp.bfloat16

D_MODEL = 1024
SB_HEADS = 8
HEAD_DIM = 64
SB_WIDTH = SB_HEADS * HEAD_DIM
RW_HEADS = 8
RW_WIDTH = RW_HEADS * HEAD_DIM
LORA_W = 64
LORA_A = 64
LORA_G = 128
SB_IN = 3 * SB_WIDTH
RW_IN = 3 * RW_WIDTH + LORA_W + LORA_A + LORA_G
N_GROUPS = 4
EXPERTS_PER_GROUP = 8
N_EXPERTS = N_GROUPS * EXPERTS_PER_GROUP
D_EXPERT = 256
RMS_EPS = 1e-6
GN_EPS = 64e-5

LANES = 128
MXU_DIM = 256
VMEM_LIMIT = 48 * 1024 * 1024


def _cparams(sem):
    return pltpu.CompilerParams(dimension_semantics=sem, vmem_limit_bytes=VMEM_LIMIT)


def _dot(a, b):
    return jnp.dot(a, b, preferred_element_type=F32)


def _dot_nt(a, b):
    return lax.dot_general(a, b, (((1,), (1,)), ((), ())), preferred_element_type=F32)


def _dot_tn(a, b):
    return lax.dot_general(a, b, (((0,), (0,)), ((), ())), preferred_element_type=F32)


IN_TM = 512


def _in_proj_body(x_ref, g_ref, wsb_ref, wrw_ref, usb_ref, urw_ref):
    x = x_ref[...]
    ms = jnp.mean(x * x, axis=-1, keepdims=True)
    xn = (x * lax.rsqrt(ms + RMS_EPS) * g_ref[...]).astype(BF16)
    usb_ref[...] = _dot(xn, wsb_ref[...]).astype(BF16)
    urw_ref[...] = _dot(xn, wrw_ref[...]).astype(BF16)


def in_proj(x2d, g, w_sb, w_rw):
    T = x2d.shape[0]
    tm = min(IN_TM, T)
    return pl.pallas_call(
        _in_proj_body,
        grid=(T // tm,),
        in_specs=[
            pl.BlockSpec((tm, D_MODEL), lambda i: (i, 0)),
            pl.BlockSpec((1, D_MODEL), lambda i: (0, 0)),
            pl.BlockSpec((D_MODEL, SB_IN), lambda i: (0, 0)),
            pl.BlockSpec((D_MODEL, RW_IN), lambda i: (0, 0)),
        ],
        out_specs=[
            pl.BlockSpec((tm, SB_IN), lambda i: (i, 0)),
            pl.BlockSpec((tm, RW_IN), lambda i: (i, 0)),
        ],
        out_shape=[
            jax.ShapeDtypeStruct((T, SB_IN), BF16),
            jax.ShapeDtypeStruct((T, RW_IN), BF16),
        ],
        compiler_params=_cparams(("arbitrary",)),
        name="in_proj",
    )(x2d, g, w_sb, w_rw)


SB_BLK = 256
LOG2_E = 1.4426950408889634
SB_SKIP_LOG2 = -144.0


def _log2_one_minus_sigmoid(z2):
    return -(jnp.maximum(z2, 0.0) + jnp.log2(1.0 + jnp.exp2(-jnp.abs(z2))))


def _sb_attn_body(q_ref, k_ref, v_ref, g_ref, o_ref, *, blk):
    qb = pl.program_id(2)
    lane = lax.broadcasted_iota(jnp.int32, (1, LANES), 1)
    head_masks = [lane < HEAD_DIM, lane >= HEAD_DIM]
    heads = range(2)
    q = q_ref[0].astype(F32) * (LOG2_E / math.sqrt(HEAD_DIM))
    qh = [jnp.where(m, q, 0.0).astype(BF16) for m in head_masks]

    row = lax.broadcasted_iota(jnp.int32, (blk, blk), 0)
    col = lax.broadcasted_iota(jnp.int32, (blk, blk), 1)
    strict = col < row
    later = (row > col).astype(BF16)

    def kv_chunk(c):
        start = pl.multiple_of(c * blk, blk)
        return k_ref[0, pl.ds(start, blk), :], v_ref[0, pl.ds(start, blk), :]

    has_prev = qb >= 1
    kd, vd = kv_chunk(qb)
    kp, vp = kv_chunk(jnp.maximum(qb - 1, 0))
    z_d = [_dot_nt(qh[h], kd) for h in heads]
    z_p = [_dot_nt(qh[h], kp) for h in heads]
    rest_d = [jnp.where(strict, _log2_one_minus_sigmoid(z), 0.0) for z in z_d]
    rest_p = [_log2_one_minus_sigmoid(z) for z in z_p]
    suf_d = [_dot(x.astype(BF16), later) for x in rest_d]
    suf_p = [_dot(x.astype(BF16), later) for x in rest_p]
    carry_d = [jnp.sum(x, axis=-1, keepdims=True) for x in rest_d]
    w_d = [jnp.where(strict, jnp.exp2(x + z + s), 0.0) for x, z, s in zip(rest_d, z_d, suf_d)]
    w_p = [jnp.where(has_prev, jnp.exp2(x + z + s + cd), 0.0) for x, z, s, cd in zip(rest_p, z_p, suf_p, carry_d)]
    pv = [_dot(a.astype(BF16), vd) + _dot(b.astype(BF16), vp) for a, b in zip(w_d, w_p)]
    acc = jnp.where(head_masks[0], pv[0], pv[1])
    carry = [cd + jnp.sum(x, axis=-1, keepdims=True) for cd, x in zip(carry_d, rest_p)]

    def alive_of(carry):
        return jnp.max(jnp.maximum(carry[0], carry[1])) > SB_SKIP_LOG2

    def cond(st):
        c, alive = st[0], st[1]
        return jnp.logical_and(c >= 0, alive)

    def body(st):
        c, _, c0, c1, acc = st
        carry = [c0, c1]
        kc, vc = kv_chunk(c)
        z = [_dot_nt(qh[h], kc) for h in heads]
        rest = [_log2_one_minus_sigmoid(x) for x in z]
        suf = [_dot(x.astype(BF16), later) for x in rest]
        w = [jnp.exp2(x + zz + s + cr) for x, zz, s, cr in zip(rest, z, suf, carry)]
        pv = [_dot(x.astype(BF16), vc) for x in w]
        acc = acc + jnp.where(head_masks[0], pv[0], pv[1])
        carry = [cr + jnp.sum(x, axis=-1, keepdims=True) for cr, x in zip(carry, rest)]
        return c - 1, alive_of(carry), carry[0], carry[1], acc

    st = lax.while_loop(cond, body, (qb - 2, alive_of(carry), carry[0], carry[1], acc))
    acc = st[4]

    sq = acc * acc
    s_lo = jnp.sum(jnp.where(head_masks[0], sq, 0.0), axis=-1, keepdims=True)
    s_all = jnp.sum(sq, axis=-1, keepdims=True)
    ms = jnp.where(head_masks[0], s_lo, s_all - s_lo) * (1.0 / HEAD_DIM)
    o_ref[0] = (acc * lax.rsqrt(ms + RMS_EPS) * g_ref[...]).astype(BF16)


def sb_attn(u_sb, sb_out_g, B, S):
    blk = min(SB_BLK, S)
    u3 = u_sb.reshape(B, S, SB_IN)
    n_pairs = SB_WIDTH // LANES
    return pl.pallas_call(
        functools.partial(_sb_attn_body, blk=blk),
        grid=(B, n_pairs, S // blk),
        in_specs=[
            pl.BlockSpec((1, blk, LANES), lambda b, p, i: (b, i, p)),
            pl.BlockSpec((1, S, LANES), lambda b, p, i: (b, 0, n_pairs + p)),
            pl.BlockSpec((1, S, LANES), lambda b, p, i: (b, 0, 2 * n_pairs + p)),
            pl.BlockSpec((1, LANES), lambda b, p, i: (0, p)),
        ],
        out_specs=pl.BlockSpec((1, blk, LANES), lambda b, p, i: (b, i, p)),
        out_shape=jax.ShapeDtypeStruct((B, S, SB_WIDTH), BF16),
        compiler_params=_cparams(("arbitrary", "arbitrary", "arbitrary")),
        name="sb_attn",
    )(u3, u3, u3, sb_out_g)


RW_CHUNK = LANES
RW_SEQS_PER_STEP = 2
RW_GROUP = MXU_DIM
RW_GROUP_HEADS = RW_GROUP // HEAD_DIM


def _softplus(y):
    return jnp.maximum(y, 0.0) + jnp.log(1.0 + jnp.exp(-jnp.abs(y)))


def _sigmoid(y):
    return 1.0 / (1.0 + jnp.exp(-y))


def _split_bf16(x):
    hi = x.astype(BF16)
    lo = (x - hi.astype(F32)).astype(BF16)
    return hi, lo


def _rwkv_body(u_ref, mu_ref, w0_ref, w2_ref, a0_ref, a2_ref, g2_ref, kk_ref, ka_ref, rk_ref,
               lnw_ref, lnb_ref, o_ref, prev_ref, state_ref, *, C, n_seqs):
    c = pl.program_id(1)
    G, GH = RW_GROUP, RW_GROUP_HEADS
    n_groups = RW_WIDTH // G

    @pl.when(c == 0)
    def _():
        prev_ref[...] = jnp.zeros_like(prev_ref)
        state_ref[...] = jnp.zeros_like(state_ref)

    head_bd = (lax.broadcasted_iota(jnp.int32, (G, G), 0) // HEAD_DIM
               == lax.broadcasted_iota(jnp.int32, (G, G), 1) // HEAD_DIM)
    ones_bd = head_bd.astype(BF16)
    stack_mask = (lax.broadcasted_iota(jnp.int32, (GH * C, G), 0) // C
                  == lax.broadcasted_iota(jnp.int32, (GH * C, G), 1) // HEAD_DIM)
    low_half = lax.broadcasted_iota(jnp.int32, (1, LANES), 1) < HEAD_DIM
    tt = lax.broadcasted_iota(jnp.int32, (C, GH * C), 0)
    ss = lax.broadcasted_iota(jnp.int32, (C, GH * C), 1) % C
    strict = ss < tt
    incl = ss <= tt
    tri_incl = (lax.broadcasted_iota(jnp.int32, (C, C), 1)
                <= lax.broadcasted_iota(jnp.int32, (C, C), 0)).astype(BF16)

    def head_sum(x):
        return _dot(x.astype(BF16), ones_bd)

    def stack(x):
        return jnp.where(stack_mask, jnp.concatenate([x] * GH, axis=0), 0.0).astype(BF16)

    def swap_halves(x):
        return pltpu.roll(x, HEAD_DIM, axis=1)

    seqs = range(n_seqs)
    units = [(n, gi) for n in seqs for gi in range(n_groups)]
    lanes_of = lambda gi: slice(gi * G, (gi + 1) * G)

    ums = []
    for n in seqs:
        u = u_ref[n].astype(F32)
        row_id = lax.broadcasted_iota(jnp.int32, (C, 1), 0)
        shifted = jnp.where(row_id == 0, prev_ref[n], pltpu.roll(u, 1, axis=0))
        prev_ref[n] = u[C - 1:C, :]
        ums.append(u + (shifted - u) * mu_ref[...])
    r = [um[:, 0:RW_WIDTH] for um in ums]
    k = [um[:, RW_WIDTH:2 * RW_WIDTH] for um in ums]
    v = [um[:, 2 * RW_WIDTH:3 * RW_WIDTH] for um in ums]
    xwa = [um[:, 3 * RW_WIDTH:3 * RW_WIDTH + LORA_W + LORA_A] for um in ums]
    xg = [um[:, 3 * RW_WIDTH + LORA_W + LORA_A:] for um in ums]

    lora_w = [_dot(jnp.tanh(x).astype(BF16), w2_ref[...]) for x in xwa]
    lora_a = [_dot(x.astype(BF16), a2_ref[...]) for x in xwa]
    gate = [_dot(_sigmoid(x).astype(BF16), g2_ref[...]) for x in xg]
    logdec = [-jnp.exp(-_softplus(-(w0_ref[...] + lw)) - 0.5) for lw in lora_w]
    lr = [_sigmoid(a0_ref[...] + la) for la in lora_a]

    splits = [_split_bf16(ld) for ld in logdec]
    cum = [_dot(tri_incl, hi) + _dot(tri_incl, lo) for hi, lo in splits]
    p_incl = [jnp.exp(cm) for cm in cum]
    p_prev = [jnp.exp(cm - ld) for cm, ld in zip(cum, logdec)]
    p_inv = [jnp.exp(-cm) for cm in cum]
    p_last = [p[C - 1:C, :] for p in p_incl]

    kk = [kn * kk_ref[...] for kn in k]
    k_adj = [kn * (1.0 + (lrn - 1.0) * ka_ref[...]) for kn, lrn in zip(k, lr)]
    rk_prod = [rn * kan * rk_ref[...] for rn, kan in zip(r, k_adj)]

    kk_ssq = [head_sum(kk[n][:, lanes_of(gi)] * kk[n][:, lanes_of(gi)]) for n, gi in units]
    kkn = [kk[n][:, lanes_of(gi)] * lax.rsqrt(jnp.maximum(s, 1e-24)) for (n, gi), s in zip(units, kk_ssq)]
    v_g = [v[n][:, lanes_of(gi)] for n, gi in units]
    at = [-kn * p_prev[n][:, lanes_of(gi)] for (n, gi), kn in zip(units, kkn)]
    bt = [kn * lr[n][:, lanes_of(gi)] * p_inv[n][:, lanes_of(gi)] for (n, gi), kn in zip(units, kkn)]
    kt = [k_adj[n][:, lanes_of(gi)] * p_inv[n][:, lanes_of(gi)] for n, gi in units]
    rt = [r[n][:, lanes_of(gi)] * p_incl[n][:, lanes_of(gi)] for n, gi in units]

    lhs2 = [jnp.concatenate([a, q], axis=0).astype(BF16) for a, q in zip(at, rt)]
    ab = [_dot_nt(l2, stack(b)) for l2, b in zip(lhs2, bt)]
    ak = [_dot_nt(l2, stack(kx)) for l2, kx in zip(lhs2, kt)]
    a_ab = [jnp.where(strict, x[:C], 0.0) for x in ab]
    q_ab = [jnp.where(incl, x[C:], 0.0) for x in ab]
    a_ak = [jnp.where(strict, x[:C], 0.0) for x in ak]
    q_ak = [jnp.where(incl, x[C:], 0.0) for x in ak]
    vs = [stack(x) for x in v_g]
    akv = [_dot(a.astype(BF16), s) for a, s in zip(a_ak, vs)]

    n_steps = max(1, (C - 1).bit_length())
    tiles_per_group = G // LANES
    ys, lps = [], []
    for ui in range(len(units)):
        for p in range(tiles_per_group):
            at_p = at[ui][:, p * LANES:(p + 1) * LANES]
            akv_p = akv[ui][:, p * LANES:(p + 1) * LANES]
            ys.append(jnp.where(low_half, at_p, swap_halves(akv_p)))
            ys.append(jnp.where(low_half, swap_halves(at_p), akv_p))
            for e in range(2):
                h = 2 * p + e
                lps.append(a_ab[ui][:, h * C:(h + 1) * C])
    for step in range(n_steps):
        if step == n_steps - 1:
            ys = [y + _dot(lp.astype(BF16), y.astype(BF16)) for y, lp in zip(ys, lps)]
        else:
            prods = [_dot(lp.astype(BF16), jnp.concatenate([y, lp], axis=1).astype(BF16))
                     for y, lp in zip(ys, lps)]
            ys = [y + pr[:, 0:LANES] for y, pr in zip(ys, prods)]
            lps = [pr[:, LANES:] for pr in prods]
    y1, y2 = [], []
    for ui in range(len(units)):
        w_tiles, u0_tiles = [], []
        for p in range(tiles_per_group):
            y_even, y_odd = ys[ui * GH + 2 * p], ys[ui * GH + 2 * p + 1]
            w_tiles.append(jnp.where(low_half, y_even, swap_halves(y_odd)))
            u0_tiles.append(jnp.where(low_half, swap_halves(y_even), y_odd))
        y1.append(jnp.concatenate(w_tiles, axis=1))
        y2.append(jnp.concatenate(u0_tiles, axis=1))

    s0 = [state_ref[n * n_groups + gi] for n, gi in units]
    s0b = [s.astype(BF16) for s in s0]
    uu = [_dot_nt(a.astype(BF16), s) + b for a, s, b in zip(y1, s0b, y2)]
    o_state = [_dot_nt(q.astype(BF16), s) for q, s in zip(rt, s0b)]
    o_u = [_dot(q.astype(BF16), stack(x)) for q, x in zip(q_ab, uu)]
    o_v = [_dot(q.astype(BF16), s) for q, s in zip(q_ak, vs)]
    upd = [_dot_tn(jnp.concatenate([x, vv], axis=0).astype(BF16),
                   jnp.concatenate([b * p_last[n][:, lanes_of(gi)], kx * p_last[n][:, lanes_of(gi)]],
                                   axis=0).astype(BF16))
           for (n, gi), x, vv, b, kx in zip(units, uu, v_g, bt, kt)]
    for (n, gi), s, up in zip(units, s0, upd):
        state_ref[n * n_groups + gi] = s * p_last[n][:, lanes_of(gi)] + jnp.where(head_bd, up, 0.0)
    o_g = [a + b + c_ for a, b, c_ in zip(o_state, o_u, o_v)]

    mean = [head_sum(x) * (1.0 / HEAD_DIM) for x in o_g]
    dev = [x - m for x, m in zip(o_g, mean)]
    var = [head_sum(d * d) * (1.0 / HEAD_DIM) for d in dev]
    rk_sum = [head_sum(rk_prod[n][:, lanes_of(gi)]) for n, gi in units]
    outs = []
    for (n, gi), d, vr, rs, vv in zip(units, dev, var, rk_sum, v_g):
        sl = lanes_of(gi)
        gn = d * lax.rsqrt(vr + GN_EPS) * lnw_ref[:, sl] + lnb_ref[:, sl]
        outs.append((gn + rs * vv) * gate[n][:, sl])
    for n in seqs:
        o_ref[n] = jnp.concatenate(outs[n * n_groups:(n + 1) * n_groups], axis=1).astype(BF16)


def rwkv(u_rw, shift_mu, w0, w2_pad, a0, a2_pad, g2, k_k, k_a, r_k, ln_w, ln_b, B, S):
    C = RW_CHUNK
    n_seqs = RW_SEQS_PER_STEP if B % RW_SEQS_PER_STEP == 0 else 1
    assert S % C == 0
    u3 = u_rw.reshape(B, S, RW_IN)
    vec = lambda n: pl.BlockSpec((1, n), lambda b, c: (0, 0))
    mat = lambda m, n: pl.BlockSpec((m, n), lambda b, c: (0, 0))
    return pl.pallas_call(
        functools.partial(_rwkv_body, C=C, n_seqs=n_seqs),
        grid=(B // n_seqs, S // C),
        in_specs=[
            pl.BlockSpec((n_seqs, C, RW_IN), lambda b, c: (b, c, 0)),
            vec(RW_IN), vec(RW_WIDTH), mat(LORA_W + LORA_A, RW_WIDTH), vec(RW_WIDTH),
            mat(LORA_W + LORA_A, RW_WIDTH), mat(LORA_G, RW_WIDTH),
            vec(RW_WIDTH), vec(RW_WIDTH), vec(RW_WIDTH), vec(RW_WIDTH), vec(RW_WIDTH),
        ],
        out_specs=pl.BlockSpec((n_seqs, C, RW_WIDTH), lambda b, c: (b, c, 0)),
        out_shape=jax.ShapeDtypeStruct((B, S, RW_WIDTH), BF16),
        scratch_shapes=[
            pltpu.VMEM((n_seqs, 1, RW_IN), F32),
            pltpu.VMEM((n_seqs * (RW_WIDTH // RW_GROUP), RW_GROUP, RW_GROUP), F32),
        ],
        compiler_params=_cparams(("arbitrary", "arbitrary")),
        name="rwkv",
    )(u3, shift_mu, w0, w2_pad, a0, a2_pad, g2, k_k, k_a, r_k, ln_w, ln_b)


OR_TM = 512
N_PAIRS = EXPERTS_PER_GROUP * (EXPERTS_PER_GROUP - 1) // 2
N_CLASSES = N_GROUPS * N_PAIRS
CLS_ROWS = LANES
ROUTER_ROWS = 48
SUBLANES = 8
H_CHUNKS = D_MODEL // LANES
SLAB_IN = H_CHUNKS + 1
SLAB_OUT = H_CHUNKS


def _first_index_of(vals, target, row_f):
    return jnp.min(jnp.where(vals == target, row_f, 1e9), axis=0, keepdims=True)


def _out_router_body(x_ref, osb_ref, orw_ref, wo_ref, g_ref, wr_ref, wrhi_ref, br_ref,
                     slab_ref, meta_ref, hist_ref, carry_ref, *, tm):
    i = pl.program_id(0)

    @pl.when(i == 0)
    def _():
        carry_ref[...] = jnp.zeros_like(carry_ref)

    h = (x_ref[...] + _dot(osb_ref[...], wo_ref[0:SB_WIDTH, :])
         + _dot(orw_ref[...], wo_ref[SB_WIDTH:, :]))
    ms = jnp.mean(h * h, axis=-1, keepdims=True)
    xn = h * lax.rsqrt(ms + RMS_EPS) * g_ref[...]

    xn_hi, xn_lo = _split_bf16(xn)
    both = _dot_nt(wr_ref[...], xn_hi)
    logits = both[:ROUTER_ROWS] + both[ROUTER_ROWS:] + _dot_nt(wrhi_ref[...], xn_lo) + br_ref[...][:, 0:1]

    row_f = lax.broadcasted_iota(jnp.int32, (SUBLANES, tm), 0).astype(F32)
    lg = jnp.where(row_f < N_GROUPS, logits[N_EXPERTS:N_EXPERTS + SUBLANES], -jnp.inf)
    eg = jnp.exp(lg - jnp.max(lg, axis=0, keepdims=True))
    pg = eg / jnp.sum(eg, axis=0, keepdims=True)
    g_val = jnp.max(pg, axis=0, keepdims=True)
    g_idx = _first_index_of(pg, g_val, row_f)
    sel = jnp.zeros((SUBLANES, tm), F32)
    for g in range(N_GROUPS):
        sel = jnp.where(g_idx == g, logits[g * EXPERTS_PER_GROUP:(g + 1) * EXPERTS_PER_GROUP], sel)
    ee = jnp.exp(sel - jnp.max(sel, axis=0, keepdims=True))
    pe = ee / jnp.sum(ee, axis=0, keepdims=True)
    e1 = jnp.max(pe, axis=0, keepdims=True)
    i1 = _first_index_of(pe, e1, row_f)
    pe2 = jnp.where(row_f == i1, -1.0, pe)
    e2 = jnp.max(pe2, axis=0, keepdims=True)
    i2 = _first_index_of(pe2, e2, row_f)
    den = e1 + e2
    wt1 = g_val * e1 / den
    wt2 = g_val * e2 / den
    first_lo = i1 < i2
    lo = jnp.where(first_lo, i1, i2)
    hi = jnp.where(first_lo, i2, i1)
    w_lo = jnp.where(first_lo, wt1, wt2)
    w_hi = jnp.where(first_lo, wt2, wt1)
    pair = lo * (2 * EXPERTS_PER_GROUP - 1 - lo) * 0.5 + (hi - lo - 1.0)
    cls = g_idx * N_PAIRS + pair

    cls_row = lax.broadcasted_iota(jnp.int32, (CLS_ROWS, tm), 0).astype(F32)
    onehot = cls_row == cls
    onehot_b = jnp.where(onehot, 1.0, 0.0).astype(BF16)
    upto = (lax.broadcasted_iota(jnp.int32, (tm, tm), 0)
            <= lax.broadcasted_iota(jnp.int32, (tm, tm), 1)).astype(BF16)
    cum = _dot(onehot_b, upto)
    tot = _dot(onehot_b, jnp.ones((tm, LANES), BF16))
    carry = carry_ref[...]
    before = jnp.concatenate([carry] * (tm // LANES), axis=1)
    rank = jnp.sum(jnp.where(onehot, cum - 1.0 + before, 0.0), axis=0, keepdims=True)
    carry_ref[...] = carry + tot
    hist_ref[...] = carry + tot

    meta = jnp.concatenate([cls, rank, jnp.zeros((SUBLANES - 2, tm), F32)], axis=0)
    meta_ref[0] = meta.astype(jnp.int32)

    w_rows = jnp.concatenate([w_lo, w_hi, jnp.zeros((LANES - 2, tm), F32)], axis=0)
    for j in range(H_CHUNKS):
        slab_ref[pl.ds(j, tm, stride=SLAB_IN), :] = h[:, j * LANES:(j + 1) * LANES]
    slab_ref[pl.ds(H_CHUNKS, tm, stride=SLAB_IN), :] = jnp.transpose(w_rows)


def out_router(x2d, o_sb, o_rw, w_out, g_ffn, wr_both, wr_hi, b_r):
    T = x2d.shape[0]
    tm = min(OR_TM, T)
    nt = T // tm
    const = lambda *shape: pl.BlockSpec(shape, lambda i: (0,) * len(shape))
    return pl.pallas_call(
        functools.partial(_out_router_body, tm=tm),
        grid=(nt,),
        in_specs=[
            pl.BlockSpec((tm, D_MODEL), lambda i: (i, 0)),
            pl.BlockSpec((tm, SB_WIDTH), lambda i: (i, 0)),
            pl.BlockSpec((tm, RW_WIDTH), lambda i: (i, 0)),
            const(D_MODEL, D_MODEL), const(1, D_MODEL),
            const(2 * ROUTER_ROWS, D_MODEL), const(ROUTER_ROWS, D_MODEL), const(ROUTER_ROWS, LANES),
        ],
        out_specs=[
            pl.BlockSpec((tm * SLAB_IN, LANES), lambda i: (i, 0)),
            pl.BlockSpec((1, SUBLANES, tm), lambda i: (i, 0, 0)),
            const(CLS_ROWS, LANES),
        ],
        out_shape=[
            jax.ShapeDtypeStruct((T * SLAB_IN, LANES), F32),
            jax.ShapeDtypeStruct((nt, SUBLANES, tm), jnp.int32),
            jax.ShapeDtypeStruct((CLS_ROWS, LANES), F32),
        ],
        scratch_shapes=[pltpu.VMEM((CLS_ROWS, LANES), F32)],
        compiler_params=_cparams(("arbitrary",)),
        name="out_router",
    )(x2d, o_sb, o_rw, w_out, g_ffn, wr_both, wr_hi, b_r)


PERM_TOKENS_PER_STEP = 1024
PERM_UNROLL = 8


def _slab_scatter_body(pos_ref, src_ref, dst_ref, sem, *, tokens, slab):
    s = pl.program_id(0)
    n_steps = pl.num_programs(0)

    def slab_copy(src_tok, dst_tok):
        return pltpu.make_async_copy(src_ref.at[pl.ds(src_tok * slab, slab)],
                                     dst_ref.at[pl.ds(dst_tok * slab, slab)], sem)

    def issue(j, carry):
        slab_copy(s * tokens + j, pos_ref[0, 0, j]).start()
        return carry

    lax.fori_loop(0, tokens, issue, 0, unroll=PERM_UNROLL)

    def drain(j, carry):
        slab_copy(0, 0).wait()
        return carry

    @pl.when(s > 0)
    def _():
        lax.fori_loop(0, tokens, drain, 0, unroll=PERM_UNROLL)

    @pl.when(s == n_steps - 1)
    def _():
        lax.fori_loop(0, tokens, drain, 0, unroll=PERM_UNROLL)


def slab_scatter(src, pos, slab):
    n_tok = pos.shape[0]
    tokens = min(PERM_TOKENS_PER_STEP, n_tok)
    steps = n_tok // tokens
    return pl.pallas_call(
        functools.partial(_slab_scatter_body, tokens=tokens, slab=slab),
        grid=(steps,),
        in_specs=[
            pl.BlockSpec((1, 1, tokens), lambda s: (s, 0, 0), memory_space=pltpu.SMEM),
            pl.BlockSpec(memory_space=pl.ANY),
        ],
        out_specs=pl.BlockSpec(memory_space=pl.ANY),
        out_shape=jax.ShapeDtypeStruct(src.shape, src.dtype),
        scratch_shapes=[pltpu.SemaphoreType.DMA(())],
        compiler_params=_cparams(("arbitrary",)),
        name="slab_scatter",
    )(pos.reshape(steps, 1, tokens), src)


GATHER_TM = 512


def _slab_gather_body(pos_ref, nxt_ref, src_ref, o_ref, buf, sem, *, tm):
    i = pl.program_id(0)
    n_steps = pl.num_programs(0)
    slot = i % 2

    def slab_copy(tok, j, to_slot):
        start = pl.multiple_of(tok * SLAB_OUT, SLAB_OUT)
        return pltpu.make_async_copy(src_ref.at[pl.ds(start, SLAB_OUT)],
                                     buf.at[to_slot, pl.ds(j * SLAB_OUT, SLAB_OUT)], sem.at[to_slot])

    def fetch(idx_ref, to_slot):
        def issue(j, carry):
            slab_copy(idx_ref[0, 0, j], j, to_slot).start()
            return carry
        lax.fori_loop(0, tm, issue, 0, unroll=PERM_UNROLL)

    @pl.when(i == 0)
    def _():
        fetch(pos_ref, 0)

    @pl.when(i + 1 < n_steps)
    def _():
        fetch(nxt_ref, 1 - slot)

    def drain(j, carry):
        slab_copy(0, 0, slot).wait()
        return carry

    lax.fori_loop(0, tm, drain, 0, unroll=PERM_UNROLL)
    for j in range(H_CHUNKS):
        o_ref[:, j * LANES:(j + 1) * LANES] = buf[slot, pl.ds(j, tm, stride=SLAB_OUT), :]


def slab_gather(src, pos):
    n_tok = pos.shape[0]
    tm = min(GATHER_TM, n_tok)
    steps = n_tok // tm
    pos3 = pos.reshape(steps, 1, tm)
    return pl.pallas_call(
        functools.partial(_slab_gather_body, tm=tm),
        grid=(steps,),
        in_specs=[
            pl.BlockSpec((1, 1, tm), lambda i: (i, 0, 0), memory_space=pltpu.SMEM),
            pl.BlockSpec((1, 1, tm), lambda i: (jnp.minimum(i + 1, steps - 1), 0, 0), memory_space=pltpu.SMEM),
            pl.BlockSpec(memory_space=pl.ANY),
        ],
        out_specs=pl.BlockSpec((tm, D_MODEL), lambda i: (i, 0)),
        out_shape=jax.ShapeDtypeStruct((n_tok, D_MODEL), src.dtype),
        scratch_shapes=[
            pltpu.VMEM((2, tm * SLAB_OUT, LANES), src.dtype),
            pltpu.SemaphoreType.DMA((2,)),
        ],
        compiler_params=_cparams(("arbitrary",)),
        name="slab_gather",
    )(pos3, pos3, src)


EX_TM = 128


def _experts_body(tile_ref, lo_ref, hi_ref, start_ref, end_ref,
                  hs_ref, gffn_ref, gfin_ref, wg_lo, wu_lo, wd_lo, wg_hi, wu_hi, wd_hi, o_ref, *, tm):
    i = pl.program_id(0)
    tile = tile_ref[i]
    prev_tile = tile_ref[jnp.maximum(i - 1, 0)]
    start = start_ref[i]
    end = end_ref[i]

    @pl.when(jnp.logical_or(i == 0, tile != prev_tile))
    def _():
        o_ref[...] = jnp.zeros_like(o_ref)

    @pl.when(end > start)
    def _():
        h = jnp.concatenate([hs_ref[pl.ds(j, tm, stride=SLAB_IN), :] for j in range(H_CHUNKS)], axis=1)
        w_row = hs_ref[pl.ds(H_CHUNKS, tm, stride=SLAB_IN), :]
        w_lo = w_row[:, 0:1]
        w_hi = w_row[:, 1:2]
        ms = jnp.mean(h * h, axis=-1, keepdims=True)
        xn = (h * lax.rsqrt(ms + RMS_EPS) * gffn_ref[...]).astype(BF16)

        gates = [_dot(xn, wg_lo[0]), _dot(xn, wg_hi[0])]
        ups = [_dot(xn, wu_lo[0]), _dot(xn, wu_hi[0])]
        acts = [(g * _sigmoid(g) * u * w).astype(BF16) for g, u, w in zip(gates, ups, (w_lo, w_hi))]
        y = _dot(acts[0], wd_lo[0]) + _dot(acts[1], wd_hi[0])
        hf = h + y
        ms2 = jnp.mean(hf * hf, axis=-1, keepdims=True)
        res = hf * lax.rsqrt(ms2 + RMS_EPS) * gfin_ref[...]
        rows = tile * tm + lax.broadcasted_iota(jnp.int32, (tm, 1), 0)
        mine = jnp.logical_and(rows >= start, rows < end)
        for j in range(H_CHUNKS):
            sl = pl.ds(j, tm, stride=SLAB_OUT)
            o_ref[sl, :] = jnp.where(mine, res[:, j * LANES:(j + 1) * LANES], o_ref[sl, :])


def experts(h_sorted, g_ffn, g_final, w_gate, w_up, w_down, item_tile, item_lo, item_hi, item_start, item_end):
    T = h_sorted.shape[0] // SLAB_IN
    tm = min(EX_TM, T)
    n_items = item_tile.shape[0]
    gate_spec = lambda which: pl.BlockSpec(
        (1, D_MODEL, D_EXPERT), lambda i, tl, lo, hi, st, en: ((lo, hi)[which][i], 0, 0))
    down_spec = lambda which: pl.BlockSpec(
        (1, D_EXPERT, D_MODEL), lambda i, tl, lo, hi, st, en: ((lo, hi)[which][i], 0, 0))
    vec_spec = pl.BlockSpec((1, D_MODEL), lambda i, tl, lo, hi, st, en: (0, 0))
    grid_spec = pltpu.PrefetchScalarGridSpec(
        num_scalar_prefetch=5,
        grid=(n_items,),
        in_specs=[
            pl.BlockSpec((tm * SLAB_IN, LANES), lambda i, tl, lo, hi, st, en: (tl[i], 0)),
            vec_spec, vec_spec,
            gate_spec(0), gate_spec(0), down_spec(0),
            gate_spec(1), gate_spec(1), down_spec(1),
        ],
        out_specs=pl.BlockSpec((tm * SLAB_OUT, LANES), lambda i, tl, lo, hi, st, en: (tl[i], 0)),
    )
    return pl.pallas_call(
        functools.partial(_experts_body, tm=tm),
        grid_spec=grid_spec,
        out_shape=jax.ShapeDtypeStruct((T * SLAB_OUT, LANES), F32),
        compiler_params=_cparams(("arbitrary",)),
        name="experts",
    )(item_tile, item_lo, item_hi, item_start, item_end,
      h_sorted, g_ffn, g_final, w_gate, w_up, w_down, w_gate, w_up, w_down)


def _pair_first(lo):
    return (lo * (2 * EXPERTS_PER_GROUP - 1 - lo)) // 2


def _work_items(hist, meta, T, tm):
    i32 = jnp.int32
    counts = hist[:N_CLASSES, 0].astype(i32)
    offs = jnp.cumsum(counts) - counts
    cls = meta[:, 0, :].reshape(T)
    rank = meta[:, 1, :].reshape(T)
    class_ids = jnp.arange(N_CLASSES, dtype=i32)
    pos = jnp.sum(jnp.where(cls[:, None] == class_ids[None, :], offs[None, :], 0), axis=1) + rank

    n_tiles = T // tm
    n_items = n_tiles + N_CLASSES
    tile_starts = jnp.arange(n_tiles, dtype=i32) * tm
    tile_slot = jnp.arange(n_tiles, dtype=i32) + jnp.sum(offs[None, :] < tile_starts[:, None], axis=1, dtype=i32)
    class_slot = class_ids + jnp.sum(tile_starts[None, :] <= offs[:, None], axis=1, dtype=i32)
    slots = jnp.arange(n_items, dtype=i32)
    starts = (jnp.sum(jnp.where(tile_slot[None, :] == slots[:, None], tile_starts[None, :], 0), axis=1)
              + jnp.sum(jnp.where(class_slot[None, :] == slots[:, None], offs[None, :], 0), axis=1))
    ends = jnp.concatenate([starts[1:], jnp.full((1,), T, i32)])
    item_tile = jnp.minimum(starts // tm, n_tiles - 1)
    item_cls = jnp.sum(offs[None, :] <= starts[:, None], axis=1, dtype=i32) - 1
    grp = item_cls // N_PAIRS
    q = item_cls % N_PAIRS
    firsts = _pair_first(jnp.arange(1, EXPERTS_PER_GROUP - 1, dtype=i32))
    lo = jnp.sum(q[:, None] >= firsts[None, :], axis=1, dtype=i32)
    hi = q - _pair_first(lo) + lo + 1
    base = grp * EXPERTS_PER_GROUP
    return pos, item_tile, base + lo, base + hi, starts, ends


def kernel(x, norm_mix_g, w_in, shift_mu, sb_out_g, rw_w0, rw_w2, rw_a0, rw_a2, rw_g2, rw_k_k, rw_k_a, rw_r_k, rw_ln_w, rw_ln_b, w_out, norm_ffn_g, router_grp_w, router_grp_b, router_exp_w, router_exp_b, exp_w_gate, exp_w_up, exp_w_down, final_norm_g):
    B, S, D = x.shape
    T = B * S
    assert D == D_MODEL and w_in.shape[0] == 1, "one layer of width 1024 is what these kernels implement"
    l = 0
    row = lambda a: a.reshape(1, -1)
    x2d = x.reshape(T, D)

    w_in_b = w_in[l].astype(BF16)
    u_sb, u_rw = in_proj(x2d, row(norm_mix_g[l]), w_in_b[:, :SB_IN], w_in_b[:, SB_IN:])
    o_sb = sb_attn(u_sb, row(sb_out_g[l]), B, S)
    w2_pad = jnp.concatenate([rw_w2[l], jnp.zeros_like(rw_a2[l])], axis=0).astype(BF16)
    a2_pad = jnp.concatenate([jnp.zeros_like(rw_w2[l]), rw_a2[l]], axis=0).astype(BF16)
    o_rw = rwkv(u_rw, row(shift_mu[l]), row(rw_w0[l]), w2_pad, row(rw_a0[l]), a2_pad, rw_g2[l].astype(BF16),
                row(rw_k_k[l]), row(rw_k_a[l]), row(rw_r_k[l]), row(rw_ln_w[l]), row(rw_ln_b[l]), B, S)

    pad_rows = ROUTER_ROWS - N_EXPERTS - N_GROUPS
    wr = jnp.concatenate([router_exp_w[l].T, router_grp_w[l].T, jnp.zeros((pad_rows, D), F32)], axis=0)
    wr_hi, wr_lo = _split_bf16(wr)
    b_r = jnp.concatenate([router_exp_b[l], router_grp_b[l], jnp.zeros((pad_rows,), F32)])
    b_r = jnp.broadcast_to(b_r[:, None], (ROUTER_ROWS, LANES))
    h_slabs, meta, hist = out_router(x2d, o_sb.reshape(T, SB_WIDTH), o_rw.reshape(T, RW_WIDTH),
                                     w_out[l].astype(BF16), row(norm_ffn_g[l]),
                                     jnp.concatenate([wr_hi, wr_lo], axis=0), wr_hi, b_r)

    tm = min(EX_TM, T)
    pos, item_tile, item_lo, item_hi, item_start, item_end = _work_items(hist, meta, T, tm)
    h_sorted = slab_scatter(h_slabs, pos, SLAB_IN)
    out_sorted = experts(h_sorted, row(norm_ffn_g[l]), row(final_norm_g),
                         exp_w_gate[l].astype(BF16), exp_w_up[l].astype(BF16), exp_w_down[l].astype(BF16),
                         item_tile, item_lo, item_hi, item_start, item_end)
    out = slab_gather(out_sorted, pos)
    return out.reshape(B, S, D)
```

```python
import functools
import math

import jax
import jax.numpy as jnp
from jax import lax
from jax.experimental import pallas as pl
from jax.experimental.pallas import tpu as pltpu

F32 = jnp.float32
BF16 = jnp.bfloat16

D_MODEL = 1024
SB_HEADS = 8
HEAD_DIM = 64
SB_WIDTH = SB_HEADS * HEAD_DIM
RW_HEADS = 8
RW_WIDTH = RW_HEADS * HEAD_DIM
LORA_W = 64
LORA_A = 64
LORA_G = 128
SB_IN = 3 * SB_WIDTH
RW_IN = 3 * RW_WIDTH + LORA_W + LORA_A + LORA_G
N_GROUPS = 4
EXPERTS_PER_GROUP = 8
N_EXPERTS = N_GROUPS * EXPERTS_PER_GROUP
D_EXPERT = 256
RMS_EPS = 1e-6
GN_EPS = 64e-5

LANES = 128
MXU_DIM = 256
VMEM_LIMIT = 48 * 1024 * 1024


def _cparams(sem):
    return pltpu.CompilerParams(dimension_semantics=sem, vmem_limit_bytes=VMEM_LIMIT)


def _dot(a, b):
    return jnp.dot(a, b, preferred_element_type=F32)


def _dot_nt(a, b):
    return lax.dot_general(a, b, (((1,), (1,)), ((), ())), preferred_element_type=F32)


def _dot_tn(a, b):
    return lax.dot_general(a, b, (((0,), (0,)), ((), ())), preferred_element_type=F32)


IN_TM = 512


def _in_proj_body(x_ref, g_ref, wsb_ref, wrw_ref, usb_ref, urw_ref):
    x = x_ref[...]
    ms = jnp.mean(x * x, axis=-1, keepdims=True)
    xn = (x * lax.rsqrt(ms + RMS_EPS) * g_ref[...]).astype(BF16)
    usb_ref[...] = _dot(xn, wsb_ref[...]).astype(BF16)
    urw_ref[...] = _dot(xn, wrw_ref[...]).astype(BF16)


def in_proj(x2d, g, w_sb, w_rw):
    T = x2d.shape[0]
    tm = min(IN_TM, T)
    return pl.pallas_call(
        _in_proj_body,
        grid=(T // tm,),
        in_specs=[
            pl.BlockSpec((tm, D_MODEL), lambda i: (i, 0)),
            pl.BlockSpec((1, D_MODEL), lambda i: (0, 0)),
            pl.BlockSpec((D_MODEL, SB_IN), lambda i: (0, 0)),
            pl.BlockSpec((D_MODEL, RW_IN), lambda i: (0, 0)),
        ],
        out_specs=[
            pl.BlockSpec((tm, SB_IN), lambda i: (i, 0)),
            pl.BlockSpec((tm, RW_IN), lambda i: (i, 0)),
        ],
        out_shape=[
            jax.ShapeDtypeStruct((T, SB_IN), BF16),
            jax.ShapeDtypeStruct((T, RW_IN), BF16),
        ],
        compiler_params=_cparams(("arbitrary",)),
        name="in_proj",
    )(x2d, g, w_sb, w_rw)


SB_BLK = 256
SB_TILES_PER_STEP = 4
SB_SKIP = 100.0


def _decay(z):
    one = jnp.asarray(1.0, z.dtype)
    zero = jnp.asarray(0.0, z.dtype)
    return jnp.maximum(z, zero) + jnp.log(one + jnp.exp(-jnp.abs(z)))


def _sb_attn_body(q_ref, k_ref, v_ref, g_ref, later_ref, strict_ref, o_ref, *, blk, n_tiles):
    qb = pl.program_id(2)
    tiles = range(n_tiles)
    lane = lax.broadcasted_iota(jnp.int32, (1, LANES), 1)
    head_masks = [lane < HEAD_DIM, lane >= HEAD_DIM]
    lanes_of = lambda t: slice(t * LANES, (t + 1) * LANES)
    qs = []
    for t in tiles:
        q = q_ref[0, :, lanes_of(t)].astype(F32) * (1.0 / math.sqrt(HEAD_DIM))
        qs.append(jnp.concatenate([jnp.where(m, q, 0.0) for m in head_masks], axis=0).astype(BF16))

    later = later_ref[...]
    strict = strict_ref[...]

    def kv_chunk(c):
        start = pl.multiple_of(c * blk, blk)
        return ([k_ref[0, pl.ds(start, blk), lanes_of(t)] for t in tiles],
                [v_ref[0, pl.ds(start, blk), lanes_of(t)] for t in tiles])

    def unstack(pv):
        return jnp.where(head_masks[0], pv[:blk], pv[blk:])

    def row_sum(d, suf):
        return jnp.sum(d.astype(F32), axis=-1, keepdims=True)

    has_prev = qb >= 1
    kd, vd = kv_chunk(qb)
    kp, vp = kv_chunk(jnp.maximum(qb - 1, 0))
    z_d = [_dot_nt(q, k) for q, k in zip(qs, kd)]
    z_p = [_dot_nt(q, k) for q, k in zip(qs, kp)]
    d_full = [_decay(z.astype(BF16)) for z in z_d]
    d_d = [d * strict for d in d_full]
    d_p = [_decay(z.astype(BF16)) for z in z_p]
    suf_d = [_dot(d, later) for d in d_d]
    suf_p = [_dot(d, later) for d in d_p]
    carry_d = [row_sum(d, s) for d, s in zip(d_d, suf_d)]
    w_d = [jnp.exp(z - d.astype(F32) - s).astype(BF16) * strict for z, d, s in zip(z_d, d_full, suf_d)]
    w_p = [jnp.where(has_prev, jnp.exp(z - d.astype(F32) - s - cd), 0.0).astype(BF16)
           for z, d, s, cd in zip(z_p, d_p, suf_p, carry_d)]
    acc = [unstack(_dot(a, va) + _dot(b, vb)) for a, va, b, vb in zip(w_d, vd, w_p, vp)]
    carry = [cd + row_sum(d, s) for cd, d, s in zip(carry_d, d_p, suf_p)]

    def alive_of(carry):
        return functools.reduce(jnp.minimum, [jnp.min(c) for c in carry]) < SB_SKIP

    def cond(st):
        c, alive = st[0], st[1]
        return jnp.logical_and(c >= 0, alive)

    def body(st):
        c, _, carry, acc = st
        kc, vc = kv_chunk(c)
        z = [_dot_nt(q, k) for q, k in zip(qs, kc)]
        d = [_decay(x.astype(BF16)) for x in z]
        suf = [_dot(x, later) for x in d]
        w = [jnp.exp(zz - dd.astype(F32) - s - cr) for zz, dd, s, cr in zip(z, d, suf, carry)]
        acc = [a + unstack(_dot(x.astype(BF16), vv)) for a, x, vv in zip(acc, w, vc)]
        carry = [cr + row_sum(dd, s) for cr, dd, s in zip(carry, d, suf)]
        return c - 1, alive_of(carry), carry, acc

    st = lax.while_loop(cond, body, (qb - 2, alive_of(carry), carry, acc))
    acc = st[3]

    for t in tiles:
        sq = acc[t] * acc[t]
        s_lo = jnp.sum(jnp.where(head_masks[0], sq, 0.0), axis=-1, keepdims=True)
        s_all = jnp.sum(sq, axis=-1, keepdims=True)
        ms = jnp.where(head_masks[0], s_lo, s_all - s_lo) * (1.0 / HEAD_DIM)
        o_ref[0, :, lanes_of(t)] = (acc[t] * lax.rsqrt(ms + RMS_EPS) * g_ref[:, lanes_of(t)]).astype(BF16)


def sb_attn(u_sb, sb_out_g, B, S):
    blk = min(SB_BLK, S)
    u3 = u_sb.reshape(B, S, SB_IN)
    n_pairs = SB_WIDTH // LANES
    idx = jnp.arange(blk)
    later = (idx[:, None] > idx[None, :]).astype(BF16)
    strict = jnp.tile((idx[None, :] < idx[:, None]).astype(BF16), (2, 1))
    n_tiles = SB_TILES_PER_STEP
    width = n_tiles * LANES
    n_groups = n_pairs // n_tiles
    return pl.pallas_call(
        functools.partial(_sb_attn_body, blk=blk, n_tiles=n_tiles),
        grid=(B, n_groups, S // blk),
        in_specs=[
            pl.BlockSpec((1, blk, width), lambda b, p, i: (b, i, p)),
            pl.BlockSpec((1, S, width), lambda b, p, i: (b, 0, n_groups + p)),
            pl.BlockSpec((1, S, width), lambda b, p, i: (b, 0, 2 * n_groups + p)),
            pl.BlockSpec((1, width), lambda b, p, i: (0, p)),
            pl.BlockSpec((blk, blk), lambda b, p, i: (0, 0)),
            pl.BlockSpec((2 * blk, blk), lambda b, p, i: (0, 0)),
        ],
        out_specs=pl.BlockSpec((1, blk, width), lambda b, p, i: (b, i, p)),
        out_shape=jax.ShapeDtypeStruct((B, S, SB_WIDTH), BF16),
        compiler_params=_cparams(("arbitrary", "arbitrary", "arbitrary")),
        name="sb_attn",
    )(u3, u3, u3, sb_out_g, later, strict)


RW_CHUNK = LANES
RW_SEQS_PER_STEP = 2
RW_GROUP = MXU_DIM
RW_GROUP_HEADS = RW_GROUP // HEAD_DIM


def _softplus(y):
    return jnp.maximum(y, 0.0) + jnp.log(1.0 + jnp.exp(-jnp.abs(y)))


def _sigmoid(y):
    return 1.0 / (1.0 + jnp.exp(-y))


def _split_bf16(x):
    hi = x.astype(BF16)
    lo = (x - hi.astype(F32)).astype(BF16)
    return hi, lo


def _rwkv_body(u_ref, mu_ref, w0_ref, w2_ref, a0_ref, a2_ref, g2_ref, kk_ref, ka_ref, rk_ref,
               lnw_ref, lnb_ref, o_ref, prev_ref, state_ref, *, C, n_seqs):
    c = pl.program_id(1)
    G, GH = RW_GROUP, RW_GROUP_HEADS
    n_groups = RW_WIDTH // G

    @pl.when(c == 0)
    def _():
        prev_ref[...] = jnp.zeros_like(prev_ref)
        state_ref[...] = jnp.zeros_like(state_ref)

    head_bd = (lax.broadcasted_iota(jnp.int32, (G, G), 0) // HEAD_DIM
               == lax.broadcasted_iota(jnp.int32, (G, G), 1) // HEAD_DIM)
    ones_bd = head_bd.astype(BF16)
    stack_mask = (lax.broadcasted_iota(jnp.int32, (GH * C, G), 0) // C
                  == lax.broadcasted_iota(jnp.int32, (GH * C, G), 1) // HEAD_DIM)
    low_half = lax.broadcasted_iota(jnp.int32, (1, LANES), 1) < HEAD_DIM
    tt = lax.broadcasted_iota(jnp.int32, (C, GH * C), 0)
    ss = lax.broadcasted_iota(jnp.int32, (C, GH * C), 1) % C
    strict = ss < tt
    incl = ss <= tt
    tri_incl = (lax.broadcasted_iota(jnp.int32, (C, C), 1)
                <= lax.broadcasted_iota(jnp.int32, (C, C), 0)).astype(BF16)

    def head_sum(x):
        return _dot(x.astype(BF16), ones_bd)

    def stack(x):
        return jnp.where(stack_mask, jnp.concatenate([x] * GH, axis=0), 0.0).astype(BF16)

    def swap_halves(x):
        return pltpu.roll(x, HEAD_DIM, axis=1)

    seqs = range(n_seqs)
    units = [(n, gi) for n in seqs for gi in range(n_groups)]
    lanes_of = lambda gi: slice(gi * G, (gi + 1) * G)

    ums = []
    for n in seqs:
        u = u_ref[n].astype(F32)
        row_id = lax.broadcasted_iota(jnp.int32, (C, 1), 0)
        shifted = jnp.where(row_id == 0, prev_ref[n], pltpu.roll(u, 1, axis=0))
        prev_ref[n] = u[C - 1:C, :]
        ums.append(u + (shifted - u) * mu_ref[...])
    r = [um[:, 0:RW_WIDTH] for um in ums]
    k = [um[:, RW_WIDTH:2 * RW_WIDTH] for um in ums]
    v = [um[:, 2 * RW_WIDTH:3 * RW_WIDTH] for um in ums]
    xwa = [um[:, 3 * RW_WIDTH:3 * RW_WIDTH + LORA_W + LORA_A] for um in ums]
    xg = [um[:, 3 * RW_WIDTH + LORA_W + LORA_A:] for um in ums]

    lora_w = [_dot(jnp.tanh(x).astype(BF16), w2_ref[...]) for x in xwa]
    lora_a = [_dot(x.astype(BF16), a2_ref[...]) for x in xwa]
    gate = [_dot(_sigmoid(x).astype(BF16), g2_ref[...]) for x in xg]
    logdec = [-jnp.exp(-_softplus(-(w0_ref[...] + lw)) - 0.5) for lw in lora_w]
    lr = [_sigmoid(a0_ref[...] + la) for la in lora_a]

    splits = [_split_bf16(ld) for ld in logdec]
    cum = [_dot(tri_incl, hi) + _dot(tri_incl, lo) for hi, lo in splits]
    p_incl = [jnp.exp(cm) for cm in cum]
    p_prev = [jnp.exp(cm - ld) for cm, ld in zip(cum, logdec)]
    p_inv = [jnp.exp(-cm) for cm in cum]
    p_last = [p[C - 1:C, :] for p in p_incl]

    kk = [kn * kk_ref[...] for kn in k]
    k_adj = [kn * (1.0 + (lrn - 1.0) * ka_ref[...]) for kn, lrn in zip(k, lr)]
    rk_prod = [rn * kan * rk_ref[...] for rn, kan in zip(r, k_adj)]

    kk_ssq = [head_sum(kk[n][:, lanes_of(gi)] * kk[n][:, lanes_of(gi)]) for n, gi in units]
    kkn = [kk[n][:, lanes_of(gi)] * lax.rsqrt(jnp.maximum(s, 1e-24)) for (n, gi), s in zip(units, kk_ssq)]
    v_g = [v[n][:, lanes_of(gi)] for n, gi in units]
    at = [-kn * p_prev[n][:, lanes_of(gi)] for (n, gi), kn in zip(units, kkn)]
    bt = [kn * lr[n][:, lanes_of(gi)] * p_inv[n][:, lanes_of(gi)] for (n, gi), kn in zip(units, kkn)]
    kt = [k_adj[n][:, lanes_of(gi)] * p_inv[n][:, lanes_of(gi)] for n, gi in units]
    rt = [r[n][:, lanes_of(gi)] * p_incl[n][:, lanes_of(gi)] for n, gi in units]

    lhs2 = [jnp.concatenate([a, q], axis=0).astype(BF16) for a, q in zip(at, rt)]
    ab = [_dot_nt(l2, stack(b)) for l2, b in zip(lhs2, bt)]
    ak = [_dot_nt(l2, stack(kx)) for l2, kx in zip(lhs2, kt)]
    a_ab = [jnp.where(strict, x[:C], 0.0) for x in ab]
    q_ab = [jnp.where(incl, x[C:], 0.0) for x in ab]
    a_ak = [jnp.where(strict, x[:C], 0.0) for x in ak]
    q_ak = [jnp.where(incl, x[C:], 0.0) for x in ak]
    vs = [stack(x) for x in v_g]
    akv = [_dot(a.astype(BF16), s) for a, s in zip(a_ak, vs)]

    n_steps = max(1, (C - 1).bit_length())
    tiles_per_group = G // LANES
    ys, lps = [], []
    for ui in range(len(units)):
        for p in range(tiles_per_group):
            at_p = at[ui][:, p * LANES:(p + 1) * LANES]
            akv_p = akv[ui][:, p * LANES:(p + 1) * LANES]
            ys.append(jnp.where(low_half, at_p, swap_halves(akv_p)))
            ys.append(jnp.where(low_half, swap_halves(at_p), akv_p))
            for e in range(2):
                h = 2 * p + e
                lps.append(a_ab[ui][:, h * C:(h + 1) * C])
    for step in range(n_steps):
        if step == n_steps - 1:
            ys = [y + _dot(lp.astype(BF16), y.astype(BF16)) for y, lp in zip(ys, lps)]
        else:
            prods = [_dot(lp.astype(BF16), jnp.concatenate([y, lp], axis=1).astype(BF16))
                     for y, lp in zip(ys, lps)]
            ys = [y + pr[:, 0:LANES] for y, pr in zip(ys, prods)]
            lps = [pr[:, LANES:] for pr in prods]
    y1, y2 = [], []
    for ui in range(len(units)):
        w_tiles, u0_tiles = [], []
        for p in range(tiles_per_group):
            y_even, y_odd = ys[ui * GH + 2 * p], ys[ui * GH + 2 * p + 1]
            w_tiles.append(jnp.where(low_half, y_even, swap_halves(y_odd)))
            u0_tiles.append(jnp.where(low_half, swap_halves(y_even), y_odd))
        y1.append(jnp.concatenate(w_tiles, axis=1))
        y2.append(jnp.concatenate(u0_tiles, axis=1))

    s0 = [state_ref[n * n_groups + gi] for n, gi in units]
    s0b = [s.astype(BF16) for s in s0]
    uu = [_dot_nt(a.astype(BF16), s) + b for a, s, b in zip(y1, s0b, y2)]
    o_state = [_dot_nt(q.astype(BF16), s) for q, s in zip(rt, s0b)]
    o_u = [_dot(q.astype(BF16), stack(x)) for q, x in zip(q_ab, uu)]
    o_v = [_dot(q.astype(BF16), s) for q, s in zip(q_ak, vs)]
    upd = [_dot_tn(jnp.concatenate([x, vv], axis=0).astype(BF16),
                   jnp.concatenate([b * p_last[n][:, lanes_of(gi)], kx * p_last[n][:, lanes_of(gi)]],
                                   axis=0).astype(BF16))
           for (n, gi), x, vv, b, kx in zip(units, uu, v_g, bt, kt)]
    for (n, gi), s, up in zip(units, s0, upd):
        state_ref[n * n_groups + gi] = s * p_last[n][:, lanes_of(gi)] + jnp.where(head_bd, up, 0.0)
    o_g = [a + b + c_ for a, b, c_ in zip(o_state, o_u, o_v)]

    mean = [head_sum(x) * (1.0 / HEAD_DIM) for x in o_g]
    dev = [x - m for x, m in zip(o_g, mean)]
    var = [head_sum(d * d) * (1.0 / HEAD_DIM) for d in dev]
    rk_sum = [head_sum(rk_prod[n][:, lanes_of(gi)]) for n, gi in units]
    outs = []
    for (n, gi), d, vr, rs, vv in zip(units, dev, var, rk_sum, v_g):
        sl = lanes_of(gi)
        gn = d * lax.rsqrt(vr + GN_EPS) * lnw_ref[:, sl] + lnb_ref[:, sl]
        outs.append((gn + rs * vv) * gate[n][:, sl])
    for n in seqs:
        o_ref[n] = jnp.concatenate(outs[n * n_groups:(n + 1) * n_groups], axis=1).astype(BF16)


def rwkv(u_rw, shift_mu, w0, w2_pad, a0, a2_pad, g2, k_k, k_a, r_k, ln_w, ln_b, B, S):
    C = RW_CHUNK
    n_seqs = RW_SEQS_PER_STEP if B % RW_SEQS_PER_STEP == 0 else 1
    assert S % C == 0
    u3 = u_rw.reshape(B, S, RW_IN)
    vec = lambda n: pl.BlockSpec((1, n), lambda b, c: (0, 0))
    mat = lambda m, n: pl.BlockSpec((m, n), lambda b, c: (0, 0))
    return pl.pallas_call(
        functools.partial(_rwkv_body, C=C, n_seqs=n_seqs),
        grid=(B // n_seqs, S // C),
        in_specs=[
            pl.BlockSpec((n_seqs, C, RW_IN), lambda b, c: (b, c, 0)),
            vec(RW_IN), vec(RW_WIDTH), mat(LORA_W + LORA_A, RW_WIDTH), vec(RW_WIDTH),
            mat(LORA_W + LORA_A, RW_WIDTH), mat(LORA_G, RW_WIDTH),
            vec(RW_WIDTH), vec(RW_WIDTH), vec(RW_WIDTH), vec(RW_WIDTH), vec(RW_WIDTH),
        ],
        out_specs=pl.BlockSpec((n_seqs, C, RW_WIDTH), lambda b, c: (b, c, 0)),
        out_shape=jax.ShapeDtypeStruct((B, S, RW_WIDTH), BF16),
        scratch_shapes=[
            pltpu.VMEM((n_seqs, 1, RW_IN), F32),
            pltpu.VMEM((n_seqs * (RW_WIDTH // RW_GROUP), RW_GROUP, RW_GROUP), F32),
        ],
        compiler_params=_cparams(("arbitrary", "arbitrary")),
        name="rwkv",
    )(u3, shift_mu, w0, w2_pad, a0, a2_pad, g2, k_k, k_a, r_k, ln_w, ln_b)


OR_TM = 512
N_PAIRS = EXPERTS_PER_GROUP * (EXPERTS_PER_GROUP - 1) // 2
N_CLASSES = N_GROUPS * N_PAIRS
CLS_ROWS = LANES
ROUTER_ROWS = 48
SUBLANES = 8
H_CHUNKS = D_MODEL // LANES
SLAB_IN = H_CHUNKS + 1
SLAB_OUT = H_CHUNKS


def _first_index_of(vals, target, row_f):
    return jnp.min(jnp.where(vals == target, row_f, 1e9), axis=0, keepdims=True)


def _out_router_body(x_ref, osb_ref, orw_ref, wo_ref, g_ref, wr_ref, wrhi_ref, br_ref,
                     slab_ref, meta_ref, hist_ref, carry_ref, *, tm):
    i = pl.program_id(0)

    @pl.when(i == 0)
    def _():
        carry_ref[...] = jnp.zeros_like(carry_ref)

    h = (x_ref[...] + _dot(osb_ref[...], wo_ref[0:SB_WIDTH, :])
         + _dot(orw_ref[...], wo_ref[SB_WIDTH:, :]))
    ms = jnp.mean(h * h, axis=-1, keepdims=True)
    xn = h * lax.rsqrt(ms + RMS_EPS) * g_ref[...]

    xn_hi, xn_lo = _split_bf16(xn)
    both = _dot_nt(wr_ref[...], xn_hi)
    logits = both[:ROUTER_ROWS] + both[ROUTER_ROWS:] + _dot_nt(wrhi_ref[...], xn_lo) + br_ref[...][:, 0:1]

    row_f = lax.broadcasted_iota(jnp.int32, (SUBLANES, tm), 0).astype(F32)
    lg = jnp.where(row_f < N_GROUPS, logits[N_EXPERTS:N_EXPERTS + SUBLANES], -jnp.inf)
    eg = jnp.exp(lg - jnp.max(lg, axis=0, keepdims=True))
    pg = eg / jnp.sum(eg, axis=0, keepdims=True)
    g_val = jnp.max(pg, axis=0, keepdims=True)
    g_idx = _first_index_of(pg, g_val, row_f)
    sel = jnp.zeros((SUBLANES, tm), F32)
    for g in range(N_GROUPS):
        sel = jnp.where(g_idx == g, logits[g * EXPERTS_PER_GROUP:(g + 1) * EXPERTS_PER_GROUP], sel)
    ee = jnp.exp(sel - jnp.max(sel, axis=0, keepdims=True))
    pe = ee / jnp.sum(ee, axis=0, keepdims=True)
    e1 = jnp.max(pe, axis=0, keepdims=True)
    i1 = _first_index_of(pe, e1, row_f)
    pe2 = jnp.where(row_f == i1, -1.0, pe)
    e2 = jnp.max(pe2, axis=0, keepdims=True)
    i2 = _first_index_of(pe2, e2, row_f)
    den = e1 + e2
    wt1 = g_val * e1 / den
    wt2 = g_val * e2 / den
    first_lo = i1 < i2
    lo = jnp.where(first_lo, i1, i2)
    hi = jnp.where(first_lo, i2, i1)
    w_lo = jnp.where(first_lo, wt1, wt2)
    w_hi = jnp.where(first_lo, wt2, wt1)
    pair = lo * (2 * EXPERTS_PER_GROUP - 1 - lo) * 0.5 + (hi - lo - 1.0)
    cls = g_idx * N_PAIRS + pair

    cls_row = lax.broadcasted_iota(jnp.int32, (CLS_ROWS, tm), 0).astype(F32)
    onehot = cls_row == cls
    onehot_b = jnp.where(onehot, 1.0, 0.0).astype(BF16)
    upto = (lax.broadcasted_iota(jnp.int32, (tm, tm), 0)
            <= lax.broadcasted_iota(jnp.int32, (tm, tm), 1)).astype(BF16)
    cum = _dot(onehot_b, upto)
    tot = _dot(onehot_b, jnp.ones((tm, LANES), BF16))
    carry = carry_ref[...]
    before = jnp.concatenate([carry] * (tm // LANES), axis=1)
    rank = jnp.sum(jnp.where(onehot, cum - 1.0 + before, 0.0), axis=0, keepdims=True)
    carry_ref[...] = carry + tot
    hist_ref[...] = carry + tot

    meta = jnp.concatenate([cls, rank, jnp.zeros((SUBLANES - 2, tm), F32)], axis=0)
    meta_ref[0] = meta.astype(jnp.int32)

    w_rows = jnp.concatenate([w_lo, w_hi, jnp.zeros((LANES - 2, tm), F32)], axis=0)
    for j in range(H_CHUNKS):
        slab_ref[pl.ds(j, tm, stride=SLAB_IN), :] = h[:, j * LANES:(j + 1) * LANES]
    slab_ref[pl.ds(H_CHUNKS, tm, stride=SLAB_IN), :] = jnp.transpose(w_rows)


def out_router(x2d, o_sb, o_rw, w_out, g_ffn, wr_both, wr_hi, b_r):
    T = x2d.shape[0]
    tm = min(OR_TM, T)
    nt = T // tm
    const = lambda *shape: pl.BlockSpec(shape, lambda i: (0,) * len(shape))
    return pl.pallas_call(
        functools.partial(_out_router_body, tm=tm),
        grid=(nt,),
        in_specs=[
            pl.BlockSpec((tm, D_MODEL), lambda i: (i, 0)),
            pl.BlockSpec((tm, SB_WIDTH), lambda i: (i, 0)),
            pl.BlockSpec((tm, RW_WIDTH), lambda i: (i, 0)),
            const(D_MODEL, D_MODEL), const(1, D_MODEL),
            const(2 * ROUTER_ROWS, D_MODEL), const(ROUTER_ROWS, D_MODEL), const(ROUTER_ROWS, LANES),
        ],
        out_specs=[
            pl.BlockSpec((tm * SLAB_IN, LANES), lambda i: (i, 0)),
            pl.BlockSpec((1, SUBLANES, tm), lambda i: (i, 0, 0)),
            const(CLS_ROWS, LANES),
        ],
        out_shape=[
            jax.ShapeDtypeStruct((T * SLAB_IN, LANES), F32),
            jax.ShapeDtypeStruct((nt, SUBLANES, tm), jnp.int32),
            jax.ShapeDtypeStruct((CLS_ROWS, LANES), F32),
        ],
        scratch_shapes=[pltpu.VMEM((CLS_ROWS, LANES), F32)],
        compiler_params=_cparams(("arbitrary",)),
        name="out_router",
    )(x2d, o_sb, o_rw, w_out, g_ffn, wr_both, wr_hi, b_r)


PERM_TOKENS_PER_STEP = 1024
PERM_WINDOW = 256
PERM_UNROLL = 8


def _slab_scatter_body(pos_ref, src_ref, dst_ref, sem, *, tokens, slab):
    window = min(PERM_WINDOW, tokens)

    def slab_copy(j, dst_tok):
        return pltpu.make_async_copy(src_ref.at[pl.ds(j * slab, slab)],
                                     dst_ref.at[pl.ds(dst_tok * slab, slab)], sem)

    def start(j, carry):
        slab_copy(j, pos_ref[0, 0, j]).start()
        return carry

    def start_and_retire(j, carry):
        slab_copy(j, pos_ref[0, 0, j]).start()
        slab_copy(0, 0).wait()
        return carry

    def retire(j, carry):
        slab_copy(0, 0).wait()
        return carry

    lax.fori_loop(0, window, start, 0, unroll=PERM_UNROLL)
    lax.fori_loop(window, tokens, start_and_retire, 0, unroll=PERM_UNROLL)
    lax.fori_loop(0, window, retire, 0, unroll=PERM_UNROLL)


def slab_scatter(src, pos, slab):
    n_tok = pos.shape[0]
    tokens = min(PERM_TOKENS_PER_STEP, n_tok)
    steps = n_tok // tokens
    return pl.pallas_call(
        functools.partial(_slab_scatter_body, tokens=tokens, slab=slab),
        grid=(steps,),
        in_specs=[
            pl.BlockSpec((1, 1, tokens), lambda s: (s, 0, 0), memory_space=pltpu.SMEM),
            pl.BlockSpec((tokens * slab, LANES), lambda s: (s, 0)),
        ],
        out_specs=pl.BlockSpec(memory_space=pl.ANY),
        out_shape=jax.ShapeDtypeStruct(src.shape, src.dtype),
        scratch_shapes=[pltpu.SemaphoreType.DMA(())],
        compiler_params=_cparams(("arbitrary",)),
        name="slab_scatter",
    )(pos.reshape(steps, 1, tokens), src)


GATHER_TM = 512


def _slab_gather_body(pos_ref, nxt_ref, src_ref, g_ref, o_ref, buf, sem, *, tm):
    i = pl.program_id(0)
    n_steps = pl.num_programs(0)
    slot = i % 2

    def slab_copy(tok, j, to_slot):
        start = pl.multiple_of(tok * SLAB_OUT, SLAB_OUT)
        return pltpu.make_async_copy(src_ref.at[pl.ds(start, SLAB_OUT)],
                                     buf.at[to_slot, pl.ds(j * SLAB_OUT, SLAB_OUT)], sem.at[to_slot])

    def fetch(idx_ref, to_slot):
        def issue(j, carry):
            slab_copy(idx_ref[0, 0, j], j, to_slot).start()
            return carry
        lax.fori_loop(0, tm, issue, 0, unroll=PERM_UNROLL)

    @pl.when(i == 0)
    def _():
        fetch(pos_ref, 0)

    @pl.when(i + 1 < n_steps)
    def _():
        fetch(nxt_ref, 1 - slot)

    def drain(j, carry):
        slab_copy(0, 0, slot).wait()
        return carry

    lax.fori_loop(0, tm, drain, 0, unroll=PERM_UNROLL)
    chunks = [buf[slot, pl.ds(j, tm, stride=SLAB_OUT), :] for j in range(H_CHUNKS)]
    ssq = sum(jnp.sum(x * x, axis=-1, keepdims=True) for x in chunks)
    scale = lax.rsqrt(ssq * (1.0 / D_MODEL) + RMS_EPS)
    for j in range(H_CHUNKS):
        o_ref[:, j * LANES:(j + 1) * LANES] = chunks[j] * scale * g_ref[:, j * LANES:(j + 1) * LANES]


def slab_gather(src, pos, g_final):
    n_tok = pos.shape[0]
    tm = min(GATHER_TM, n_tok)
    steps = n_tok // tm
    pos3 = pos.reshape(steps, 1, tm)
    return pl.pallas_call(
        functools.partial(_slab_gather_body, tm=tm),
        grid=(steps,),
        in_specs=[
            pl.BlockSpec((1, 1, tm), lambda i: (i, 0, 0), memory_space=pltpu.SMEM),
            pl.BlockSpec((1, 1, tm), lambda i: (jnp.minimum(i + 1, steps - 1), 0, 0), memory_space=pltpu.SMEM),
            pl.BlockSpec(memory_space=pl.ANY),
            pl.BlockSpec((1, D_MODEL), lambda i: (0, 0)),
        ],
        out_specs=pl.BlockSpec((tm, D_MODEL), lambda i: (i, 0)),
        out_shape=jax.ShapeDtypeStruct((n_tok, D_MODEL), src.dtype),
        scratch_shapes=[
            pltpu.VMEM((2, tm * SLAB_OUT, LANES), src.dtype),
            pltpu.SemaphoreType.DMA((2,)),
        ],
        compiler_params=_cparams(("arbitrary",)),
        name="slab_gather",
    )(pos3, pos3, src, g_final)


EX_TM = 256


def _experts_body(tile_ref, lo_ref, hi_ref, start_ref, end_ref,
                  hs_ref, gffn_ref, wg_lo, wu_lo, wd_lo, wg_hi, wu_hi, wd_hi, o_ref, *, tm):
    i = pl.program_id(0)
    tile = tile_ref[i]
    prev_tile = tile_ref[jnp.maximum(i - 1, 0)]
    start = start_ref[i]
    end = end_ref[i]

    @pl.when(jnp.logical_or(i == 0, tile != prev_tile))
    def _():
        o_ref[...] = jnp.zeros_like(o_ref)

    @pl.when(end > start)
    def _():
        h = jnp.concatenate([hs_ref[pl.ds(j, tm, stride=SLAB_IN), :] for j in range(H_CHUNKS)], axis=1)
        w_row = hs_ref[pl.ds(H_CHUNKS, tm, stride=SLAB_IN), :]
        w_lo = w_row[:, 0:1]
        w_hi = w_row[:, 1:2]
        ms = jnp.mean(h * h, axis=-1, keepdims=True)
        xn = (h * lax.rsqrt(ms + RMS_EPS) * gffn_ref[...]).astype(BF16)

        gates = [_dot(xn, wg_lo[0]), _dot(xn, wg_hi[0])]
        ups = [_dot(xn, wu_lo[0]), _dot(xn, wu_hi[0])]
        acts = [(g * _sigmoid(g) * u * w).astype(BF16) for g, u, w in zip(gates, ups, (w_lo, w_hi))]
        y = _dot(acts[0], wd_lo[0]) + _dot(acts[1], wd_hi[0])
        res = h + y
        rows = tile * tm + lax.broadcasted_iota(jnp.int32, (tm, 1), 0)
        mine = jnp.logical_and(rows >= start, rows < end)
        for j in range(H_CHUNKS):
            sl = pl.ds(j, tm, stride=SLAB_OUT)
            o_ref[sl, :] = jnp.where(mine, res[:, j * LANES:(j + 1) * LANES], o_ref[sl, :])


def experts(h_sorted, g_ffn, w_gate, w_up, w_down, item_tile, item_lo, item_hi, item_start, item_end):
    T = h_sorted.shape[0] // SLAB_IN
    tm = min(EX_TM, T)
    n_items = item_tile.shape[0]
    gate_spec = lambda which: pl.BlockSpec(
        (1, D_MODEL, D_EXPERT), lambda i, tl, lo, hi, st, en: ((lo, hi)[which][i], 0, 0))
    down_spec = lambda which: pl.BlockSpec(
        (1, D_EXPERT, D_MODEL), lambda i, tl, lo, hi, st, en: ((lo, hi)[which][i], 0, 0))
    vec_spec = pl.BlockSpec((1, D_MODEL), lambda i, tl, lo, hi, st, en: (0, 0))
    grid_spec = pltpu.PrefetchScalarGridSpec(
        num_scalar_prefetch=5,
        grid=(n_items,),
        in_specs=[
            pl.BlockSpec((tm * SLAB_IN, LANES), lambda i, tl, lo, hi, st, en: (tl[i], 0)),
            vec_spec,
            gate_spec(0), gate_spec(0), down_spec(0),
            gate_spec(1), gate_spec(1), down_spec(1),
        ],
        out_specs=pl.BlockSpec((tm * SLAB_OUT, LANES), lambda i, tl, lo, hi, st, en: (tl[i], 0)),
    )
    return pl.pallas_call(
        functools.partial(_experts_body, tm=tm),
        grid_spec=grid_spec,
        out_shape=jax.ShapeDtypeStruct((T * SLAB_OUT, LANES), F32),
        compiler_params=_cparams(("arbitrary",)),
        name="experts",
    )(item_tile, item_lo, item_hi, item_start, item_end,
      h_sorted, g_ffn, w_gate, w_up, w_down, w_gate, w_up, w_down)


def _pair_first(lo):
    return (lo * (2 * EXPERTS_PER_GROUP - 1 - lo)) // 2


def _work_items(hist, meta, T, tm):
    i32 = jnp.int32
    counts = hist[:N_CLASSES, 0].astype(i32)
    offs = jnp.cumsum(counts) - counts
    cls = meta[:, 0, :].reshape(T)
    rank = meta[:, 1, :].reshape(T)
    class_ids = jnp.arange(N_CLASSES, dtype=i32)
    pos = jnp.sum(jnp.where(cls[:, None] == class_ids[None, :], offs[None, :], 0), axis=1) + rank

    n_tiles = T // tm
    n_items = n_tiles + N_CLASSES
    tile_starts = jnp.arange(n_tiles, dtype=i32) * tm
    tile_slot = jnp.arange(n_tiles, dtype=i32) + jnp.sum(offs[None, :] < tile_starts[:, None], axis=1, dtype=i32)
    class_slot = class_ids + jnp.sum(tile_starts[None, :] <= offs[:, None], axis=1, dtype=i32)
    slots = jnp.arange(n_items, dtype=i32)
    starts = (jnp.sum(jnp.where(tile_slot[None, :] == slots[:, None], tile_starts[None, :], 0), axis=1)
              + jnp.sum(jnp.where(class_slot[None, :] == slots[:, None], offs[None, :], 0), axis=1))
    ends = jnp.concatenate([starts[1:], jnp.full((1,), T, i32)])
    item_tile = jnp.minimum(starts // tm, n_tiles - 1)
    item_cls = jnp.sum(offs[None, :] <= starts[:, None], axis=1, dtype=i32) - 1
    grp = item_cls // N_PAIRS
    q = item_cls % N_PAIRS
    firsts = _pair_first(jnp.arange(1, EXPERTS_PER_GROUP - 1, dtype=i32))
    lo = jnp.sum(q[:, None] >= firsts[None, :], axis=1, dtype=i32)
    hi = q - _pair_first(lo) + lo + 1
    base = grp * EXPERTS_PER_GROUP
    return pos, item_tile, base + lo, base + hi, starts, ends


def kernel(x, norm_mix_g, w_in, shift_mu, sb_out_g, rw_w0, rw_w2, rw_a0, rw_a2, rw_g2, rw_k_k, rw_k_a, rw_r_k, rw_ln_w, rw_ln_b, w_out, norm_ffn_g, router_grp_w, router_grp_b, router_exp_w, router_exp_b, exp_w_gate, exp_w_up, exp_w_down, final_norm_g):
    B, S, D = x.shape
    T = B * S
    assert D == D_MODEL and w_in.shape[0] == 1, "one layer of width 1024 is what these kernels implement"
    l = 0
    row = lambda a: a.reshape(1, -1)
    x2d = x.reshape(T, D)

    w_in_b = w_in[l].astype(BF16)
    u_sb, u_rw = in_proj(x2d, row(norm_mix_g[l]), w_in_b[:, :SB_IN], w_in_b[:, SB_IN:])
    o_sb = sb_attn(u_sb, row(sb_out_g[l]), B, S)
    w2_pad = jnp.concatenate([rw_w2[l], jnp.zeros_like(rw_a2[l])], axis=0).astype(BF16)
    a2_pad = jnp.concatenate([jnp.zeros_like(rw_w2[l]), rw_a2[l]], axis=0).astype(BF16)
    o_rw = rwkv(u_rw, row(shift_mu[l]), row(rw_w0[l]), w2_pad, row(rw_a0[l]), a2_pad, rw_g2[l].astype(BF16),
                row(rw_k_k[l]), row(rw_k_a[l]), row(rw_r_k[l]), row(rw_ln_w[l]), row(rw_ln_b[l]), B, S)

    pad_rows = ROUTER_ROWS - N_EXPERTS - N_GROUPS
    wr = jnp.concatenate([router_exp_w[l].T, router_grp_w[l].T, jnp.zeros((pad_rows, D), F32)], axis=0)
    wr_hi, wr_lo = _split_bf16(wr)
    b_r = jnp.concatenate([router_exp_b[l], router_grp_b[l], jnp.zeros((pad_rows,), F32)])
    b_r = jnp.broadcast_to(b_r[:, None], (ROUTER_ROWS, LANES))
    h_slabs, meta, hist = out_router(x2d, o_sb.reshape(T, SB_WIDTH), o_rw.reshape(T, RW_WIDTH),
                                     w_out[l].astype(BF16), row(norm_ffn_g[l]),
                                     jnp.concatenate([wr_hi, wr_lo], axis=0), wr_hi, b_r)

    tm = min(EX_TM, T)
    pos, item_tile, item_lo, item_hi, item_start, item_end = _work_items(hist, meta, T, tm)
    h_sorted = slab_scatter(h_slabs, pos, SLAB_IN)
    out_sorted = experts(h_sorted, row(norm_ffn_g[l]),
                         exp_w_gate[l].astype(BF16), exp_w_up[l].astype(BF16), exp_w_down[l].astype(BF16),
                         item_tile, item_lo, item_hi, item_start, item_end)
    out = slab_gather(out_sorted, pos, row(final_norm_g))
    return out.reshape(B, S, D)
```

```python
import functools
import math

import jax
import jax.numpy as jnp
from jax import lax
from jax.experimental import pallas as pl
from jax.experimental.pallas import tpu as pltpu

F32 = jnp.float32
BF16 = jnp.bfloat16

D_MODEL = 1024
SB_HEADS = 8
HEAD_DIM = 64
SB_WIDTH = SB_HEADS * HEAD_DIM
RW_HEADS = 8
RW_WIDTH = RW_HEADS * HEAD_DIM
LORA_W = 64
LORA_A = 64
LORA_G = 128
SB_IN = 3 * SB_WIDTH
RW_IN = 3 * RW_WIDTH + LORA_W + LORA_A + LORA_G
N_GROUPS = 4
EXPERTS_PER_GROUP = 8
N_EXPERTS = N_GROUPS * EXPERTS_PER_GROUP
D_EXPERT = 256
RMS_EPS = 1e-6
GN_EPS = 64e-5

LANES = 128
MXU_DIM = 256
VMEM_LIMIT = 48 * 1024 * 1024


def _cparams(sem):
    return pltpu.CompilerParams(dimension_semantics=sem, vmem_limit_bytes=VMEM_LIMIT)


def _dot(a, b):
    return jnp.dot(a, b, preferred_element_type=F32)


def _dot_nt(a, b):
    return lax.dot_general(a, b, (((1,), (1,)), ((), ())), preferred_element_type=F32)


def _dot_tn(a, b):
    return lax.dot_general(a, b, (((0,), (0,)), ((), ())), preferred_element_type=F32)


IN_TM = 512


def _in_proj_body(x_ref, g_ref, wsb_ref, wrw_ref, usb_ref, urw_ref):
    x = x_ref[...]
    ms = jnp.mean(x * x, axis=-1, keepdims=True)
    xn = (x * lax.rsqrt(ms + RMS_EPS) * g_ref[...]).astype(BF16)
    usb_ref[...] = _dot(xn, wsb_ref[...]).astype(BF16)
    urw_ref[...] = _dot(xn, wrw_ref[...]).astype(BF16)


def in_proj(x2d, g, w_sb, w_rw):
    T = x2d.shape[0]
    tm = min(IN_TM, T)
    return pl.pallas_call(
        _in_proj_body,
        grid=(T // tm,),
        in_specs=[
            pl.BlockSpec((tm, D_MODEL), lambda i: (i, 0)),
            pl.BlockSpec((1, D_MODEL), lambda i: (0, 0)),
            pl.BlockSpec((D_MODEL, SB_IN), lambda i: (0, 0)),
            pl.BlockSpec((D_MODEL, RW_IN), lambda i: (0, 0)),
        ],
        out_specs=[
            pl.BlockSpec((tm, SB_IN), lambda i: (i, 0)),
            pl.BlockSpec((tm, RW_IN), lambda i: (i, 0)),
        ],
        out_shape=[
            jax.ShapeDtypeStruct((T, SB_IN), BF16),
            jax.ShapeDtypeStruct((T, RW_IN), BF16),
        ],
        compiler_params=_cparams(("arbitrary",)),
        name="in_proj",
    )(x2d, g, w_sb, w_rw)


SB_BLK = 256
SB_TILES_PER_STEP = 4
SB_SKIP = 100.0


def _decay(z):
    one = jnp.asarray(1.0, z.dtype)
    zero = jnp.asarray(0.0, z.dtype)
    return jnp.maximum(z, zero) + jnp.log(one + jnp.exp(-jnp.abs(z)))


def _sb_attn_body(q_ref, k_ref, v_ref, g_ref, later_ref, strict_ref, o_ref, *, blk, n_tiles):
    qb = pl.program_id(2)
    tiles = range(n_tiles)
    lane = lax.broadcasted_iota(jnp.int32, (1, LANES), 1)
    head_masks = [lane < HEAD_DIM, lane >= HEAD_DIM]
    lanes_of = lambda t: slice(t * LANES, (t + 1) * LANES)
    qs = []
    for t in tiles:
        q = q_ref[0, :, lanes_of(t)].astype(F32) * (1.0 / math.sqrt(HEAD_DIM))
        qs.append(jnp.concatenate([jnp.where(m, q, 0.0) for m in head_masks], axis=0).astype(BF16))

    later = later_ref[...]
    strict = strict_ref[...]

    def kv_chunk(c):
        start = pl.multiple_of(c * blk, blk)
        return ([k_ref[0, pl.ds(start, blk), lanes_of(t)] for t in tiles],
                [v_ref[0, pl.ds(start, blk), lanes_of(t)] for t in tiles])

    def unstack(pv):
        return jnp.where(head_masks[0], pv[:blk], pv[blk:])

    def row_sum(d):
        return jnp.sum(d.astype(F32), axis=-1, keepdims=True)

    has_prev = qb >= 1
    kd, vd = kv_chunk(qb)
    kp, vp = kv_chunk(jnp.maximum(qb - 1, 0))
    z_d = [_dot_nt(q, k) for q, k in zip(qs, kd)]
    z_p = [_dot_nt(q, k) for q, k in zip(qs, kp)]
    d_full = [_decay(z.astype(BF16)) for z in z_d]
    d_d = [d * strict for d in d_full]
    d_p = [_decay(z.astype(BF16)) for z in z_p]
    suf_d = [_dot(d, later) for d in d_d]
    suf_p = [_dot(d, later) for d in d_p]
    carry_d = [row_sum(d) for d, s in zip(d_d, suf_d)]
    w_d = [jnp.exp(jnp.minimum(z - s, 0.0)).astype(BF16) * strict for z, s in zip(z_d, suf_d)]
    w_p = [jnp.where(has_prev, jnp.exp(z - s - cd), 0.0).astype(BF16)
           for z, s, cd in zip(z_p, suf_p, carry_d)]
    acc = [unstack(_dot(a, va) + _dot(b, vb)) for a, va, b, vb in zip(w_d, vd, w_p, vp)]
    carry = [cd + row_sum(d) for cd, d, s in zip(carry_d, d_p, suf_p)]

    def alive_of(carry):
        return functools.reduce(jnp.minimum, [jnp.min(c) for c in carry]) < SB_SKIP

    def cond(st):
        c, alive = st[0], st[1]
        return jnp.logical_and(c >= 0, alive)

    def body(st):
        c, _, carry, acc = st
        kc, vc = kv_chunk(c)
        z = [_dot_nt(q, k) for q, k in zip(qs, kc)]
        d = [_decay(x.astype(BF16)) for x in z]
        suf = [_dot(x, later) for x in d]
        w = [jnp.exp(zz - s - cr) for zz, s, cr in zip(z, suf, carry)]
        acc = [a + unstack(_dot(x.astype(BF16), vv)) for a, x, vv in zip(acc, w, vc)]
        carry = [cr + row_sum(dd) for cr, dd, s in zip(carry, d, suf)]
        return c - 1, alive_of(carry), carry, acc

    st = lax.while_loop(cond, body, (qb - 2, alive_of(carry), carry, acc))
    acc = st[3]

    for t in tiles:
        sq = acc[t] * acc[t]
        s_lo = jnp.sum(jnp.where(head_masks[0], sq, 0.0), axis=-1, keepdims=True)
        s_all = jnp.sum(sq, axis=-1, keepdims=True)
        ms = jnp.where(head_masks[0], s_lo, s_all - s_lo) * (1.0 / HEAD_DIM)
        o_ref[0, :, lanes_of(t)] = (acc[t] * lax.rsqrt(ms + RMS_EPS) * g_ref[:, lanes_of(t)]).astype(BF16)


def sb_attn(u_sb, sb_out_g, B, S):
    blk = min(SB_BLK, S)
    u3 = u_sb.reshape(B, S, SB_IN)
    n_pairs = SB_WIDTH // LANES
    idx = jnp.arange(blk)
    later = (idx[:, None] >= idx[None, :]).astype(BF16)
    strict = jnp.tile((idx[None, :] < idx[:, None]).astype(BF16), (2, 1))
    n_tiles = SB_TILES_PER_STEP
    width = n_tiles * LANES
    n_groups = n_pairs // n_tiles
    return pl.pallas_call(
        functools.partial(_sb_attn_body, blk=blk, n_tiles=n_tiles),
        grid=(B, n_groups, S // blk),
        in_specs=[
            pl.BlockSpec((1, blk, width), lambda b, p, i: (b, i, p)),
            pl.BlockSpec((1, S, width), lambda b, p, i: (b, 0, n_groups + p)),
            pl.BlockSpec((1, S, width), lambda b, p, i: (b, 0, 2 * n_groups + p)),
            pl.BlockSpec((1, width), lambda b, p, i: (0, p)),
            pl.BlockSpec((blk, blk), lambda b, p, i: (0, 0)),
            pl.BlockSpec((2 * blk, blk), lambda b, p, i: (0, 0)),
        ],
        out_specs=pl.BlockSpec((1, blk, width), lambda b, p, i: (b, i, p)),
        out_shape=jax.ShapeDtypeStruct((B, S, SB_WIDTH), BF16),
        compiler_params=_cparams(("arbitrary", "arbitrary", "arbitrary")),
        name="sb_attn",
    )(u3, u3, u3, sb_out_g, later, strict)


RW_CHUNK = LANES
RW_SEQS_PER_STEP = 4
RW_GROUP = MXU_DIM
RW_GROUP_HEADS = RW_GROUP // HEAD_DIM


def _softplus(y):
    return jnp.maximum(y, 0.0) + jnp.log(1.0 + jnp.exp(-jnp.abs(y)))


def _sigmoid(y):
    return 1.0 / (1.0 + jnp.exp(-y))


def _split_bf16(x):
    hi = x.astype(BF16)
    lo = (x - hi.astype(F32)).astype(BF16)
    return hi, lo


def _rwkv_body(u_ref, mu_ref, w0_ref, w2_ref, a0_ref, a2_ref, g2_ref, kk_ref, ka_ref, rk_ref,
               lnw_ref, lnb_ref, o_ref, prev_ref, state_ref, *, C, n_seqs):
    c = pl.program_id(1)
    G, GH = RW_GROUP, RW_GROUP_HEADS
    n_groups = RW_WIDTH // G

    @pl.when(c == 0)
    def _():
        prev_ref[...] = jnp.zeros_like(prev_ref)
        state_ref[...] = jnp.zeros_like(state_ref)

    head_bd = (lax.broadcasted_iota(jnp.int32, (G, G), 0) // HEAD_DIM
               == lax.broadcasted_iota(jnp.int32, (G, G), 1) // HEAD_DIM)
    ones_bd = head_bd.astype(BF16)
    stack_mask = (lax.broadcasted_iota(jnp.int32, (GH * C, G), 0) // C
                  == lax.broadcasted_iota(jnp.int32, (GH * C, G), 1) // HEAD_DIM)
    low_half = lax.broadcasted_iota(jnp.int32, (1, LANES), 1) < HEAD_DIM
    tt = lax.broadcasted_iota(jnp.int32, (C, GH * C), 0)
    ss = lax.broadcasted_iota(jnp.int32, (C, GH * C), 1) % C
    strict = ss < tt
    incl = ss <= tt
    tri_incl = (lax.broadcasted_iota(jnp.int32, (C, C), 1)
                <= lax.broadcasted_iota(jnp.int32, (C, C), 0)).astype(BF16)

    def head_sum(x):
        return _dot(x.astype(BF16), ones_bd)

    def stack(x):
        return jnp.where(stack_mask, jnp.concatenate([x] * GH, axis=0), 0.0).astype(BF16)

    def swap_halves(x):
        return pltpu.roll(x, HEAD_DIM, axis=1)

    seqs = range(n_seqs)
    units = [(n, gi) for n in seqs for gi in range(n_groups)]
    lanes_of = lambda gi: slice(gi * G, (gi + 1) * G)

    ums = []
    for n in seqs:
        u = u_ref[n].astype(F32)
        row_id = lax.broadcasted_iota(jnp.int32, (C, 1), 0)
        shifted = jnp.where(row_id == 0, prev_ref[n], pltpu.roll(u, 1, axis=0))
        prev_ref[n] = u[C - 1:C, :]
        ums.append(u + (shifted - u) * mu_ref[...])
    r = [um[:, 0:RW_WIDTH] for um in ums]
    k = [um[:, RW_WIDTH:2 * RW_WIDTH] for um in ums]
    v = [um[:, 2 * RW_WIDTH:3 * RW_WIDTH] for um in ums]
    xwa = [um[:, 3 * RW_WIDTH:3 * RW_WIDTH + LORA_W + LORA_A] for um in ums]
    xg = [um[:, 3 * RW_WIDTH + LORA_W + LORA_A:] for um in ums]

    lora_w = [_dot(jnp.tanh(x).astype(BF16), w2_ref[...]) for x in xwa]
    lora_a = [_dot(x.astype(BF16), a2_ref[...]) for x in xwa]
    gate = [_dot(_sigmoid(x).astype(BF16), g2_ref[...]) for x in xg]
    logdec = [-jnp.exp(-_softplus(-(w0_ref[...] + lw)) - 0.5) for lw in lora_w]
    lr = [_sigmoid(a0_ref[...] + la) for la in lora_a]

    splits = [_split_bf16(ld) for ld in logdec]
    cum = [_dot(tri_incl, hi) + _dot(tri_incl, lo) for hi, lo in splits]
    p_incl = [jnp.exp(cm) for cm in cum]
    p_prev = [jnp.exp(cm - ld) for cm, ld in zip(cum, logdec)]
    p_inv = [jnp.exp(-cm) for cm in cum]
    p_last = [p[C - 1:C, :] for p in p_incl]

    kk = [kn * kk_ref[...] for kn in k]
    k_adj = [kn * (1.0 + (lrn - 1.0) * ka_ref[...]) for kn, lrn in zip(k, lr)]
    rk_prod = [rn * kan * rk_ref[...] for rn, kan in zip(r, k_adj)]

    kk_ssq = [head_sum(kk[n][:, lanes_of(gi)] * kk[n][:, lanes_of(gi)]) for n, gi in units]
    kkn = [kk[n][:, lanes_of(gi)] * lax.rsqrt(jnp.maximum(s, 1e-24)) for (n, gi), s in zip(units, kk_ssq)]
    v_g = [v[n][:, lanes_of(gi)] for n, gi in units]
    at = [-kn * p_prev[n][:, lanes_of(gi)] for (n, gi), kn in zip(units, kkn)]
    bt = [kn * lr[n][:, lanes_of(gi)] * p_inv[n][:, lanes_of(gi)] for (n, gi), kn in zip(units, kkn)]
    kt = [k_adj[n][:, lanes_of(gi)] * p_inv[n][:, lanes_of(gi)] for n, gi in units]
    rt = [r[n][:, lanes_of(gi)] * p_incl[n][:, lanes_of(gi)] for n, gi in units]

    lhs2 = [jnp.concatenate([a, q], axis=0).astype(BF16) for a, q in zip(at, rt)]
    ab = [_dot_nt(l2, stack(b)) for l2, b in zip(lhs2, bt)]
    ak = [_dot_nt(l2, stack(kx)) for l2, kx in zip(lhs2, kt)]
    a_ab = [jnp.where(strict, x[:C], 0.0) for x in ab]
    q_ab = [jnp.where(incl, x[C:], 0.0) for x in ab]
    a_ak = [jnp.where(strict, x[:C], 0.0) for x in ak]
    q_ak = [jnp.where(incl, x[C:], 0.0) for x in ak]
    vs = [stack(x) for x in v_g]
    akv = [_dot(a.astype(BF16), s) for a, s in zip(a_ak, vs)]

    n_steps = max(1, (C - 1).bit_length())
    tiles_per_group = G // LANES
    ys, lps = [], []
    for ui in range(len(units)):
        for p in range(tiles_per_group):
            at_p = at[ui][:, p * LANES:(p + 1) * LANES]
            akv_p = akv[ui][:, p * LANES:(p + 1) * LANES]
            ys.append(jnp.where(low_half, at_p, swap_halves(akv_p)))
            ys.append(jnp.where(low_half, swap_halves(at_p), akv_p))
            for e in range(2):
                h = 2 * p + e
                lps.append(a_ab[ui][:, h * C:(h + 1) * C])
    for step in range(n_steps):
        if step == n_steps - 1:
            ys = [y + _dot(lp.astype(BF16), y.astype(BF16)) for y, lp in zip(ys, lps)]
        else:
            prods = [_dot(lp.astype(BF16), jnp.concatenate([y, lp], axis=1).astype(BF16))
                     for y, lp in zip(ys, lps)]
            ys = [y + pr[:, 0:LANES] for y, pr in zip(ys, prods)]
            lps = [pr[:, LANES:] for pr in prods]
    y1, y2 = [], []
    for ui in range(len(units)):
        w_tiles, u0_tiles = [], []
        for p in range(tiles_per_group):
            y_even, y_odd = ys[ui * GH + 2 * p], ys[ui * GH + 2 * p + 1]
            w_tiles.append(jnp.where(low_half, y_even, swap_halves(y_odd)))
            u0_tiles.append(jnp.where(low_half, swap_halves(y_even), y_odd))
        y1.append(jnp.concatenate(w_tiles, axis=1))
        y2.append(jnp.concatenate(u0_tiles, axis=1))

    s0 = [state_ref[n * n_groups + gi] for n, gi in units]
    s0b = [s.astype(BF16) for s in s0]
    uu = [_dot_nt(a.astype(BF16), s) + b for a, s, b in zip(y1, s0b, y2)]
    o_state = [_dot_nt(q.astype(BF16), s) for q, s in zip(rt, s0b)]
    o_u = [_dot(q.astype(BF16), stack(x)) for q, x in zip(q_ab, uu)]
    o_v = [_dot(q.astype(BF16), s) for q, s in zip(q_ak, vs)]
    upd = [_dot_tn(jnp.concatenate([x, vv], axis=0).astype(BF16),
                   jnp.concatenate([b * p_last[n][:, lanes_of(gi)], kx * p_last[n][:, lanes_of(gi)]],
                                   axis=0).astype(BF16))
           for (n, gi), x, vv, b, kx in zip(units, uu, v_g, bt, kt)]
    for (n, gi), s, up in zip(units, s0, upd):
        state_ref[n * n_groups + gi] = s * p_last[n][:, lanes_of(gi)] + jnp.where(head_bd, up, 0.0)
    o_g = [a + b + c_ for a, b, c_ in zip(o_state, o_u, o_v)]

    mean = [head_sum(x) * (1.0 / HEAD_DIM) for x in o_g]
    dev = [x - m for x, m in zip(o_g, mean)]
    var = [head_sum(d * d) * (1.0 / HEAD_DIM) for d in dev]
    rk_sum = [head_sum(rk_prod[n][:, lanes_of(gi)]) for n, gi in units]
    outs = []
    for (n, gi), d, vr, rs, vv in zip(units, dev, var, rk_sum, v_g):
        sl = lanes_of(gi)
        gn = d * lax.rsqrt(vr + GN_EPS) * lnw_ref[:, sl] + lnb_ref[:, sl]
        outs.append((gn + rs * vv) * gate[n][:, sl])
    for n in seqs:
        o_ref[n] = jnp.concatenate(outs[n * n_groups:(n + 1) * n_groups], axis=1).astype(BF16)


def rwkv(u_rw, shift_mu, w0, w2_pad, a0, a2_pad, g2, k_k, k_a, r_k, ln_w, ln_b, B, S):
    C = RW_CHUNK
    n_seqs = RW_SEQS_PER_STEP if B % RW_SEQS_PER_STEP == 0 else 1
    assert S % C == 0
    u3 = u_rw.reshape(B, S, RW_IN)
    vec = lambda n: pl.BlockSpec((1, n), lambda b, c: (0, 0))
    mat = lambda m, n: pl.BlockSpec((m, n), lambda b, c: (0, 0))
    return pl.pallas_call(
        functools.partial(_rwkv_body, C=C, n_seqs=n_seqs),
        grid=(B // n_seqs, S // C),
        in_specs=[
            pl.BlockSpec((n_seqs, C, RW_IN), lambda b, c: (b, c, 0)),
            vec(RW_IN), vec(RW_WIDTH), mat(LORA_W + LORA_A, RW_WIDTH), vec(RW_WIDTH),
            mat(LORA_W + LORA_A, RW_WIDTH), mat(LORA_G, RW_WIDTH),
            vec(RW_WIDTH), vec(RW_WIDTH), vec(RW_WIDTH), vec(RW_WIDTH), vec(RW_WIDTH),
        ],
        out_specs=pl.BlockSpec((n_seqs, C, RW_WIDTH), lambda b, c: (b, c, 0)),
        out_shape=jax.ShapeDtypeStruct((B, S, RW_WIDTH), BF16),
        scratch_shapes=[
            pltpu.VMEM((n_seqs, 1, RW_IN), F32),
            pltpu.VMEM((n_seqs * (RW_WIDTH // RW_GROUP), RW_GROUP, RW_GROUP), F32),
        ],
        compiler_params=_cparams(("arbitrary", "arbitrary")),
        name="rwkv",
    )(u3, shift_mu, w0, w2_pad, a0, a2_pad, g2, k_k, k_a, r_k, ln_w, ln_b)


OR_TM = 512
N_PAIRS = EXPERTS_PER_GROUP * (EXPERTS_PER_GROUP - 1) // 2
N_CLASSES = N_GROUPS * N_PAIRS
CLS_ROWS = LANES
ROUTER_ROWS = 48
SUBLANES = 8
H_CHUNKS = D_MODEL // LANES
SLAB_IN = H_CHUNKS + 1
SLAB_OUT = H_CHUNKS


def _first_index_of(vals, target, row_f):
    return jnp.min(jnp.where(vals == target, row_f, 1e9), axis=0, keepdims=True)


def _out_router_body(x_ref, osb_ref, orw_ref, wo_ref, g_ref, wr_ref, wrhi_ref, br_ref,
                     slab_ref, meta_ref, hist_ref, carry_ref, *, tm):
    i = pl.program_id(0)

    @pl.when(i == 0)
    def _():
        carry_ref[...] = jnp.zeros_like(carry_ref)

    h = (x_ref[...] + _dot(osb_ref[...], wo_ref[0:SB_WIDTH, :])
         + _dot(orw_ref[...], wo_ref[SB_WIDTH:, :]))
    ms = jnp.mean(h * h, axis=-1, keepdims=True)
    xn = h * lax.rsqrt(ms + RMS_EPS) * g_ref[...]

    xn_hi, xn_lo = _split_bf16(xn)
    both = _dot_nt(wr_ref[...], xn_hi)
    logits = both[:ROUTER_ROWS] + both[ROUTER_ROWS:] + _dot_nt(wrhi_ref[...], xn_lo) + br_ref[...][:, 0:1]

    row_f = lax.broadcasted_iota(jnp.int32, (SUBLANES, tm), 0).astype(F32)
    lg = jnp.where(row_f < N_GROUPS, logits[N_EXPERTS:N_EXPERTS + SUBLANES], -jnp.inf)
    eg = jnp.exp(lg - jnp.max(lg, axis=0, keepdims=True))
    pg = eg / jnp.sum(eg, axis=0, keepdims=True)
    g_val = jnp.max(pg, axis=0, keepdims=True)
    g_idx = _first_index_of(pg, g_val, row_f)
    sel = jnp.zeros((SUBLANES, tm), F32)
    for g in range(N_GROUPS):
        sel = jnp.where(g_idx == g, logits[g * EXPERTS_PER_GROUP:(g + 1) * EXPERTS_PER_GROUP], sel)
    ee = jnp.exp(sel - jnp.max(sel, axis=0, keepdims=True))
    pe = ee / jnp.sum(ee, axis=0, keepdims=True)
    e1 = jnp.max(pe, axis=0, keepdims=True)
    i1 = _first_index_of(pe, e1, row_f)
    pe2 = jnp.where(row_f == i1, -1.0, pe)
    e2 = jnp.max(pe2, axis=0, keepdims=True)
    i2 = _first_index_of(pe2, e2, row_f)
    den = e1 + e2
    wt1 = g_val * e1 / den
    wt2 = g_val * e2 / den
    first_lo = i1 < i2
    lo = jnp.where(first_lo, i1, i2)
    hi = jnp.where(first_lo, i2, i1)
    w_lo = jnp.where(first_lo, wt1, wt2)
    w_hi = jnp.where(first_lo, wt2, wt1)
    pair = lo * (2 * EXPERTS_PER_GROUP - 1 - lo) * 0.5 + (hi - lo - 1.0)
    cls = g_idx * N_PAIRS + pair

    cls_row = lax.broadcasted_iota(jnp.int32, (CLS_ROWS, tm), 0).astype(F32)
    onehot = cls_row == cls
    onehot_b = jnp.where(onehot, 1.0, 0.0).astype(BF16)
    upto = (lax.broadcasted_iota(jnp.int32, (tm, tm), 0)
            <= lax.broadcasted_iota(jnp.int32, (tm, tm), 1)).astype(BF16)
    cum = _dot(onehot_b, upto)
    tot = _dot(onehot_b, jnp.ones((tm, LANES), BF16))
    carry = carry_ref[...]
    before = jnp.concatenate([carry] * (tm // LANES), axis=1)
    rank = jnp.sum(jnp.where(onehot, cum - 1.0 + before, 0.0), axis=0, keepdims=True)
    carry_ref[...] = carry + tot
    hist_ref[...] = carry + tot

    meta = jnp.concatenate([cls, rank, jnp.zeros((SUBLANES - 2, tm), F32)], axis=0)
    meta_ref[0] = meta.astype(jnp.int32)

    w_rows = jnp.concatenate([w_lo, w_hi, jnp.zeros((LANES - 2, tm), F32)], axis=0)
    for j in range(H_CHUNKS):
        slab_ref[pl.ds(j, tm, stride=SLAB_IN), :] = h[:, j * LANES:(j + 1) * LANES]
    slab_ref[pl.ds(H_CHUNKS, tm, stride=SLAB_IN), :] = jnp.transpose(w_rows)


def out_router(x2d, o_sb, o_rw, w_out, g_ffn, wr_both, wr_hi, b_r):
    T = x2d.shape[0]
    tm = min(OR_TM, T)
    nt = T // tm
    const = lambda *shape: pl.BlockSpec(shape, lambda i: (0,) * len(shape))
    return pl.pallas_call(
        functools.partial(_out_router_body, tm=tm),
        grid=(nt,),
        in_specs=[
            pl.BlockSpec((tm, D_MODEL), lambda i: (i, 0)),
            pl.BlockSpec((tm, SB_WIDTH), lambda i: (i, 0)),
            pl.BlockSpec((tm, RW_WIDTH), lambda i: (i, 0)),
            const(D_MODEL, D_MODEL), const(1, D_MODEL),
            const(2 * ROUTER_ROWS, D_MODEL), const(ROUTER_ROWS, D_MODEL), const(ROUTER_ROWS, LANES),
        ],
        out_specs=[
            pl.BlockSpec((tm * SLAB_IN, LANES), lambda i: (i, 0)),
            pl.BlockSpec((1, SUBLANES, tm), lambda i: (i, 0, 0)),
            const(CLS_ROWS, LANES),
        ],
        out_shape=[
            jax.ShapeDtypeStruct((T * SLAB_IN, LANES), F32),
            jax.ShapeDtypeStruct((nt, SUBLANES, tm), jnp.int32),
            jax.ShapeDtypeStruct((CLS_ROWS, LANES), F32),
        ],
        scratch_shapes=[pltpu.VMEM((CLS_ROWS, LANES), F32)],
        compiler_params=_cparams(("arbitrary",)),
        name="out_router",
    )(x2d, o_sb, o_rw, w_out, g_ffn, wr_both, wr_hi, b_r)


PERM_TOKENS_PER_STEP = 1024
PERM_WINDOW = 256
PERM_UNROLL = 8
DMA_PRIORITIES = 2


def _slab_scatter_body(pos_ref, src_ref, dst_ref, sem, *, tokens, slab):
    window = min(PERM_WINDOW, tokens)

    def slab_copy(j, dst_tok):
        return pltpu.make_async_copy(src_ref.at[pl.ds(j * slab, slab)],
                                     dst_ref.at[pl.ds(dst_tok * slab, slab)], sem)

    def start_pair(jj, carry):
        for prio in range(DMA_PRIORITIES):
            j = jj * DMA_PRIORITIES + prio
            slab_copy(j, pos_ref[0, 0, j]).start(priority=prio)
        return carry

    def start_and_retire_pair(jj, carry):
        for prio in range(DMA_PRIORITIES):
            j = jj * DMA_PRIORITIES + prio
            slab_copy(j, pos_ref[0, 0, j]).start(priority=prio)
            slab_copy(0, 0).wait()
        return carry

    def retire(j, carry):
        slab_copy(0, 0).wait()
        return carry

    lax.fori_loop(0, window // DMA_PRIORITIES, start_pair, 0, unroll=PERM_UNROLL // DMA_PRIORITIES)
    lax.fori_loop(window // DMA_PRIORITIES, tokens // DMA_PRIORITIES, start_and_retire_pair, 0,
                  unroll=PERM_UNROLL // DMA_PRIORITIES)
    lax.fori_loop(0, window, retire, 0, unroll=PERM_UNROLL)


def slab_scatter(src, pos, slab):
    n_tok = pos.shape[0]
    tokens = min(PERM_TOKENS_PER_STEP, n_tok)
    steps = n_tok // tokens
    return pl.pallas_call(
        functools.partial(_slab_scatter_body, tokens=tokens, slab=slab),
        grid=(steps,),
        in_specs=[
            pl.BlockSpec((1, 1, tokens), lambda s: (s, 0, 0), memory_space=pltpu.SMEM),
            pl.BlockSpec((tokens * slab, LANES), lambda s: (s, 0)),
        ],
        out_specs=pl.BlockSpec(memory_space=pl.ANY),
        out_shape=jax.ShapeDtypeStruct(src.shape, src.dtype),
        scratch_shapes=[pltpu.SemaphoreType.DMA(())],
        compiler_params=_cparams(("arbitrary",)),
        name="slab_scatter",
    )(pos.reshape(steps, 1, tokens), src)


GATHER_TM = 512


def _slab_gather_body(pos_ref, nxt_ref, src_ref, g_ref, o_ref, buf, sem, *, tm):
    i = pl.program_id(0)
    n_steps = pl.num_programs(0)
    slot = i % 2

    def slab_copy(tok, j, to_slot):
        start = pl.multiple_of(tok * SLAB_OUT, SLAB_OUT)
        return pltpu.make_async_copy(src_ref.at[pl.ds(start, SLAB_OUT)],
                                     buf.at[to_slot, pl.ds(j * SLAB_OUT, SLAB_OUT)], sem.at[to_slot])

    def fetch(idx_ref, to_slot):
        def issue_pair(jj, carry):
            for prio in range(DMA_PRIORITIES):
                j = jj * DMA_PRIORITIES + prio
                slab_copy(idx_ref[0, 0, j], j, to_slot).start(priority=prio)
            return carry
        lax.fori_loop(0, tm // DMA_PRIORITIES, issue_pair, 0, unroll=PERM_UNROLL // DMA_PRIORITIES)

    @pl.when(i == 0)
    def _():
        fetch(pos_ref, 0)

    @pl.when(i + 1 < n_steps)
    def _():
        fetch(nxt_ref, 1 - slot)

    def drain(j, carry):
        slab_copy(0, 0, slot).wait()
        return carry

    lax.fori_loop(0, tm, drain, 0, unroll=PERM_UNROLL)
    chunks = [buf[slot, pl.ds(j, tm, stride=SLAB_OUT), :] for j in range(H_CHUNKS)]
    ssq = sum(jnp.sum(x * x, axis=-1, keepdims=True) for x in chunks)
    scale = lax.rsqrt(ssq * (1.0 / D_MODEL) + RMS_EPS)
    for j in range(H_CHUNKS):
        o_ref[:, j * LANES:(j + 1) * LANES] = chunks[j] * scale * g_ref[:, j * LANES:(j + 1) * LANES]


def slab_gather(src, pos, g_final):
    n_tok = pos.shape[0]
    tm = min(GATHER_TM, n_tok)
    steps = n_tok // tm
    pos3 = pos.reshape(steps, 1, tm)
    return pl.pallas_call(
        functools.partial(_slab_gather_body, tm=tm),
        grid=(steps,),
        in_specs=[
            pl.BlockSpec((1, 1, tm), lambda i: (i, 0, 0), memory_space=pltpu.SMEM),
            pl.BlockSpec((1, 1, tm), lambda i: (jnp.minimum(i + 1, steps - 1), 0, 0), memory_space=pltpu.SMEM),
            pl.BlockSpec(memory_space=pl.ANY),
            pl.BlockSpec((1, D_MODEL), lambda i: (0, 0)),
        ],
        out_specs=pl.BlockSpec((tm, D_MODEL), lambda i: (i, 0)),
        out_shape=jax.ShapeDtypeStruct((n_tok, D_MODEL), src.dtype),
        scratch_shapes=[
            pltpu.VMEM((2, tm * SLAB_OUT, LANES), src.dtype),
            pltpu.SemaphoreType.DMA((2,)),
        ],
        compiler_params=_cparams(("arbitrary",)),
        name="slab_gather",
    )(pos3, pos3, src, g_final)


EX_TM = 256
EX_ROW_PARTS = 1


def _experts_body(tile_ref, lo_ref, hi_ref, start_ref, end_ref,
                  hs_ref, gffn_ref, wg_lo, wu_lo, wd_lo, wg_hi, wu_hi, wd_hi, o_ref, *, tm):
    i = pl.program_id(0)
    tile = tile_ref[i]
    start = start_ref[i]
    end = end_ref[i]

    @pl.when(jnp.logical_or(i == 0, tile != tile_ref[jnp.maximum(i - 1, 0)]))
    def _():
        o_ref[...] = jnp.zeros_like(o_ref)

    @pl.when(end > start)
    def _():
        parts = range(EX_ROW_PARTS)
        rp = tm // EX_ROW_PARTS
        in_rows = lambda r, j: pl.ds(r * rp * SLAB_IN + j, rp, stride=SLAB_IN)
        out_rows = lambda r, j: pl.ds(r * rp * SLAB_OUT + j, rp, stride=SLAB_OUT)
        h = [jnp.concatenate([hs_ref[in_rows(r, j), :] for j in range(H_CHUNKS)], axis=1) for r in parts]
        w_row = [hs_ref[in_rows(r, H_CHUNKS), :] for r in parts]
        w_lo = [w[:, 0:1] for w in w_row]
        w_hi = [w[:, 1:2] for w in w_row]
        ms = [jnp.mean(x * x, axis=-1, keepdims=True) for x in h]
        xn = [(x * lax.rsqrt(m + RMS_EPS) * gffn_ref[...]).astype(BF16) for x, m in zip(h, ms)]

        g_lo = [_dot(x, wg_lo[0]) for x in xn]
        g_hi = [_dot(x, wg_hi[0]) for x in xn]
        u_lo = [_dot(x, wu_lo[0]) for x in xn]
        u_hi = [_dot(x, wu_hi[0]) for x in xn]
        a_lo = [(g * _sigmoid(g) * u * w).astype(BF16) for g, u, w in zip(g_lo, u_lo, w_lo)]
        a_hi = [(g * _sigmoid(g) * u * w).astype(BF16) for g, u, w in zip(g_hi, u_hi, w_hi)]
        y = [_dot(a, wd_lo[0]) + _dot(b, wd_hi[0]) for a, b in zip(a_lo, a_hi)]
        for r in parts:
            res = h[r] + y[r]
            rows = tile * tm + r * rp + lax.broadcasted_iota(jnp.int32, (rp, 1), 0)
            mine = jnp.logical_and(rows >= start, rows < end)
            for j in range(H_CHUNKS):
                sl = out_rows(r, j)
                o_ref[sl, :] = jnp.where(mine, res[:, j * LANES:(j + 1) * LANES], o_ref[sl, :])


def experts(h_sorted, g_ffn, w_gate, w_up, w_down, item_tile, item_lo, item_hi, item_start, item_end):
    T = h_sorted.shape[0] // SLAB_IN
    tm = min(EX_TM, T)
    n_items = item_tile.shape[0]
    gate_spec = lambda which: pl.BlockSpec(
        (1, D_MODEL, D_EXPERT), lambda i, tl, lo, hi, st, en: ((lo, hi)[which][i], 0, 0))
    down_spec = lambda which: pl.BlockSpec(
        (1, D_EXPERT, D_MODEL), lambda i, tl, lo, hi, st, en: ((lo, hi)[which][i], 0, 0))
    vec_spec = pl.BlockSpec((1, D_MODEL), lambda i, tl, lo, hi, st, en: (0, 0))
    grid_spec = pltpu.PrefetchScalarGridSpec(
        num_scalar_prefetch=5,
        grid=(n_items,),
        in_specs=[
            pl.BlockSpec((tm * SLAB_IN, LANES), lambda i, tl, lo, hi, st, en: (tl[i], 0)),
            vec_spec,
            gate_spec(0), gate_spec(0), down_spec(0),
            gate_spec(1), gate_spec(1), down_spec(1),
        ],
        out_specs=pl.BlockSpec((tm * SLAB_OUT, LANES), lambda i, tl, lo, hi, st, en: (tl[i], 0)),
    )
    return pl.pallas_call(
        functools.partial(_experts_body, tm=tm),
        grid_spec=grid_spec,
        out_shape=jax.ShapeDtypeStruct((T * SLAB_OUT, LANES), F32),
        compiler_params=_cparams(("arbitrary",)),
        name="experts",
    )(item_tile, item_lo, item_hi, item_start, item_end,
      h_sorted, g_ffn, w_gate, w_up, w_down, w_gate, w_up, w_down)


def _pair_first(lo):
    return (lo * (2 * EXPERTS_PER_GROUP - 1 - lo)) // 2


def _work_items(hist, meta, T, tm):
    i32 = jnp.int32
    counts = hist[:N_CLASSES, 0].astype(i32)
    offs = jnp.cumsum(counts) - counts
    cls = meta[:, 0, :].reshape(T)
    rank = meta[:, 1, :].reshape(T)
    class_ids = jnp.arange(N_CLASSES, dtype=i32)
    pos = jnp.sum(jnp.where(cls[:, None] == class_ids[None, :], offs[None, :], 0), axis=1) + rank

    n_tiles = T // tm
    n_items = n_tiles + N_CLASSES
    tile_starts = jnp.arange(n_tiles, dtype=i32) * tm
    tile_slot = jnp.arange(n_tiles, dtype=i32) + jnp.sum(offs[None, :] < tile_starts[:, None], axis=1, dtype=i32)
    class_slot = class_ids + jnp.sum(tile_starts[None, :] <= offs[:, None], axis=1, dtype=i32)
    slots = jnp.arange(n_items, dtype=i32)
    starts = (jnp.sum(jnp.where(tile_slot[None, :] == slots[:, None], tile_starts[None, :], 0), axis=1)
              + jnp.sum(jnp.where(class_slot[None, :] == slots[:, None], offs[None, :], 0), axis=1))
    ends = jnp.concatenate([starts[1:], jnp.full((1,), T, i32)])
    item_tile = jnp.minimum(starts // tm, n_tiles - 1)
    item_cls = jnp.sum(offs[None, :] <= starts[:, None], axis=1, dtype=i32) - 1
    grp = item_cls // N_PAIRS
    q = item_cls % N_PAIRS
    firsts = _pair_first(jnp.arange(1, EXPERTS_PER_GROUP - 1, dtype=i32))
    lo = jnp.sum(q[:, None] >= firsts[None, :], axis=1, dtype=i32)
    hi = q - _pair_first(lo) + lo + 1
    base = grp * EXPERTS_PER_GROUP
    return pos, item_tile, base + lo, base + hi, starts, ends


def kernel(x, norm_mix_g, w_in, shift_mu, sb_out_g, rw_w0, rw_w2, rw_a0, rw_a2, rw_g2, rw_k_k, rw_k_a, rw_r_k, rw_ln_w, rw_ln_b, w_out, norm_ffn_g, router_grp_w, router_grp_b, router_exp_w, router_exp_b, exp_w_gate, exp_w_up, exp_w_down, final_norm_g):
    B, S, D = x.shape
    T = B * S
    assert D == D_MODEL and w_in.shape[0] == 1, "one layer of width 1024 is what these kernels implement"
    l = 0
    row = lambda a: a.reshape(1, -1)
    x2d = x.reshape(T, D)

    w_in_b = w_in[l].astype(BF16)
    u_sb, u_rw = in_proj(x2d, row(norm_mix_g[l]), w_in_b[:, :SB_IN], w_in_b[:, SB_IN:])
    o_sb = sb_attn(u_sb, row(sb_out_g[l]), B, S)
    w2_pad = jnp.concatenate([rw_w2[l], jnp.zeros_like(rw_a2[l])], axis=0).astype(BF16)
    a2_pad = jnp.concatenate([jnp.zeros_like(rw_w2[l]), rw_a2[l]], axis=0).astype(BF16)
    o_rw = rwkv(u_rw, row(shift_mu[l]), row(rw_w0[l]), w2_pad, row(rw_a0[l]), a2_pad, rw_g2[l].astype(BF16),
                row(rw_k_k[l]), row(rw_k_a[l]), row(rw_r_k[l]), row(rw_ln_w[l]), row(rw_ln_b[l]), B, S)

    pad_rows = ROUTER_ROWS - N_EXPERTS - N_GROUPS
    wr = jnp.concatenate([router_exp_w[l].T, router_grp_w[l].T, jnp.zeros((pad_rows, D), F32)], axis=0)
    wr_hi, wr_lo = _split_bf16(wr)
    b_r = jnp.concatenate([router_exp_b[l], router_grp_b[l], jnp.zeros((pad_rows,), F32)])
    b_r = jnp.broadcast_to(b_r[:, None], (ROUTER_ROWS, LANES))
    h_slabs, meta, hist = out_router(x2d, o_sb.reshape(T, SB_WIDTH), o_rw.reshape(T, RW_WIDTH),
                                     w_out[l].astype(BF16), row(norm_ffn_g[l]),
                                     jnp.concatenate([wr_hi, wr_lo], axis=0), wr_hi, b_r)

    tm = min(EX_TM, T)
    pos, item_tile, item_lo, item_hi, item_start, item_end = _work_items(hist, meta, T, tm)
    h_sorted = slab_scatter(h_slabs, pos, SLAB_IN)
    out_sorted = experts(h_sorted, row(norm_ffn_g[l]),
                         exp_w_gate[l].astype(BF16), exp_w_up[l].astype(BF16), exp_w_down[l].astype(BF16),
                         item_tile, item_lo, item_hi, item_start, item_end)
    out = slab_gather(out_sorted, pos, row(final_norm_g))
    return out.reshape(B, S, D)
```

```python
import functools
import math

import jax
import jax.numpy as jnp
from jax import lax
from jax.experimental import pallas as pl
from jax.experimental.pallas import tpu as pltpu

F32 = jnp.float32
BF16 = jnp.bfloat16

D_MODEL = 1024
SB_HEADS = 8
HEAD_DIM = 64
SB_WIDTH = SB_HEADS * HEAD_DIM
RW_HEADS = 8
RW_WIDTH = RW_HEADS * HEAD_DIM
LORA_W = 64
LORA_A = 64
LORA_G = 128
SB_IN = 3 * SB_WIDTH
RW_IN = 3 * RW_WIDTH + LORA_W + LORA_A + LORA_G
N_GROUPS = 4
EXPERTS_PER_GROUP = 8
N_EXPERTS = N_GROUPS * EXPERTS_PER_GROUP
D_EXPERT = 256
RMS_EPS = 1e-6
GN_EPS = 64e-5

LANES = 128
MXU_DIM = 256
VMEM_LIMIT = 48 * 1024 * 1024


def _cparams(sem):
    return pltpu.CompilerParams(dimension_semantics=sem, vmem_limit_bytes=VMEM_LIMIT)


def _dot(a, b):
    return jnp.dot(a, b, preferred_element_type=F32)


def _dot_nt(a, b):
    return lax.dot_general(a, b, (((1,), (1,)), ((), ())), preferred_element_type=F32)


def _dot_tn(a, b):
    return lax.dot_general(a, b, (((0,), (0,)), ((), ())), preferred_element_type=F32)


IN_TM = 512


def _in_proj_body(x_ref, g_ref, wsb_ref, wrw_ref, usb_ref, urw_ref):
    x = x_ref[...]
    ms = jnp.mean(x * x, axis=-1, keepdims=True)
    xn = (x * lax.rsqrt(ms + RMS_EPS) * g_ref[...]).astype(BF16)
    usb_ref[...] = _dot(xn, wsb_ref[...]).astype(BF16)
    urw_ref[...] = _dot(xn, wrw_ref[...]).astype(BF16)


def in_proj(x2d, g, w_sb, w_rw):
    T = x2d.shape[0]
    tm = min(IN_TM, T)
    return pl.pallas_call(
        _in_proj_body,
        grid=(T // tm,),
        in_specs=[
            pl.BlockSpec((tm, D_MODEL), lambda i: (i, 0)),
            pl.BlockSpec((1, D_MODEL), lambda i: (0, 0)),
            pl.BlockSpec((D_MODEL, SB_IN), lambda i: (0, 0)),
            pl.BlockSpec((D_MODEL, RW_IN), lambda i: (0, 0)),
        ],
        out_specs=[
            pl.BlockSpec((tm, SB_IN), lambda i: (i, 0)),
            pl.BlockSpec((tm, RW_IN), lambda i: (i, 0)),
        ],
        out_shape=[
            jax.ShapeDtypeStruct((T, SB_IN), BF16),
            jax.ShapeDtypeStruct((T, RW_IN), BF16),
        ],
        compiler_params=_cparams(("arbitrary",)),
        name="in_proj",
    )(x2d, g, w_sb, w_rw)


SB_BLK = 256
SB_TILES_PER_STEP = 4
SB_SKIP = 40.0


def _decay(z):
    one = jnp.asarray(1.0, z.dtype)
    zero = jnp.asarray(0.0, z.dtype)
    return jnp.maximum(z, zero) + jnp.log(one + jnp.exp(-jnp.abs(z)))


def _sb_attn_body(q_ref, k_ref, v_ref, g_ref, later_h_ref, later_f_ref, mask_t_ref, mask_b_ref, o_ref,
                  *, blk, n_tiles):
    qb = pl.program_id(2)
    half = blk // 2
    tiles = range(n_tiles)
    lane = lax.broadcasted_iota(jnp.int32, (1, LANES), 1)
    head_masks = [lane < HEAD_DIM, lane >= HEAD_DIM]
    lanes_of = lambda t: slice(t * LANES, (t + 1) * LANES)

    def stack_heads(x):
        return jnp.concatenate([jnp.where(m, x, 0.0) for m in head_masks], axis=0).astype(BF16)

    def unstack(pv):
        n = pv.shape[0] // 2
        return jnp.where(head_masks[0], pv[:n], pv[n:])

    def row_sum(d):
        return jnp.sum(d.astype(F32), axis=-1, keepdims=True)

    q = [q_ref[0, :, lanes_of(t)].astype(F32) * (1.0 / math.sqrt(HEAD_DIM)) for t in tiles]
    q_top = [stack_heads(x[:half]) for x in q]
    q_bot = [stack_heads(x[half:]) for x in q]

    later_h = later_h_ref[...]
    later_f = later_f_ref[...]
    mask_t = mask_t_ref[...]
    mask_b = mask_b_ref[...]

    def kv_chunk(c):
        start = pl.multiple_of(c * blk, blk)
        return ([k_ref[0, pl.ds(start, blk), lanes_of(t)] for t in tiles],
                [v_ref[0, pl.ds(start, blk), lanes_of(t)] for t in tiles])

    has_prev = qb >= 1
    kd, vd = kv_chunk(qb)
    kp, vp = kv_chunk(jnp.maximum(qb - 1, 0))
    z_td = [_dot_nt(x, k[:half]) for x, k in zip(q_top, kd)]
    z_bd = [_dot_nt(x, k) for x, k in zip(q_bot, kd)]
    z_tp = [_dot_nt(x, k) for x, k in zip(q_top, kp)]
    d_td = [_decay(z.astype(BF16)) * mask_t for z in z_td]
    d_bd = [_decay(z.astype(BF16)) * mask_b for z in z_bd]
    d_tp = [_decay(z.astype(BF16)) for z in z_tp]
    s_td = [_dot(d, later_h) for d in d_td]
    s_bd = [_dot(d, later_f) for d in d_bd]
    s_tp = [_dot(d, later_f) for d in d_tp]
    c_td = [row_sum(d) for d in d_td]
    w_td = [jnp.exp(jnp.minimum(z - s, 0.0)).astype(BF16) * mask_t for z, s in zip(z_td, s_td)]
    w_bd = [jnp.exp(jnp.minimum(z - s, 0.0)).astype(BF16) * mask_b for z, s in zip(z_bd, s_bd)]
    w_tp = [jnp.where(has_prev, jnp.exp(z - s - c), 0.0).astype(BF16) for z, s, c in zip(z_tp, s_tp, c_td)]
    acc_top = [unstack(_dot(a, v[:half]) + _dot(b, vv)) for a, v, b, vv in zip(w_td, vd, w_tp, vp)]
    acc_bot = [unstack(_dot(a, v)) for a, v in zip(w_bd, vd)]
    carry_top = [c + row_sum(d) for c, d in zip(c_td, d_tp)]
    carry_bot = [row_sum(d) for d in d_bd]

    def earlier_chunks(qs, first, carry, acc):
        def alive_of(carry):
            return functools.reduce(jnp.minimum, [jnp.min(c) for c in carry]) < SB_SKIP

        def cond(st):
            return jnp.logical_and(st[0] >= 0, st[1])

        def body(st):
            c, _, carry, acc = st
            kc, vc = kv_chunk(c)
            z = [_dot_nt(x, k) for x, k in zip(qs, kc)]
            d = [_decay(x.astype(BF16)) for x in z]
            s = [_dot(x, later_f) for x in d]
            w = [jnp.exp(zz - ss - cr).astype(BF16) for zz, ss, cr in zip(z, s, carry)]
            acc = [a + unstack(_dot(x, vv)) for a, x, vv in zip(acc, w, vc)]
            carry = [cr + row_sum(dd) for cr, dd in zip(carry, d)]
            return c - 1, alive_of(carry), carry, acc

        return lax.while_loop(cond, body, (first, alive_of(carry), carry, acc))[3]

    acc_top = earlier_chunks(q_top, qb - 2, carry_top, acc_top)
    acc_bot = earlier_chunks(q_bot, qb - 1, carry_bot, acc_bot)

    for t in tiles:
        for rows, acc in ((slice(0, half), acc_top[t]), (slice(half, blk), acc_bot[t])):
            sq = acc * acc
            s_lo = jnp.sum(jnp.where(head_masks[0], sq, 0.0), axis=-1, keepdims=True)
            s_all = jnp.sum(sq, axis=-1, keepdims=True)
            ms = jnp.where(head_masks[0], s_lo, s_all - s_lo) * (1.0 / HEAD_DIM)
            o_ref[0, rows, lanes_of(t)] = (acc * lax.rsqrt(ms + RMS_EPS) * g_ref[:, lanes_of(t)]).astype(BF16)


def sb_attn(u_sb, sb_out_g, B, S):
    blk = min(SB_BLK, S)
    half = blk // 2
    u3 = u_sb.reshape(B, S, SB_IN)
    n_pairs = SB_WIDTH // LANES
    after = lambda n: (jnp.arange(n)[:, None] >= jnp.arange(n)[None, :]).astype(BF16)
    before = (jnp.arange(half)[None, :] < jnp.arange(half)[:, None]).astype(BF16)
    mask_t = jnp.tile(before, (2, 1))
    mask_b = jnp.tile(jnp.concatenate([jnp.ones((half, half), BF16), before], axis=1), (2, 1))
    n_tiles = SB_TILES_PER_STEP
    width = n_tiles * LANES
    n_groups = n_pairs // n_tiles
    const = lambda r, c: pl.BlockSpec((r, c), lambda b, p, i: (0, 0))
    return pl.pallas_call(
        functools.partial(_sb_attn_body, blk=blk, n_tiles=n_tiles),
        grid=(B, n_groups, S // blk),
        in_specs=[
            pl.BlockSpec((1, blk, width), lambda b, p, i: (b, i, p)),
            pl.BlockSpec((1, S, width), lambda b, p, i: (b, 0, n_groups + p)),
            pl.BlockSpec((1, S, width), lambda b, p, i: (b, 0, 2 * n_groups + p)),
            pl.BlockSpec((1, width), lambda b, p, i: (0, p)),
            const(half, half), const(blk, blk), const(blk, half), const(blk, blk),
        ],
        out_specs=pl.BlockSpec((1, blk, width), lambda b, p, i: (b, i, p)),
        out_shape=jax.ShapeDtypeStruct((B, S, SB_WIDTH), BF16),
        compiler_params=_cparams(("arbitrary", "arbitrary", "arbitrary")),
        name="sb_attn",
    )(u3, u3, u3, sb_out_g, after(half), after(blk), mask_t, mask_b)


RW_CHUNK = LANES
RW_SEQS_PER_STEP = 4
RW_GROUP = MXU_DIM
RW_GROUP_HEADS = RW_GROUP // HEAD_DIM


def _softplus(y):
    return jnp.maximum(y, 0.0) + jnp.log(1.0 + jnp.exp(-jnp.abs(y)))


def _sigmoid(y):
    return 1.0 / (1.0 + jnp.exp(-y))


def _split_bf16(x):
    hi = x.astype(BF16)
    lo = (x - hi.astype(F32)).astype(BF16)
    return hi, lo


def _rwkv_body(u_ref, mu_ref, w0_ref, w2_ref, a0_ref, a2_ref, g2_ref, kk_ref, ka_ref, rk_ref,
               lnw_ref, lnb_ref, o_ref, prev_ref, state_ref, *, C, n_seqs):
    c = pl.program_id(1)
    G, GH = RW_GROUP, RW_GROUP_HEADS
    n_groups = RW_WIDTH // G

    @pl.when(c == 0)
    def _():
        prev_ref[...] = jnp.zeros_like(prev_ref)
        state_ref[...] = jnp.zeros_like(state_ref)

    head_bd = (lax.broadcasted_iota(jnp.int32, (G, G), 0) // HEAD_DIM
               == lax.broadcasted_iota(jnp.int32, (G, G), 1) // HEAD_DIM)
    ones_bd = head_bd.astype(BF16)
    stack_mask = (lax.broadcasted_iota(jnp.int32, (GH * C, G), 0) // C
                  == lax.broadcasted_iota(jnp.int32, (GH * C, G), 1) // HEAD_DIM)
    low_half = lax.broadcasted_iota(jnp.int32, (1, LANES), 1) < HEAD_DIM
    tt = lax.broadcasted_iota(jnp.int32, (C, GH * C), 0)
    ss = lax.broadcasted_iota(jnp.int32, (C, GH * C), 1) % C
    strict = ss < tt
    incl = ss <= tt
    tri_incl = (lax.broadcasted_iota(jnp.int32, (C, C), 1)
                <= lax.broadcasted_iota(jnp.int32, (C, C), 0)).astype(BF16)

    def head_sum(x):
        return _dot(x.astype(BF16), ones_bd)

    def stack(x):
        return jnp.where(stack_mask, jnp.concatenate([x] * GH, axis=0), 0.0).astype(BF16)

    def swap_halves(x):
        return pltpu.roll(x, HEAD_DIM, axis=1)

    seqs = range(n_seqs)
    units = [(n, gi) for n in seqs for gi in range(n_groups)]
    lanes_of = lambda gi: slice(gi * G, (gi + 1) * G)

    ums = []
    for n in seqs:
        u = u_ref[n].astype(F32)
        row_id = lax.broadcasted_iota(jnp.int32, (C, 1), 0)
        shifted = jnp.where(row_id == 0, prev_ref[n], pltpu.roll(u, 1, axis=0))
        prev_ref[n] = u[C - 1:C, :]
        ums.append(u + (shifted - u) * mu_ref[...])
    r = [um[:, 0:RW_WIDTH] for um in ums]
    k = [um[:, RW_WIDTH:2 * RW_WIDTH] for um in ums]
    v = [um[:, 2 * RW_WIDTH:3 * RW_WIDTH] for um in ums]
    xwa = [um[:, 3 * RW_WIDTH:3 * RW_WIDTH + LORA_W + LORA_A] for um in ums]
    xg = [um[:, 3 * RW_WIDTH + LORA_W + LORA_A:] for um in ums]

    lora_w = [_dot(jnp.tanh(x).astype(BF16), w2_ref[...]) for x in xwa]
    lora_a = [_dot(x.astype(BF16), a2_ref[...]) for x in xwa]
    gate = [_dot(_sigmoid(x).astype(BF16), g2_ref[...]) for x in xg]
    logdec = [-jnp.exp(-_softplus(-(w0_ref[...] + lw)) - 0.5) for lw in lora_w]
    lr = [_sigmoid(a0_ref[...] + la) for la in lora_a]

    splits = [_split_bf16(ld) for ld in logdec]
    cum = [_dot(tri_incl, hi) + _dot(tri_incl, lo) for hi, lo in splits]
    p_incl = [jnp.exp(cm) for cm in cum]
    p_prev = [jnp.exp(cm - ld) for cm, ld in zip(cum, logdec)]
    p_inv = [jnp.exp(-cm) for cm in cum]
    p_last = [p[C - 1:C, :] for p in p_incl]

    kk = [kn * kk_ref[...] for kn in k]
    k_adj = [kn * (1.0 + (lrn - 1.0) * ka_ref[...]) for kn, lrn in zip(k, lr)]
    rk_prod = [rn * kan * rk_ref[...] for rn, kan in zip(r, k_adj)]

    kk_ssq = [head_sum(kk[n][:, lanes_of(gi)] * kk[n][:, lanes_of(gi)]) for n, gi in units]
    kkn = [kk[n][:, lanes_of(gi)] * lax.rsqrt(jnp.maximum(s, 1e-24)) for (n, gi), s in zip(units, kk_ssq)]
    v_g = [v[n][:, lanes_of(gi)] for n, gi in units]
    at = [-kn * p_prev[n][:, lanes_of(gi)] for (n, gi), kn in zip(units, kkn)]
    bt = [kn * lr[n][:, lanes_of(gi)] * p_inv[n][:, lanes_of(gi)] for (n, gi), kn in zip(units, kkn)]
    kt = [k_adj[n][:, lanes_of(gi)] * p_inv[n][:, lanes_of(gi)] for n, gi in units]
    rt = [r[n][:, lanes_of(gi)] * p_incl[n][:, lanes_of(gi)] for n, gi in units]

    lhs2 = [jnp.concatenate([a, q], axis=0).astype(BF16) for a, q in zip(at, rt)]
    ab = [_dot_nt(l2, stack(b)) for l2, b in zip(lhs2, bt)]
    ak = [_dot_nt(l2, stack(kx)) for l2, kx in zip(lhs2, kt)]
    a_ab = [jnp.where(strict, x[:C], 0.0) for x in ab]
    q_ab = [jnp.where(incl, x[C:], 0.0) for x in ab]
    a_ak = [jnp.where(strict, x[:C], 0.0) for x in ak]
    q_ak = [jnp.where(incl, x[C:], 0.0) for x in ak]
    vs = [stack(x) for x in v_g]
    akv = [_dot(a.astype(BF16), s) for a, s in zip(a_ak, vs)]

    n_steps = max(1, (C - 1).bit_length())
    tiles_per_group = G // LANES
    ys, lps = [], []
    for ui in range(len(units)):
        for p in range(tiles_per_group):
            at_p = at[ui][:, p * LANES:(p + 1) * LANES]
            akv_p = akv[ui][:, p * LANES:(p + 1) * LANES]
            ys.append(jnp.where(low_half, at_p, swap_halves(akv_p)))
            ys.append(jnp.where(low_half, swap_halves(at_p), akv_p))
            for e in range(2):
                h = 2 * p + e
                lps.append(a_ab[ui][:, h * C:(h + 1) * C])
    for step in range(n_steps):
        if step == n_steps - 1:
            ys = [y + _dot(lp.astype(BF16), y.astype(BF16)) for y, lp in zip(ys, lps)]
        else:
            prods = [_dot(lp.astype(BF16), jnp.concatenate([y, lp], axis=1).astype(BF16))
                     for y, lp in zip(ys, lps)]
            ys = [y + pr[:, 0:LANES] for y, pr in zip(ys, prods)]
            lps = [pr[:, LANES:] for pr in prods]
    y1, y2 = [], []
    for ui in range(len(units)):
        w_tiles, u0_tiles = [], []
        for p in range(tiles_per_group):
            y_even, y_odd = ys[ui * GH + 2 * p], ys[ui * GH + 2 * p + 1]
            w_tiles.append(jnp.where(low_half, y_even, swap_halves(y_odd)))
            u0_tiles.append(jnp.where(low_half, swap_halves(y_even), y_odd))
        y1.append(jnp.concatenate(w_tiles, axis=1))
        y2.append(jnp.concatenate(u0_tiles, axis=1))

    s0 = [state_ref[n * n_groups + gi] for n, gi in units]
    s0b = [s.astype(BF16) for s in s0]
    uu = [_dot_nt(a.astype(BF16), s) + b for a, s, b in zip(y1, s0b, y2)]
    o_state = [_dot_nt(q.astype(BF16), s) for q, s in zip(rt, s0b)]
    o_u = [_dot(q.astype(BF16), stack(x)) for q, x in zip(q_ab, uu)]
    o_v = [_dot(q.astype(BF16), s) for q, s in zip(q_ak, vs)]
    upd = [_dot_tn(jnp.concatenate([x, vv], axis=0).astype(BF16),
                   jnp.concatenate([b * p_last[n][:, lanes_of(gi)], kx * p_last[n][:, lanes_of(gi)]],
                                   axis=0).astype(BF16))
           for (n, gi), x, vv, b, kx in zip(units, uu, v_g, bt, kt)]
    for (n, gi), s, up in zip(units, s0, upd):
        state_ref[n * n_groups + gi] = s * p_last[n][:, lanes_of(gi)] + jnp.where(head_bd, up, 0.0)
    o_g = [a + b + c_ for a, b, c_ in zip(o_state, o_u, o_v)]

    mean = [head_sum(x) * (1.0 / HEAD_DIM) for x in o_g]
    dev = [x - m for x, m in zip(o_g, mean)]
    var = [head_sum(d * d) * (1.0 / HEAD_DIM) for d in dev]
    rk_sum = [head_sum(rk_prod[n][:, lanes_of(gi)]) for n, gi in units]
    outs = []
    for (n, gi), d, vr, rs, vv in zip(units, dev, var, rk_sum, v_g):
        sl = lanes_of(gi)
        gn = d * lax.rsqrt(vr + GN_EPS) * lnw_ref[:, sl] + lnb_ref[:, sl]
        outs.append((gn + rs * vv) * gate[n][:, sl])
    for n in seqs:
        o_ref[n] = jnp.concatenate(outs[n * n_groups:(n + 1) * n_groups], axis=1).astype(BF16)


def rwkv(u_rw, shift_mu, w0, w2_pad, a0, a2_pad, g2, k_k, k_a, r_k, ln_w, ln_b, B, S):
    C = RW_CHUNK
    n_seqs = RW_SEQS_PER_STEP if B % RW_SEQS_PER_STEP == 0 else 1
    assert S % C == 0
    u3 = u_rw.reshape(B, S, RW_IN)
    vec = lambda n: pl.BlockSpec((1, n), lambda b, c: (0, 0))
    mat = lambda m, n: pl.BlockSpec((m, n), lambda b, c: (0, 0))
    return pl.pallas_call(
        functools.partial(_rwkv_body, C=C, n_seqs=n_seqs),
        grid=(B // n_seqs, S // C),
        in_specs=[
            pl.BlockSpec((n_seqs, C, RW_IN), lambda b, c: (b, c, 0)),
            vec(RW_IN), vec(RW_WIDTH), mat(LORA_W + LORA_A, RW_WIDTH), vec(RW_WIDTH),
            mat(LORA_W + LORA_A, RW_WIDTH), mat(LORA_G, RW_WIDTH),
            vec(RW_WIDTH), vec(RW_WIDTH), vec(RW_WIDTH), vec(RW_WIDTH), vec(RW_WIDTH),
        ],
        out_specs=pl.BlockSpec((n_seqs, C, RW_WIDTH), lambda b, c: (b, c, 0)),
        out_shape=jax.ShapeDtypeStruct((B, S, RW_WIDTH), BF16),
        scratch_shapes=[
            pltpu.VMEM((n_seqs, 1, RW_IN), F32),
            pltpu.VMEM((n_seqs * (RW_WIDTH // RW_GROUP), RW_GROUP, RW_GROUP), F32),
        ],
        compiler_params=_cparams(("arbitrary", "arbitrary")),
        name="rwkv",
    )(u3, shift_mu, w0, w2_pad, a0, a2_pad, g2, k_k, k_a, r_k, ln_w, ln_b)


OR_TM = 512
N_PAIRS = EXPERTS_PER_GROUP * (EXPERTS_PER_GROUP - 1) // 2
N_CLASSES = N_GROUPS * N_PAIRS
CLS_ROWS = LANES
ROUTER_ROWS = 48
SUBLANES = 8
H_CHUNKS = D_MODEL // LANES
SLAB_IN = H_CHUNKS + 1
SLAB_OUT = H_CHUNKS


def _first_index_of(vals, target, row_f):
    return jnp.min(jnp.where(vals == target, row_f, 1e9), axis=0, keepdims=True)


def _out_router_body(x_ref, osb_ref, orw_ref, wo_ref, g_ref, wr_ref, wrhi_ref, br_ref,
                     slab_ref, meta_ref, hist_ref, carry_ref, *, tm):
    i = pl.program_id(0)

    @pl.when(i == 0)
    def _():
        carry_ref[...] = jnp.zeros_like(carry_ref)

    h = (x_ref[...] + _dot(osb_ref[...], wo_ref[0:SB_WIDTH, :])
         + _dot(orw_ref[...], wo_ref[SB_WIDTH:, :]))
    ms = jnp.mean(h * h, axis=-1, keepdims=True)
    xn = h * lax.rsqrt(ms + RMS_EPS) * g_ref[...]

    xn_hi, xn_lo = _split_bf16(xn)
    both = _dot_nt(wr_ref[...], xn_hi)
    logits = both[:ROUTER_ROWS] + both[ROUTER_ROWS:] + _dot_nt(wrhi_ref[...], xn_lo) + br_ref[...][:, 0:1]

    row_f = lax.broadcasted_iota(jnp.int32, (SUBLANES, tm), 0).astype(F32)
    lg = jnp.where(row_f < N_GROUPS, logits[N_EXPERTS:N_EXPERTS + SUBLANES], -jnp.inf)
    eg = jnp.exp(lg - jnp.max(lg, axis=0, keepdims=True))
    pg = eg / jnp.sum(eg, axis=0, keepdims=True)
    g_val = jnp.max(pg, axis=0, keepdims=True)
    g_idx = _first_index_of(pg, g_val, row_f)
    sel = jnp.zeros((SUBLANES, tm), F32)
    for g in range(N_GROUPS):
        sel = jnp.where(g_idx == g, logits[g * EXPERTS_PER_GROUP:(g + 1) * EXPERTS_PER_GROUP], sel)
    ee = jnp.exp(sel - jnp.max(sel, axis=0, keepdims=True))
    pe = ee / jnp.sum(ee, axis=0, keepdims=True)
    e1 = jnp.max(pe, axis=0, keepdims=True)
    i1 = _first_index_of(pe, e1, row_f)
    pe2 = jnp.where(row_f == i1, -1.0, pe)
    e2 = jnp.max(pe2, axis=0, keepdims=True)
    i2 = _first_index_of(pe2, e2, row_f)
    den = e1 + e2
    wt1 = g_val * e1 / den
    wt2 = g_val * e2 / den
    first_lo = i1 < i2
    lo = jnp.where(first_lo, i1, i2)
    hi = jnp.where(first_lo, i2, i1)
    w_lo = jnp.where(first_lo, wt1, wt2)
    w_hi = jnp.where(first_lo, wt2, wt1)
    pair = lo * (2 * EXPERTS_PER_GROUP - 1 - lo) * 0.5 + (hi - lo - 1.0)
    cls = g_idx * N_PAIRS + pair

    cls_row = lax.broadcasted_iota(jnp.int32, (CLS_ROWS, tm), 0).astype(F32)
    onehot = cls_row == cls
    onehot_b = jnp.where(onehot, 1.0, 0.0).astype(BF16)
    upto = (lax.broadcasted_iota(jnp.int32, (tm, tm), 0)
            <= lax.broadcasted_iota(jnp.int32, (tm, tm), 1)).astype(BF16)
    cum = _dot(onehot_b, upto)
    tot = _dot(onehot_b, jnp.ones((tm, LANES), BF16))
    carry = carry_ref[...]
    before = jnp.concatenate([carry] * (tm // LANES), axis=1)
    rank = jnp.sum(jnp.where(onehot, cum - 1.0 + before, 0.0), axis=0, keepdims=True)
    carry_ref[...] = carry + tot
    hist_ref[...] = carry + tot

    meta = jnp.concatenate([cls, rank, jnp.zeros((SUBLANES - 2, tm), F32)], axis=0)
    meta_ref[0] = meta.astype(jnp.int32)

    w_rows = jnp.concatenate([w_lo, w_hi, jnp.zeros((LANES - 2, tm), F32)], axis=0)
    for j in range(H_CHUNKS):
        slab_ref[pl.ds(j, tm, stride=SLAB_IN), :] = h[:, j * LANES:(j + 1) * LANES]
    slab_ref[pl.ds(H_CHUNKS, tm, stride=SLAB_IN), :] = jnp.transpose(w_rows)


def out_router(x2d, o_sb, o_rw, w_out, g_ffn, wr_both, wr_hi, b_r):
    T = x2d.shape[0]
    tm = min(OR_TM, T)
    nt = T // tm
    const = lambda *shape: pl.BlockSpec(shape, lambda i: (0,) * len(shape))
    return pl.pallas_call(
        functools.partial(_out_router_body, tm=tm),
        grid=(nt,),
        in_specs=[
            pl.BlockSpec((tm, D_MODEL), lambda i: (i, 0)),
            pl.BlockSpec((tm, SB_WIDTH), lambda i: (i, 0)),
            pl.BlockSpec((tm, RW_WIDTH), lambda i: (i, 0)),
            const(D_MODEL, D_MODEL), const(1, D_MODEL),
            const(2 * ROUTER_ROWS, D_MODEL), const(ROUTER_ROWS, D_MODEL), const(ROUTER_ROWS, LANES),
        ],
        out_specs=[
            pl.BlockSpec((tm * SLAB_IN, LANES), lambda i: (i, 0)),
            pl.BlockSpec((1, SUBLANES, tm), lambda i: (i, 0, 0)),
            const(CLS_ROWS, LANES),
        ],
        out_shape=[
            jax.ShapeDtypeStruct((T * SLAB_IN, LANES), F32),
            jax.ShapeDtypeStruct((nt, SUBLANES, tm), jnp.int32),
            jax.ShapeDtypeStruct((CLS_ROWS, LANES), F32),
        ],
        scratch_shapes=[pltpu.VMEM((CLS_ROWS, LANES), F32)],
        compiler_params=_cparams(("arbitrary",)),
        name="out_router",
    )(x2d, o_sb, o_rw, w_out, g_ffn, wr_both, wr_hi, b_r)


PERM_TOKENS_PER_STEP = 1024
PERM_WINDOW = 256
PERM_UNROLL = 8


def _slab_scatter_body(pos_ref, src_ref, dst_ref, sem, *, tokens, slab):
    window = min(PERM_WINDOW, tokens)

    def slab_copy(j, dst_tok):
        return pltpu.make_async_copy(src_ref.at[pl.ds(j * slab, slab)],
                                     dst_ref.at[pl.ds(dst_tok * slab, slab)], sem)

    def start(j, carry):
        slab_copy(j, pos_ref[0, 0, j]).start()
        return carry

    def retire(j, carry):
        slab_copy(0, 0).wait()
        return carry

    def start_batch_retire_batch(b, carry):
        lax.fori_loop(0, PERM_UNROLL, lambda j, c: start(b * PERM_UNROLL + j, c), 0, unroll=True)
        lax.fori_loop(0, PERM_UNROLL, retire, 0, unroll=True)
        return carry

    lax.fori_loop(0, window, start, 0, unroll=PERM_UNROLL)
    lax.fori_loop(window // PERM_UNROLL, tokens // PERM_UNROLL, start_batch_retire_batch, 0)
    lax.fori_loop(0, window, retire, 0, unroll=PERM_UNROLL)


def slab_scatter(src, pos, slab):
    n_tok = pos.shape[0]
    tokens = min(PERM_TOKENS_PER_STEP, n_tok)
    steps = n_tok // tokens
    return pl.pallas_call(
        functools.partial(_slab_scatter_body, tokens=tokens, slab=slab),
        grid=(steps,),
        in_specs=[
            pl.BlockSpec((1, 1, tokens), lambda s: (s, 0, 0), memory_space=pltpu.SMEM),
            pl.BlockSpec((tokens * slab, LANES), lambda s: (s, 0)),
        ],
        out_specs=pl.BlockSpec(memory_space=pl.ANY),
        out_shape=jax.ShapeDtypeStruct(src.shape, src.dtype),
        scratch_shapes=[pltpu.SemaphoreType.DMA(())],
        compiler_params=_cparams(("arbitrary",)),
        name="slab_scatter",
    )(pos.reshape(steps, 1, tokens), src)


GATHER_TM = 512


def _slab_gather_body(pos_ref, nxt_ref, src_ref, g_ref, o_ref, buf, sem, *, tm):
    i = pl.program_id(0)
    n_steps = pl.num_programs(0)
    slot = i % 2

    def slab_copy(tok, j, to_slot):
        start = pl.multiple_of(tok * SLAB_OUT, SLAB_OUT)
        return pltpu.make_async_copy(src_ref.at[pl.ds(start, SLAB_OUT)],
                                     buf.at[to_slot, pl.ds(j * SLAB_OUT, SLAB_OUT)], sem.at[to_slot])

    def fetch(idx_ref, to_slot):
        def issue(j, carry):
            slab_copy(idx_ref[0, 0, j], j, to_slot).start()
            return carry
        lax.fori_loop(0, tm, issue, 0, unroll=PERM_UNROLL)

    @pl.when(i == 0)
    def _():
        fetch(pos_ref, 0)

    @pl.when(i + 1 < n_steps)
    def _():
        fetch(nxt_ref, 1 - slot)

    def drain(j, carry):
        slab_copy(0, 0, slot).wait()
        return carry

    lax.fori_loop(0, tm, drain, 0, unroll=PERM_UNROLL)
    chunks = [buf[slot, pl.ds(j, tm, stride=SLAB_OUT), :] for j in range(H_CHUNKS)]
    ssq = sum(jnp.sum(x * x, axis=-1, keepdims=True) for x in chunks)
    scale = lax.rsqrt(ssq * (1.0 / D_MODEL) + RMS_EPS)
    for j in range(H_CHUNKS):
        o_ref[:, j * LANES:(j + 1) * LANES] = chunks[j] * scale * g_ref[:, j * LANES:(j + 1) * LANES]


def slab_gather(src, pos, g_final):
    n_tok = pos.shape[0]
    tm = min(GATHER_TM, n_tok)
    steps = n_tok // tm
    pos3 = pos.reshape(steps, 1, tm)
    return pl.pallas_call(
        functools.partial(_slab_gather_body, tm=tm),
        grid=(steps,),
        in_specs=[
            pl.BlockSpec((1, 1, tm), lambda i: (i, 0, 0), memory_space=pltpu.SMEM),
            pl.BlockSpec((1, 1, tm), lambda i: (jnp.minimum(i + 1, steps - 1), 0, 0), memory_space=pltpu.SMEM),
            pl.BlockSpec(memory_space=pl.ANY),
            pl.BlockSpec((1, D_MODEL), lambda i: (0, 0)),
        ],
        out_specs=pl.BlockSpec((tm, D_MODEL), lambda i: (i, 0)),
        out_shape=jax.ShapeDtypeStruct((n_tok, D_MODEL), src.dtype),
        scratch_shapes=[
            pltpu.VMEM((2, tm * SLAB_OUT, LANES), src.dtype),
            pltpu.SemaphoreType.DMA((2,)),
        ],
        compiler_params=_cparams(("arbitrary",)),
        name="slab_gather",
    )(pos3, pos3, src, g_final)


EX_TM = 256


def _experts_body(tile_ref, lo_ref, hi_ref, start_ref, end_ref,
                  hs_ref, gffn_ref, wg_lo, wu_lo, wd_lo, wg_hi, wu_hi, wd_hi, o_ref, *, tm):
    i = pl.program_id(0)
    tile = tile_ref[i]
    start = start_ref[i]
    end = end_ref[i]

    @pl.when(jnp.logical_or(i == 0, tile != tile_ref[jnp.maximum(i - 1, 0)]))
    def _():
        o_ref[...] = jnp.zeros_like(o_ref)

    @pl.when(end > start)
    def _():
        in_rows = lambda j: pl.ds(j, tm, stride=SLAB_IN)
        out_rows = lambda j: pl.ds(j, tm, stride=SLAB_OUT)
        h = jnp.concatenate([hs_ref[in_rows(j), :] for j in range(H_CHUNKS)], axis=1)
        w_row = hs_ref[in_rows(H_CHUNKS), :]
        w_lo = w_row[:, 0:1]
        w_hi = w_row[:, 1:2]
        ms = jnp.mean(h * h, axis=-1, keepdims=True)
        xn = (h * lax.rsqrt(ms + RMS_EPS) * gffn_ref[...]).astype(BF16)

        g_lo = _dot(xn, wg_lo[0])
        g_hi = _dot(xn, wg_hi[0])
        u_lo = _dot(xn, wu_lo[0])
        u_hi = _dot(xn, wu_hi[0])
        a_lo = (g_lo * _sigmoid(g_lo) * u_lo * w_lo).astype(BF16)
        a_hi = (g_hi * _sigmoid(g_hi) * u_hi * w_hi).astype(BF16)
        res = h + _dot(a_lo, wd_lo[0]) + _dot(a_hi, wd_hi[0])
        rows = tile * tm + lax.broadcasted_iota(jnp.int32, (tm, 1), 0)
        mine = jnp.logical_and(rows >= start, rows < end)
        for j in range(H_CHUNKS):
            o_ref[out_rows(j), :] = jnp.where(mine, res[:, j * LANES:(j + 1) * LANES], o_ref[out_rows(j), :])


def experts(h_sorted, g_ffn, w_gate, w_up, w_down, item_tile, item_lo, item_hi, item_start, item_end):
    T = h_sorted.shape[0] // SLAB_IN
    tm = min(EX_TM, T)
    n_items = item_tile.shape[0]
    gate_spec = lambda which: pl.BlockSpec(
        (1, D_MODEL, D_EXPERT), lambda i, tl, lo, hi, st, en: ((lo, hi)[which][i], 0, 0))
    down_spec = lambda which: pl.BlockSpec(
        (1, D_EXPERT, D_MODEL), lambda i, tl, lo, hi, st, en: ((lo, hi)[which][i], 0, 0))
    vec_spec = pl.BlockSpec((1, D_MODEL), lambda i, tl, lo, hi, st, en: (0, 0))
    grid_spec = pltpu.PrefetchScalarGridSpec(
        num_scalar_prefetch=5,
        grid=(n_items,),
        in_specs=[
            pl.BlockSpec((tm * SLAB_IN, LANES), lambda i, tl, lo, hi, st, en: (tl[i], 0)),
            vec_spec,
            gate_spec(0), gate_spec(0), down_spec(0),
            gate_spec(1), gate_spec(1), down_spec(1),
        ],
        out_specs=pl.BlockSpec((tm * SLAB_OUT, LANES), lambda i, tl, lo, hi, st, en: (tl[i], 0)),
    )
    return pl.pallas_call(
        functools.partial(_experts_body, tm=tm),
        grid_spec=grid_spec,
        out_shape=jax.ShapeDtypeStruct((T * SLAB_OUT, LANES), F32),
        compiler_params=_cparams(("arbitrary",)),
        name="experts",
    )(item_tile, item_lo, item_hi, item_start, item_end,
      h_sorted, g_ffn, w_gate, w_up, w_down, w_gate, w_up, w_down)


def _pair_first(lo):
    return (lo * (2 * EXPERTS_PER_GROUP - 1 - lo)) // 2


def _work_items(hist, meta, T, tm):
    i32 = jnp.int32
    counts = hist[:N_CLASSES, 0].astype(i32)
    offs = jnp.cumsum(counts) - counts
    cls = meta[:, 0, :].reshape(T)
    rank = meta[:, 1, :].reshape(T)
    class_ids = jnp.arange(N_CLASSES, dtype=i32)
    pos = jnp.sum(jnp.where(cls[:, None] == class_ids[None, :], offs[None, :], 0), axis=1) + rank

    n_tiles = T // tm
    n_items = n_tiles + N_CLASSES
    tile_starts = jnp.arange(n_tiles, dtype=i32) * tm
    tile_slot = jnp.arange(n_tiles, dtype=i32) + jnp.sum(offs[None, :] < tile_starts[:, None], axis=1, dtype=i32)
    class_slot = class_ids + jnp.sum(tile_starts[None, :] <= offs[:, None], axis=1, dtype=i32)
    slots = jnp.arange(n_items, dtype=i32)
    starts = (jnp.sum(jnp.where(tile_slot[None, :] == slots[:, None], tile_starts[None, :], 0), axis=1)
              + jnp.sum(jnp.where(class_slot[None, :] == slots[:, None], offs[None, :], 0), axis=1))
    ends = jnp.concatenate([starts[1:], jnp.full((1,), T, i32)])
    item_tile = jnp.minimum(starts // tm, n_tiles - 1)
    item_cls = jnp.sum(offs[None, :] <= starts[:, None], axis=1, dtype=i32) - 1
    grp = item_cls // N_PAIRS
    q = item_cls % N_PAIRS
    firsts = _pair_first(jnp.arange(1, EXPERTS_PER_GROUP - 1, dtype=i32))
    lo = jnp.sum(q[:, None] >= firsts[None, :], axis=1, dtype=i32)
    hi = q - _pair_first(lo) + lo + 1
    base = grp * EXPERTS_PER_GROUP
    return pos, item_tile, base + lo, base + hi, starts, ends


def kernel(x, norm_mix_g, w_in, shift_mu, sb_out_g, rw_w0, rw_w2, rw_a0, rw_a2, rw_g2, rw_k_k, rw_k_a, rw_r_k, rw_ln_w, rw_ln_b, w_out, norm_ffn_g, router_grp_w, router_grp_b, router_exp_w, router_exp_b, exp_w_gate, exp_w_up, exp_w_down, final_norm_g):
    B, S, D = x.shape
    T = B * S
    assert D == D_MODEL and w_in.shape[0] == 1, "one layer of width 1024 is what these kernels implement"
    l = 0
    row = lambda a: a.reshape(1, -1)
    x2d = x.reshape(T, D)

    w_in_b = w_in[l].astype(BF16)
    u_sb, u_rw = in_proj(x2d, row(norm_mix_g[l]), w_in_b[:, :SB_IN], w_in_b[:, SB_IN:])
    o_sb = sb_attn(u_sb, row(sb_out_g[l]), B, S)
    w2_pad = jnp.concatenate([rw_w2[l], jnp.zeros_like(rw_a2[l])], axis=0).astype(BF16)
    a2_pad = jnp.concatenate([jnp.zeros_like(rw_w2[l]), rw_a2[l]], axis=0).astype(BF16)
    o_rw = rwkv(u_rw, row(shift_mu[l]), row(rw_w0[l]), w2_pad, row(rw_a0[l]), a2_pad, rw_g2[l].astype(BF16),
                row(rw_k_k[l]), row(rw_k_a[l]), row(rw_r_k[l]), row(rw_ln_w[l]), row(rw_ln_b[l]), B, S)

    pad_rows = ROUTER_ROWS - N_EXPERTS - N_GROUPS
    wr = jnp.concatenate([router_exp_w[l].T, router_grp_w[l].T, jnp.zeros((pad_rows, D), F32)], axis=0)
    wr_hi, wr_lo = _split_bf16(wr)
    b_r = jnp.concatenate([router_exp_b[l], router_grp_b[l], jnp.zeros((pad_rows,), F32)])
    b_r = jnp.broadcast_to(b_r[:, None], (ROUTER_ROWS, LANES))
    h_slabs, meta, hist = out_router(x2d, o_sb.reshape(T, SB_WIDTH), o_rw.reshape(T, RW_WIDTH),
                                     w_out[l].astype(BF16), row(norm_ffn_g[l]),
                                     jnp.concatenate([wr_hi, wr_lo], axis=0), wr_hi, b_r)

    tm = min(EX_TM, T)
    pos, item_tile, item_lo, item_hi, item_start, item_end = _work_items(hist, meta, T, tm)
    h_sorted = slab_scatter(h_slabs, pos, SLAB_IN)
    out_sorted = experts(h_sorted, row(norm_ffn_g[l]),
                         exp_w_gate[l].astype(BF16), exp_w_up[l].astype(BF16), exp_w_down[l].astype(BF16),
                         item_tile, item_lo, item_hi, item_start, item_end)
    out = slab_gather(out_sorted, pos, row(final_norm_g))
    return out.reshape(B, S, D)
```

```python
import functools
import math

import jax
import jax.numpy as jnp
from jax import lax
from jax.experimental import pallas as pl
from jax.experimental.pallas import tpu as pltpu

F32 = jnp.float32
BF16 = jnp.bfloat16

D_MODEL = 1024
SB_HEADS = 8
HEAD_DIM = 64
SB_WIDTH = SB_HEADS * HEAD_DIM
RW_HEADS = 8
RW_WIDTH = RW_HEADS * HEAD_DIM
LORA_W = 64
LORA_A = 64
LORA_G = 128
SB_IN = 3 * SB_WIDTH
RW_IN = 3 * RW_WIDTH + LORA_W + LORA_A + LORA_G
N_GROUPS = 4
EXPERTS_PER_GROUP = 8
N_EXPERTS = N_GROUPS * EXPERTS_PER_GROUP
D_EXPERT = 256
RMS_EPS = 1e-6
GN_EPS = 64e-5

LANES = 128
MXU_DIM = 256
VMEM_LIMIT = 48 * 1024 * 1024


def _cparams(sem):
    return pltpu.CompilerParams(dimension_semantics=sem, vmem_limit_bytes=VMEM_LIMIT)


def _dot(a, b):
    return jnp.dot(a, b, preferred_element_type=F32)


def _dot_nt(a, b):
    return lax.dot_general(a, b, (((1,), (1,)), ((), ())), preferred_element_type=F32)


def _dot_tn(a, b):
    return lax.dot_general(a, b, (((0,), (0,)), ((), ())), preferred_element_type=F32)


IN_TM = 512


def _in_proj_body(x_ref, g_ref, wsb_ref, wrw_ref, usb_ref, urw_ref):
    x = x_ref[...]
    ms = jnp.mean(x * x, axis=-1, keepdims=True)
    xn = (x * lax.rsqrt(ms + RMS_EPS) * g_ref[...]).astype(BF16)
    usb_ref[...] = _dot(xn, wsb_ref[...]).astype(BF16)
    urw_ref[...] = _dot(xn, wrw_ref[...]).astype(BF16)


def in_proj(x2d, g, w_sb, w_rw):
    T = x2d.shape[0]
    tm = min(IN_TM, T)
    return pl.pallas_call(
        _in_proj_body,
        grid=(T // tm,),
        in_specs=[
            pl.BlockSpec((tm, D_MODEL), lambda i: (i, 0)),
            pl.BlockSpec((1, D_MODEL), lambda i: (0, 0)),
            pl.BlockSpec((D_MODEL, SB_IN), lambda i: (0, 0)),
            pl.BlockSpec((D_MODEL, RW_IN), lambda i: (0, 0)),
        ],
        out_specs=[
            pl.BlockSpec((tm, SB_IN), lambda i: (i, 0)),
            pl.BlockSpec((tm, RW_IN), lambda i: (i, 0)),
        ],
        out_shape=[
            jax.ShapeDtypeStruct((T, SB_IN), BF16),
            jax.ShapeDtypeStruct((T, RW_IN), BF16),
        ],
        compiler_params=_cparams(("arbitrary",)),
        name="in_proj",
    )(x2d, g, w_sb, w_rw)


SB_BLK = 256
SB_TILES_PER_STEP = 4
SB_SKIP = 40.0


def _decay(z):
    one = jnp.asarray(1.0, z.dtype)
    zero = jnp.asarray(0.0, z.dtype)
    return jnp.maximum(z, zero) + jnp.log(one + jnp.exp(-jnp.abs(z)))


def _sb_attn_body(q_ref, k_ref, v_ref, g_ref, later_h_ref, later_f_ref, mask_t_ref, mask_b_ref, o_ref,
                  *, blk, n_tiles):
    qb = pl.program_id(2)
    half = blk // 2
    tiles = range(n_tiles)
    lane = lax.broadcasted_iota(jnp.int32, (1, LANES), 1)
    head_masks = [lane < HEAD_DIM, lane >= HEAD_DIM]
    lanes_of = lambda t: slice(t * LANES, (t + 1) * LANES)

    def stack_heads(x):
        return jnp.concatenate([jnp.where(m, x, 0.0) for m in head_masks], axis=0).astype(BF16)

    def unstack(pv):
        n = pv.shape[0] // 2
        return jnp.where(head_masks[0], pv[:n], pv[n:])

    def row_sum(d):
        return jnp.sum(d.astype(F32), axis=-1, keepdims=True)

    q = [q_ref[0, :, lanes_of(t)].astype(F32) * (1.0 / math.sqrt(HEAD_DIM)) for t in tiles]
    q_top = [stack_heads(x[:half]) for x in q]
    q_bot = [stack_heads(x[half:]) for x in q]

    later_h = later_h_ref[...]
    later_f = later_f_ref[...]
    mask_t = mask_t_ref[...]
    mask_b = mask_b_ref[...]

    def kv_rows(index, size):
        start = pl.multiple_of(index * size, size)
        return ([k_ref[0, pl.ds(start, size), lanes_of(t)] for t in tiles],
                [v_ref[0, pl.ds(start, size), lanes_of(t)] for t in tiles])

    has_prev = qb >= 1
    kd, vd = kv_rows(qb, blk)
    kp, vp = kv_rows(jnp.maximum(2 * qb - 1, 0), half)
    z_td = [_dot_nt(x, k[:half]) for x, k in zip(q_top, kd)]
    z_bd = [_dot_nt(x, k) for x, k in zip(q_bot, kd)]
    z_tp = [_dot_nt(x, k) for x, k in zip(q_top, kp)]
    d_td = [_decay(z.astype(BF16)) * mask_t for z in z_td]
    d_bd = [_decay(z.astype(BF16)) * mask_b for z in z_bd]
    d_tp = [_decay(z.astype(BF16)) for z in z_tp]
    s_td = [_dot(d, later_h) for d in d_td]
    s_bd = [_dot(d, later_f) for d in d_bd]
    s_tp = [_dot(d, later_h) for d in d_tp]
    c_td = [row_sum(d) for d in d_td]
    w_td = [jnp.exp(jnp.minimum(z - s, 0.0)).astype(BF16) * mask_t for z, s in zip(z_td, s_td)]
    w_bd = [jnp.exp(jnp.minimum(z - s, 0.0)).astype(BF16) * mask_b for z, s in zip(z_bd, s_bd)]
    w_tp = [jnp.where(has_prev, jnp.exp(z - s - c), 0.0).astype(BF16) for z, s, c in zip(z_tp, s_tp, c_td)]
    acc_top = [unstack(_dot(a, v[:half]) + _dot(b, vv)) for a, v, b, vv in zip(w_td, vd, w_tp, vp)]
    acc_bot = [unstack(_dot(a, v)) for a, v in zip(w_bd, vd)]
    carry_top = [c + row_sum(d) for c, d in zip(c_td, d_tp)]
    carry_bot = [row_sum(d) for d in d_bd]

    def earlier_chunks(qs, first, carry, acc):
        def alive_of(carry):
            return functools.reduce(jnp.minimum, [jnp.min(c) for c in carry]) < SB_SKIP

        def cond(st):
            return jnp.logical_and(st[0] >= 0, st[1])

        def body(st):
            c, _, carry, acc = st
            kc, vc = kv_rows(c, half)
            z = [_dot_nt(x, k) for x, k in zip(qs, kc)]
            d = [_decay(x.astype(BF16)) for x in z]
            s = [_dot(x, later_h) for x in d]
            w = [jnp.exp(zz - ss - cr).astype(BF16) for zz, ss, cr in zip(z, s, carry)]
            acc = [a + unstack(_dot(x, vv)) for a, x, vv in zip(acc, w, vc)]
            carry = [cr + row_sum(dd) for cr, dd in zip(carry, d)]
            return c - 1, alive_of(carry), carry, acc

        return lax.while_loop(cond, body, (first, alive_of(carry), carry, acc))[3]

    acc_top = earlier_chunks(q_top, 2 * qb - 2, carry_top, acc_top)
    acc_bot = earlier_chunks(q_bot, 2 * qb - 1, carry_bot, acc_bot)

    for t in tiles:
        for rows, acc in ((slice(0, half), acc_top[t]), (slice(half, blk), acc_bot[t])):
            sq = acc * acc
            s_lo = jnp.sum(jnp.where(head_masks[0], sq, 0.0), axis=-1, keepdims=True)
            s_all = jnp.sum(sq, axis=-1, keepdims=True)
            ms = jnp.where(head_masks[0], s_lo, s_all - s_lo) * (1.0 / HEAD_DIM)
            o_ref[0, rows, lanes_of(t)] = (acc * lax.rsqrt(ms + RMS_EPS) * g_ref[:, lanes_of(t)]).astype(BF16)


def sb_attn(u_sb, sb_out_g, B, S):
    blk = min(SB_BLK, S)
    half = blk // 2
    u3 = u_sb.reshape(B, S, SB_IN)
    n_pairs = SB_WIDTH // LANES
    after = lambda n: (jnp.arange(n)[:, None] >= jnp.arange(n)[None, :]).astype(BF16)
    before = (jnp.arange(half)[None, :] < jnp.arange(half)[:, None]).astype(BF16)
    mask_t = jnp.tile(before, (2, 1))
    mask_b = jnp.tile(jnp.concatenate([jnp.ones((half, half), BF16), before], axis=1), (2, 1))
    n_tiles = SB_TILES_PER_STEP
    width = n_tiles * LANES
    n_groups = n_pairs // n_tiles
    const = lambda r, c: pl.BlockSpec((r, c), lambda b, p, i: (0, 0))
    return pl.pallas_call(
        functools.partial(_sb_attn_body, blk=blk, n_tiles=n_tiles),
        grid=(B, n_groups, S // blk),
        in_specs=[
            pl.BlockSpec((1, blk, width), lambda b, p, i: (b, i, p)),
            pl.BlockSpec((1, S, width), lambda b, p, i: (b, 0, n_groups + p)),
            pl.BlockSpec((1, S, width), lambda b, p, i: (b, 0, 2 * n_groups + p)),
            pl.BlockSpec((1, width), lambda b, p, i: (0, p)),
            const(half, half), const(blk, blk), const(blk, half), const(blk, blk),
        ],
        out_specs=pl.BlockSpec((1, blk, width), lambda b, p, i: (b, i, p)),
        out_shape=jax.ShapeDtypeStruct((B, S, SB_WIDTH), BF16),
        compiler_params=_cparams(("arbitrary", "arbitrary", "arbitrary")),
        name="sb_attn",
    )(u3, u3, u3, sb_out_g, after(half), after(blk), mask_t, mask_b)


RW_CHUNK = LANES
RW_SEQS_PER_STEP = 4
RW_GROUP = MXU_DIM
RW_GROUP_HEADS = RW_GROUP // HEAD_DIM


def _softplus(y):
    return jnp.maximum(y, 0.0) + jnp.log(1.0 + jnp.exp(-jnp.abs(y)))


def _sigmoid(y):
    return 1.0 / (1.0 + jnp.exp(-y))


def _split_bf16(x):
    hi = x.astype(BF16)
    lo = (x - hi.astype(F32)).astype(BF16)
    return hi, lo


def _rwkv_body(u_ref, mu_ref, w0_ref, w2_ref, a0_ref, a2_ref, g2_ref, kk_ref, ka_ref, rk_ref,
               lnw_ref, lnb_ref, o_ref, prev_ref, state_ref, *, C, n_seqs):
    c = pl.program_id(1)
    G, GH = RW_GROUP, RW_GROUP_HEADS
    n_groups = RW_WIDTH // G

    @pl.when(c == 0)
    def _():
        prev_ref[...] = jnp.zeros_like(prev_ref)
        state_ref[...] = jnp.zeros_like(state_ref)

    head_bd = (lax.broadcasted_iota(jnp.int32, (G, G), 0) // HEAD_DIM
               == lax.broadcasted_iota(jnp.int32, (G, G), 1) // HEAD_DIM)
    ones_bd = head_bd.astype(BF16)
    stack_mask = (lax.broadcasted_iota(jnp.int32, (GH * C, G), 0) // C
                  == lax.broadcasted_iota(jnp.int32, (GH * C, G), 1) // HEAD_DIM)
    low_half = lax.broadcasted_iota(jnp.int32, (1, LANES), 1) < HEAD_DIM
    tt = lax.broadcasted_iota(jnp.int32, (C, GH * C), 0)
    ss = lax.broadcasted_iota(jnp.int32, (C, GH * C), 1) % C
    strict = ss < tt
    incl = ss <= tt
    tri_incl = (lax.broadcasted_iota(jnp.int32, (C, C), 1)
                <= lax.broadcasted_iota(jnp.int32, (C, C), 0)).astype(BF16)

    def head_sum(x):
        return _dot(x.astype(BF16), ones_bd)

    def stack(x):
        return jnp.where(stack_mask, jnp.concatenate([x] * GH, axis=0), 0.0).astype(BF16)

    def swap_halves(x):
        return pltpu.roll(x, HEAD_DIM, axis=1)

    seqs = range(n_seqs)
    units = [(n, gi) for n in seqs for gi in range(n_groups)]
    lanes_of = lambda gi: slice(gi * G, (gi + 1) * G)

    ums = []
    for n in seqs:
        u = u_ref[n].astype(F32)
        row_id = lax.broadcasted_iota(jnp.int32, (C, 1), 0)
        shifted = jnp.where(row_id == 0, prev_ref[n], pltpu.roll(u, 1, axis=0))
        prev_ref[n] = u[C - 1:C, :]
        ums.append(u + (shifted - u) * mu_ref[...])
    r = [um[:, 0:RW_WIDTH] for um in ums]
    k = [um[:, RW_WIDTH:2 * RW_WIDTH] for um in ums]
    v = [um[:, 2 * RW_WIDTH:3 * RW_WIDTH] for um in ums]
    xwa = [um[:, 3 * RW_WIDTH:3 * RW_WIDTH + LORA_W + LORA_A] for um in ums]
    xg = [um[:, 3 * RW_WIDTH + LORA_W + LORA_A:] for um in ums]

    lora_w = [_dot(jnp.tanh(x).astype(BF16), w2_ref[...]) for x in xwa]
    lora_a = [_dot(x.astype(BF16), a2_ref[...]) for x in xwa]
    gate = [_dot(_sigmoid(x).astype(BF16), g2_ref[...]) for x in xg]
    logdec = [-jnp.exp(-_softplus(-(w0_ref[...] + lw)) - 0.5) for lw in lora_w]
    lr = [_sigmoid(a0_ref[...] + la) for la in lora_a]

    splits = [_split_bf16(ld) for ld in logdec]
    cum = [_dot(tri_incl, hi) + _dot(tri_incl, lo) for hi, lo in splits]
    p_incl = [jnp.exp(cm) for cm in cum]
    p_prev = [jnp.exp(cm - ld) for cm, ld in zip(cum, logdec)]
    p_inv = [jnp.exp(-cm) for cm in cum]
    p_last = [p[C - 1:C, :] for p in p_incl]

    kk = [kn * kk_ref[...] for kn in k]
    k_adj = [kn * (1.0 + (lrn - 1.0) * ka_ref[...]) for kn, lrn in zip(k, lr)]
    rk_prod = [rn * kan * rk_ref[...] for rn, kan in zip(r, k_adj)]

    kk_ssq = [head_sum(kk[n][:, lanes_of(gi)] * kk[n][:, lanes_of(gi)]) for n, gi in units]
    kkn = [kk[n][:, lanes_of(gi)] * lax.rsqrt(jnp.maximum(s, 1e-24)) for (n, gi), s in zip(units, kk_ssq)]
    v_g = [v[n][:, lanes_of(gi)] for n, gi in units]
    at = [-kn * p_prev[n][:, lanes_of(gi)] for (n, gi), kn in zip(units, kkn)]
    bt = [kn * lr[n][:, lanes_of(gi)] * p_inv[n][:, lanes_of(gi)] for (n, gi), kn in zip(units, kkn)]
    kt = [k_adj[n][:, lanes_of(gi)] * p_inv[n][:, lanes_of(gi)] for n, gi in units]
    rt = [r[n][:, lanes_of(gi)] * p_incl[n][:, lanes_of(gi)] for n, gi in units]

    lhs2 = [jnp.concatenate([a, q], axis=0).astype(BF16) for a, q in zip(at, rt)]
    ab = [_dot_nt(l2, stack(b)) for l2, b in zip(lhs2, bt)]
    ak = [_dot_nt(l2, stack(kx)) for l2, kx in zip(lhs2, kt)]
    a_ab = [jnp.where(strict, x[:C], 0.0) for x in ab]
    q_ab = [jnp.where(incl, x[C:], 0.0) for x in ab]
    a_ak = [jnp.where(strict, x[:C], 0.0) for x in ak]
    q_ak = [jnp.where(incl, x[C:], 0.0) for x in ak]
    vs = [stack(x) for x in v_g]
    akv = [_dot(a.astype(BF16), s) for a, s in zip(a_ak, vs)]

    n_steps = max(1, (C - 1).bit_length())
    tiles_per_group = G // LANES
    ys, lps = [], []
    for ui in range(len(units)):
        for p in range(tiles_per_group):
            at_p = at[ui][:, p * LANES:(p + 1) * LANES]
            akv_p = akv[ui][:, p * LANES:(p + 1) * LANES]
            ys.append(jnp.where(low_half, at_p, swap_halves(akv_p)))
            ys.append(jnp.where(low_half, swap_halves(at_p), akv_p))
            for e in range(2):
                h = 2 * p + e
                lps.append(a_ab[ui][:, h * C:(h + 1) * C])
    for step in range(n_steps):
        if step == n_steps - 1:
            ys = [y + _dot(lp.astype(BF16), y.astype(BF16)) for y, lp in zip(ys, lps)]
        else:
            prods = [_dot(lp.astype(BF16), jnp.concatenate([y, lp], axis=1).astype(BF16))
                     for y, lp in zip(ys, lps)]
            ys = [y + pr[:, 0:LANES] for y, pr in zip(ys, prods)]
            lps = [pr[:, LANES:] for pr in prods]
    y1, y2 = [], []
    for ui in range(len(units)):
        w_tiles, u0_tiles = [], []
        for p in range(tiles_per_group):
            y_even, y_odd = ys[ui * GH + 2 * p], ys[ui * GH + 2 * p + 1]
            w_tiles.append(jnp.where(low_half, y_even, swap_halves(y_odd)))
            u0_tiles.append(jnp.where(low_half, swap_halves(y_even), y_odd))
        y1.append(jnp.concatenate(w_tiles, axis=1))
        y2.append(jnp.concatenate(u0_tiles, axis=1))

    s0 = [state_ref[n * n_groups + gi] for n, gi in units]
    s0b = [s.astype(BF16) for s in s0]
    uu = [_dot_nt(a.astype(BF16), s) + b for a, s, b in zip(y1, s0b, y2)]
    o_state = [_dot_nt(q.astype(BF16), s) for q, s in zip(rt, s0b)]
    o_u = [_dot(q.astype(BF16), stack(x)) for q, x in zip(q_ab, uu)]
    o_v = [_dot(q.astype(BF16), s) for q, s in zip(q_ak, vs)]
    upd = [_dot_tn(jnp.concatenate([x, vv], axis=0).astype(BF16),
                   jnp.concatenate([b * p_last[n][:, lanes_of(gi)], kx * p_last[n][:, lanes_of(gi)]],
                                   axis=0).astype(BF16))
           for (n, gi), x, vv, b, kx in zip(units, uu, v_g, bt, kt)]
    for (n, gi), s, up in zip(units, s0, upd):
        state_ref[n * n_groups + gi] = s * p_last[n][:, lanes_of(gi)] + jnp.where(head_bd, up, 0.0)
    o_g = [a + b + c_ for a, b, c_ in zip(o_state, o_u, o_v)]

    mean = [head_sum(x) * (1.0 / HEAD_DIM) for x in o_g]
    dev = [x - m for x, m in zip(o_g, mean)]
    var = [head_sum(d * d) * (1.0 / HEAD_DIM) for d in dev]
    rk_sum = [head_sum(rk_prod[n][:, lanes_of(gi)]) for n, gi in units]
    outs = []
    for (n, gi), d, vr, rs, vv in zip(units, dev, var, rk_sum, v_g):
        sl = lanes_of(gi)
        gn = d * lax.rsqrt(vr + GN_EPS) * lnw_ref[:, sl] + lnb_ref[:, sl]
        outs.append((gn + rs * vv) * gate[n][:, sl])
    for n in seqs:
        o_ref[n] = jnp.concatenate(outs[n * n_groups:(n + 1) * n_groups], axis=1).astype(BF16)


def rwkv(u_rw, shift_mu, w0, w2_pad, a0, a2_pad, g2, k_k, k_a, r_k, ln_w, ln_b, B, S):
    C = RW_CHUNK
    n_seqs = RW_SEQS_PER_STEP if B % RW_SEQS_PER_STEP == 0 else 1
    assert S % C == 0
    u3 = u_rw.reshape(B, S, RW_IN)
    vec = lambda n: pl.BlockSpec((1, n), lambda b, c: (0, 0))
    mat = lambda m, n: pl.BlockSpec((m, n), lambda b, c: (0, 0))
    return pl.pallas_call(
        functools.partial(_rwkv_body, C=C, n_seqs=n_seqs),
        grid=(B // n_seqs, S // C),
        in_specs=[
            pl.BlockSpec((n_seqs, C, RW_IN), lambda b, c: (b, c, 0)),
            vec(RW_IN), vec(RW_WIDTH), mat(LORA_W + LORA_A, RW_WIDTH), vec(RW_WIDTH),
            mat(LORA_W + LORA_A, RW_WIDTH), mat(LORA_G, RW_WIDTH),
            vec(RW_WIDTH), vec(RW_WIDTH), vec(RW_WIDTH), vec(RW_WIDTH), vec(RW_WIDTH),
        ],
        out_specs=pl.BlockSpec((n_seqs, C, RW_WIDTH), lambda b, c: (b, c, 0)),
        out_shape=jax.ShapeDtypeStruct((B, S, RW_WIDTH), BF16),
        scratch_shapes=[
            pltpu.VMEM((n_seqs, 1, RW_IN), F32),
            pltpu.VMEM((n_seqs * (RW_WIDTH // RW_GROUP), RW_GROUP, RW_GROUP), F32),
        ],
        compiler_params=_cparams(("arbitrary", "arbitrary")),
        name="rwkv",
    )(u3, shift_mu, w0, w2_pad, a0, a2_pad, g2, k_k, k_a, r_k, ln_w, ln_b)


OR_TM = 1024
OR_PARTS = 2
N_PAIRS = EXPERTS_PER_GROUP * (EXPERTS_PER_GROUP - 1) // 2
N_CLASSES = N_GROUPS * N_PAIRS
CLS_ROWS = LANES
ROUTER_ROWS = 48
SUBLANES = 8
H_CHUNKS = D_MODEL // LANES
SLAB_IN = H_CHUNKS + 1
SLAB_OUT = H_CHUNKS


def _first_index_of(vals, target, row_f):
    return jnp.min(jnp.where(vals == target, row_f, 1e9), axis=0, keepdims=True)


def _out_router_body(x_ref, osb_ref, orw_ref, wo_ref, g_ref, wr_ref, wrhi_ref, br_ref,
                     slab_ref, meta_ref, hist_ref, carry_ref, *, tm):
    i = pl.program_id(0)

    @pl.when(i == 0)
    def _():
        carry_ref[...] = jnp.zeros_like(carry_ref)

    parts = range(OR_PARTS)
    tp = tm // OR_PARTS
    rows_of = lambda p: slice(p * tp, (p + 1) * tp)
    hs = [x_ref[rows_of(p), :] + _dot(osb_ref[rows_of(p), :], wo_ref[0:SB_WIDTH, :])
          + _dot(orw_ref[rows_of(p), :], wo_ref[SB_WIDTH:, :]) for p in parts]
    carry = carry_ref[...]
    for p in parts:
        carry = _route_part(p, tp, hs[p], carry, g_ref, wr_ref, wrhi_ref, br_ref, slab_ref, meta_ref)
    carry_ref[...] = carry
    hist_ref[...] = carry


def _route_part(p, tm, h, carry, g_ref, wr_ref, wrhi_ref, br_ref, slab_ref, meta_ref):
    ms = jnp.mean(h * h, axis=-1, keepdims=True)
    xn = h * lax.rsqrt(ms + RMS_EPS) * g_ref[...]

    xn_hi, xn_lo = _split_bf16(xn)
    both = _dot_nt(wr_ref[...], xn_hi)
    logits = both[:ROUTER_ROWS] + both[ROUTER_ROWS:] + _dot_nt(wrhi_ref[...], xn_lo) + br_ref[...][:, 0:1]

    row_f = lax.broadcasted_iota(jnp.int32, (SUBLANES, tm), 0).astype(F32)
    lg = jnp.where(row_f < N_GROUPS, logits[N_EXPERTS:N_EXPERTS + SUBLANES], -jnp.inf)
    eg = jnp.exp(lg - jnp.max(lg, axis=0, keepdims=True))
    pg = eg / jnp.sum(eg, axis=0, keepdims=True)
    g_val = jnp.max(pg, axis=0, keepdims=True)
    g_idx = _first_index_of(pg, g_val, row_f)
    sel = jnp.zeros((SUBLANES, tm), F32)
    for g in range(N_GROUPS):
        sel = jnp.where(g_idx == g, logits[g * EXPERTS_PER_GROUP:(g + 1) * EXPERTS_PER_GROUP], sel)
    ee = jnp.exp(sel - jnp.max(sel, axis=0, keepdims=True))
    pe = ee / jnp.sum(ee, axis=0, keepdims=True)
    e1 = jnp.max(pe, axis=0, keepdims=True)
    i1 = _first_index_of(pe, e1, row_f)
    pe2 = jnp.where(row_f == i1, -1.0, pe)
    e2 = jnp.max(pe2, axis=0, keepdims=True)
    i2 = _first_index_of(pe2, e2, row_f)
    den = e1 + e2
    wt1 = g_val * e1 / den
    wt2 = g_val * e2 / den
    first_lo = i1 < i2
    lo = jnp.where(first_lo, i1, i2)
    hi = jnp.where(first_lo, i2, i1)
    w_lo = jnp.where(first_lo, wt1, wt2)
    w_hi = jnp.where(first_lo, wt2, wt1)
    pair = lo * (2 * EXPERTS_PER_GROUP - 1 - lo) * 0.5 + (hi - lo - 1.0)
    cls = g_idx * N_PAIRS + pair

    cls_row = lax.broadcasted_iota(jnp.int32, (CLS_ROWS, tm), 0).astype(F32)
    onehot = cls_row == cls
    onehot_b = jnp.where(onehot, 1.0, 0.0).astype(BF16)
    upto = (lax.broadcasted_iota(jnp.int32, (tm, tm), 0)
            <= lax.broadcasted_iota(jnp.int32, (tm, tm), 1)).astype(BF16)
    cum = _dot(onehot_b, upto)
    tot = _dot(onehot_b, jnp.ones((tm, LANES), BF16))
    before = jnp.concatenate([carry] * (tm // LANES), axis=1)
    rank = jnp.sum(jnp.where(onehot, cum - 1.0 + before, 0.0), axis=0, keepdims=True)

    meta = jnp.concatenate([cls, rank, jnp.zeros((SUBLANES - 2, tm), F32)], axis=0)
    meta_ref[0, :, p * tm:(p + 1) * tm] = meta.astype(jnp.int32)

    w_rows = jnp.concatenate([w_lo, w_hi, jnp.zeros((LANES - 2, tm), F32)], axis=0)
    slab_rows = lambda j: pl.ds(p * tm * SLAB_IN + j, tm, stride=SLAB_IN)
    for j in range(H_CHUNKS):
        slab_ref[slab_rows(j), :] = h[:, j * LANES:(j + 1) * LANES]
    slab_ref[slab_rows(H_CHUNKS), :] = jnp.transpose(w_rows)
    return carry + tot


def out_router(x2d, o_sb, o_rw, w_out, g_ffn, wr_both, wr_hi, b_r):
    T = x2d.shape[0]
    tm = min(OR_TM, T)
    nt = T // tm
    const = lambda *shape: pl.BlockSpec(shape, lambda i: (0,) * len(shape))
    return pl.pallas_call(
        functools.partial(_out_router_body, tm=tm),
        grid=(nt,),
        in_specs=[
            pl.BlockSpec((tm, D_MODEL), lambda i: (i, 0)),
            pl.BlockSpec((tm, SB_WIDTH), lambda i: (i, 0)),
            pl.BlockSpec((tm, RW_WIDTH), lambda i: (i, 0)),
            const(D_MODEL, D_MODEL), const(1, D_MODEL),
            const(2 * ROUTER_ROWS, D_MODEL), const(ROUTER_ROWS, D_MODEL), const(ROUTER_ROWS, LANES),
        ],
        out_specs=[
            pl.BlockSpec((tm * SLAB_IN, LANES), lambda i: (i, 0)),
            pl.BlockSpec((1, SUBLANES, tm), lambda i: (i, 0, 0)),
            const(CLS_ROWS, LANES),
        ],
        out_shape=[
            jax.ShapeDtypeStruct((T * SLAB_IN, LANES), F32),
            jax.ShapeDtypeStruct((nt, SUBLANES, tm), jnp.int32),
            jax.ShapeDtypeStruct((CLS_ROWS, LANES), F32),
        ],
        scratch_shapes=[pltpu.VMEM((CLS_ROWS, LANES), F32)],
        compiler_params=_cparams(("arbitrary",)),
        name="out_router",
    )(x2d, o_sb, o_rw, w_out, g_ffn, wr_both, wr_hi, b_r)


PERM_TOKENS_PER_STEP = 1024
PERM_WINDOW = 256
PERM_UNROLL = 8


def _slab_scatter_body(pos_ref, src_ref, dst_ref, sem, *, tokens, slab):
    window = min(PERM_WINDOW, tokens)

    def slab_copy(j, dst_tok):
        return pltpu.make_async_copy(src_ref.at[pl.ds(j * slab, slab)],
                                     dst_ref.at[pl.ds(dst_tok * slab, slab)], sem)

    def start(j, carry):
        slab_copy(j, pos_ref[0, 0, j]).start()
        return carry

    def retire(j, carry):
        slab_copy(0, 0).wait()
        return carry

    def start_batch_retire_batch(b, carry):
        lax.fori_loop(0, PERM_UNROLL, lambda j, c: start(b * PERM_UNROLL + j, c), 0, unroll=True)
        lax.fori_loop(0, PERM_UNROLL, retire, 0, unroll=True)
        return carry

    lax.fori_loop(0, window, start, 0, unroll=PERM_UNROLL)
    lax.fori_loop(window // PERM_UNROLL, tokens // PERM_UNROLL, start_batch_retire_batch, 0)
    lax.fori_loop(0, window, retire, 0, unroll=PERM_UNROLL)


def slab_scatter(src, pos, slab):
    n_tok = pos.shape[0]
    tokens = min(PERM_TOKENS_PER_STEP, n_tok)
    steps = n_tok // tokens
    return pl.pallas_call(
        functools.partial(_slab_scatter_body, tokens=tokens, slab=slab),
        grid=(steps,),
        in_specs=[
            pl.BlockSpec((1, 1, tokens), lambda s: (s, 0, 0), memory_space=pltpu.SMEM),
            pl.BlockSpec((tokens * slab, LANES), lambda s: (s, 0)),
        ],
        out_specs=pl.BlockSpec(memory_space=pl.ANY),
        out_shape=jax.ShapeDtypeStruct(src.shape, src.dtype),
        scratch_shapes=[pltpu.SemaphoreType.DMA(())],
        compiler_params=_cparams(("arbitrary",)),
        name="slab_scatter",
    )(pos.reshape(steps, 1, tokens), src)


GATHER_TM = 512


def _slab_gather_body(pos_ref, nxt_ref, src_ref, g_ref, o_ref, buf, sem, *, tm):
    i = pl.program_id(0)
    n_steps = pl.num_programs(0)
    slot = i % 2

    def slab_copy(tok, j, to_slot):
        start = pl.multiple_of(tok * SLAB_OUT, SLAB_OUT)
        return pltpu.make_async_copy(src_ref.at[pl.ds(start, SLAB_OUT)],
                                     buf.at[to_slot, pl.ds(j * SLAB_OUT, SLAB_OUT)], sem.at[to_slot])

    def fetch(idx_ref, to_slot):
        def issue(j, carry):
            slab_copy(idx_ref[0, 0, j], j, to_slot).start()
            return carry
        lax.fori_loop(0, tm, issue, 0, unroll=PERM_UNROLL)

    @pl.when(i == 0)
    def _():
        fetch(pos_ref, 0)

    @pl.when(i + 1 < n_steps)
    def _():
        fetch(nxt_ref, 1 - slot)

    def drain(j, carry):
        slab_copy(0, 0, slot).wait()
        return carry

    lax.fori_loop(0, tm, drain, 0, unroll=PERM_UNROLL)
    chunks = [buf[slot, pl.ds(j, tm, stride=SLAB_OUT), :] for j in range(H_CHUNKS)]
    squares = functools.reduce(lambda a, b: a + b, [x * x for x in chunks])
    scale = lax.rsqrt(jnp.sum(squares, axis=-1, keepdims=True) * (1.0 / D_MODEL) + RMS_EPS)
    for j in range(H_CHUNKS):
        o_ref[:, j * LANES:(j + 1) * LANES] = chunks[j] * scale * g_ref[:, j * LANES:(j + 1) * LANES]


def slab_gather(src, pos, g_final):
    n_tok = pos.shape[0]
    tm = min(GATHER_TM, n_tok)
    steps = n_tok // tm
    pos3 = pos.reshape(steps, 1, tm)
    return pl.pallas_call(
        functools.partial(_slab_gather_body, tm=tm),
        grid=(steps,),
        in_specs=[
            pl.BlockSpec((1, 1, tm), lambda i: (i, 0, 0), memory_space=pltpu.SMEM),
            pl.BlockSpec((1, 1, tm), lambda i: (jnp.minimum(i + 1, steps - 1), 0, 0), memory_space=pltpu.SMEM),
            pl.BlockSpec(memory_space=pl.ANY),
            pl.BlockSpec((1, D_MODEL), lambda i: (0, 0)),
        ],
        out_specs=pl.BlockSpec((tm, D_MODEL), lambda i: (i, 0)),
        out_shape=jax.ShapeDtypeStruct((n_tok, D_MODEL), src.dtype),
        scratch_shapes=[
            pltpu.VMEM((2, tm * SLAB_OUT, LANES), src.dtype),
            pltpu.SemaphoreType.DMA((2,)),
        ],
        compiler_params=_cparams(("arbitrary",)),
        name="slab_gather",
    )(pos3, pos3, src, g_final)


EX_TM = 256


def _experts_body(tile_ref, lo_ref, hi_ref, start_ref, end_ref,
                  hs_ref, gffn_ref, wg_lo, wu_lo, wd_lo, wg_hi, wu_hi, wd_hi, o_ref, *, tm):
    i = pl.program_id(0)
    tile = tile_ref[i]
    start = start_ref[i]
    end = end_ref[i]

    @pl.when(jnp.logical_or(i == 0, tile != tile_ref[jnp.maximum(i - 1, 0)]))
    def _():
        o_ref[...] = jnp.zeros_like(o_ref)

    @pl.when(end > start)
    def _():
        in_rows = lambda j: pl.ds(j, tm, stride=SLAB_IN)
        out_rows = lambda j: pl.ds(j, tm, stride=SLAB_OUT)
        h = jnp.concatenate([hs_ref[in_rows(j), :] for j in range(H_CHUNKS)], axis=1)
        w_row = hs_ref[in_rows(H_CHUNKS), :]
        w_lo = w_row[:, 0:1]
        w_hi = w_row[:, 1:2]
        ms = jnp.mean(h * h, axis=-1, keepdims=True)
        xn = (h * lax.rsqrt(ms + RMS_EPS) * gffn_ref[...]).astype(BF16)

        g_lo = _dot(xn, wg_lo[0])
        g_hi = _dot(xn, wg_hi[0])
        u_lo = _dot(xn, wu_lo[0])
        u_hi = _dot(xn, wu_hi[0])
        a_lo = (g_lo * _sigmoid(g_lo) * u_lo * w_lo).astype(BF16)
        a_hi = (g_hi * _sigmoid(g_hi) * u_hi * w_hi).astype(BF16)
        res = h + _dot(a_lo, wd_lo[0]) + _dot(a_hi, wd_hi[0])
        rows = tile * tm + lax.broadcasted_iota(jnp.int32, (tm, 1), 0)
        mine = jnp.logical_and(rows >= start, rows < end)
        for j in range(H_CHUNKS):
            o_ref[out_rows(j), :] = jnp.where(mine, res[:, j * LANES:(j + 1) * LANES], o_ref[out_rows(j), :])


def experts(h_sorted, g_ffn, w_gate, w_up, w_down, item_tile, item_lo, item_hi, item_start, item_end):
    T = h_sorted.shape[0] // SLAB_IN
    tm = min(EX_TM, T)
    n_items = item_tile.shape[0]
    gate_spec = lambda which: pl.BlockSpec(
        (1, D_MODEL, D_EXPERT), lambda i, tl, lo, hi, st, en: ((lo, hi)[which][i], 0, 0))
    down_spec = lambda which: pl.BlockSpec(
        (1, D_EXPERT, D_MODEL), lambda i, tl, lo, hi, st, en: ((lo, hi)[which][i], 0, 0))
    vec_spec = pl.BlockSpec((1, D_MODEL), lambda i, tl, lo, hi, st, en: (0, 0))
    grid_spec = pltpu.PrefetchScalarGridSpec(
        num_scalar_prefetch=5,
        grid=(n_items,),
        in_specs=[
            pl.BlockSpec((tm * SLAB_IN, LANES), lambda i, tl, lo, hi, st, en: (tl[i], 0)),
            vec_spec,
            gate_spec(0), gate_spec(0), down_spec(0),
            gate_spec(1), gate_spec(1), down_spec(1),
        ],
        out_specs=pl.BlockSpec((tm * SLAB_OUT, LANES), lambda i, tl, lo, hi, st, en: (tl[i], 0)),
    )
    return pl.pallas_call(
        functools.partial(_experts_body, tm=tm),
        grid_spec=grid_spec,
        out_shape=jax.ShapeDtypeStruct((T * SLAB_OUT, LANES), F32),
        compiler_params=_cparams(("arbitrary",)),
        name="experts",
    )(item_tile, item_lo, item_hi, item_start, item_end,
      h_sorted, g_ffn, w_gate, w_up, w_down, w_gate, w_up, w_down)


def _pair_first(lo):
    return (lo * (2 * EXPERTS_PER_GROUP - 1 - lo)) // 2


def _work_items(hist, meta, T, tm):
    i32 = jnp.int32
    counts = hist[:N_CLASSES, 0].astype(i32)
    offs = jnp.cumsum(counts) - counts
    cls = meta[:, 0, :].reshape(T)
    rank = meta[:, 1, :].reshape(T)
    class_ids = jnp.arange(N_CLASSES, dtype=i32)
    pos = jnp.sum(jnp.where(cls[:, None] == class_ids[None, :], offs[None, :], 0), axis=1) + rank

    n_tiles = T // tm
    n_items = n_tiles + N_CLASSES
    tile_starts = jnp.arange(n_tiles, dtype=i32) * tm
    tile_slot = jnp.arange(n_tiles, dtype=i32) + jnp.sum(offs[None, :] < tile_starts[:, None], axis=1, dtype=i32)
    class_slot = class_ids + jnp.sum(tile_starts[None, :] <= offs[:, None], axis=1, dtype=i32)
    slots = jnp.arange(n_items, dtype=i32)
    starts = (jnp.sum(jnp.where(tile_slot[None, :] == slots[:, None], tile_starts[None, :], 0), axis=1)
              + jnp.sum(jnp.where(class_slot[None, :] == slots[:, None], offs[None, :], 0), axis=1))
    ends = jnp.concatenate([starts[1:], jnp.full((1,), T, i32)])
    item_tile = jnp.minimum(starts // tm, n_tiles - 1)
    item_cls = jnp.sum(offs[None, :] <= starts[:, None], axis=1, dtype=i32) - 1
    grp = item_cls // N_PAIRS
    q = item_cls % N_PAIRS
    firsts = _pair_first(jnp.arange(1, EXPERTS_PER_GROUP - 1, dtype=i32))
    lo = jnp.sum(q[:, None] >= firsts[None, :], axis=1, dtype=i32)
    hi = q - _pair_first(lo) + lo + 1
    base = grp * EXPERTS_PER_GROUP
    return pos, item_tile, base + lo, base + hi, starts, ends


def kernel(x, norm_mix_g, w_in, shift_mu, sb_out_g, rw_w0, rw_w2, rw_a0, rw_a2, rw_g2, rw_k_k, rw_k_a, rw_r_k, rw_ln_w, rw_ln_b, w_out, norm_ffn_g, router_grp_w, router_grp_b, router_exp_w, router_exp_b, exp_w_gate, exp_w_up, exp_w_down, final_norm_g):
    B, S, D = x.shape
    T = B * S
    assert D == D_MODEL and w_in.shape[0] == 1, "one layer of width 1024 is what these kernels implement"
    l = 0
    row = lambda a: a.reshape(1, -1)
    x2d = x.reshape(T, D)

    w_in_b = w_in[l].astype(BF16)
    u_sb, u_rw = in_proj(x2d, row(norm_mix_g[l]), w_in_b[:, :SB_IN], w_in_b[:, SB_IN:])
    o_sb = sb_attn(u_sb, row(sb_out_g[l]), B, S)
    w2_pad = jnp.concatenate([rw_w2[l], jnp.zeros_like(rw_a2[l])], axis=0).astype(BF16)
    a2_pad = jnp.concatenate([jnp.zeros_like(rw_w2[l]), rw_a2[l]], axis=0).astype(BF16)
    o_rw = rwkv(u_rw, row(shift_mu[l]), row(rw_w0[l]), w2_pad, row(rw_a0[l]), a2_pad, rw_g2[l].astype(BF16),
                row(rw_k_k[l]), row(rw_k_a[l]), row(rw_r_k[l]), row(rw_ln_w[l]), row(rw_ln_b[l]), B, S)

    pad_rows = ROUTER_ROWS - N_EXPERTS - N_GROUPS
    wr = jnp.concatenate([router_exp_w[l].T, router_grp_w[l].T, jnp.zeros((pad_rows, D), F32)], axis=0)
    wr_hi, wr_lo = _split_bf16(wr)
    b_r = jnp.concatenate([router_exp_b[l], router_grp_b[l], jnp.zeros((pad_rows,), F32)])
    b_r = jnp.broadcast_to(b_r[:, None], (ROUTER_ROWS, LANES))
    h_slabs, meta, hist = out_router(x2d, o_sb.reshape(T, SB_WIDTH), o_rw.reshape(T, RW_WIDTH),
                                     w_out[l].astype(BF16), row(norm_ffn_g[l]),
                                     jnp.concatenate([wr_hi, wr_lo], axis=0), wr_hi, b_r)

    tm = min(EX_TM, T)
    pos, item_tile, item_lo, item_hi, item_start, item_end = _work_items(hist, meta, T, tm)
    h_sorted = slab_scatter(h_slabs, pos, SLAB_IN)
    out_sorted = experts(h_sorted, row(norm_ffn_g[l]),
                         exp_w_gate[l].astype(BF16), exp_w_up[l].astype(BF16), exp_w_down[l].astype(BF16),
                         item_tile, item_lo, item_hi, item_start, item_end)
    out = slab_gather(out_sorted, pos, row(final_norm_g))
    return out.reshape(B, S, D)
```

```python
import functools
import math

import jax
import jax.numpy as jnp
from jax import lax
from jax.experimental import pallas as pl
from jax.experimental.pallas import tpu as pltpu

F32 = jnp.float32
BF16 = jnp.bfloat16

D_MODEL = 1024
SB_HEADS = 8
HEAD_DIM = 64
SB_WIDTH = SB_HEADS * HEAD_DIM
RW_HEADS = 8
RW_WIDTH = RW_HEADS * HEAD_DIM
LORA_W = 64
LORA_A = 64
LORA_G = 128
SB_IN = 3 * SB_WIDTH
RW_IN = 3 * RW_WIDTH + LORA_W + LORA_A + LORA_G
N_GROUPS = 4
EXPERTS_PER_GROUP = 8
N_EXPERTS = N_GROUPS * EXPERTS_PER_GROUP
D_EXPERT = 256
RMS_EPS = 1e-6
GN_EPS = 64e-5

LANES = 128
MXU_DIM = 256
VMEM_LIMIT = 48 * 1024 * 1024


def _cparams(sem):
    return pltpu.CompilerParams(dimension_semantics=sem, vmem_limit_bytes=VMEM_LIMIT)


def _dot(a, b):
    return jnp.dot(a, b, preferred_element_type=F32)


def _dot_nt(a, b):
    return lax.dot_general(a, b, (((1,), (1,)), ((), ())), preferred_element_type=F32)


def _dot_tn(a, b):
    return lax.dot_general(a, b, (((0,), (0,)), ((), ())), preferred_element_type=F32)


IN_TM = 512


def _in_proj_body(x_ref, g_ref, wsb_ref, wrw_ref, usb_ref, urw_ref):
    x = x_ref[...]
    ms = jnp.mean(x * x, axis=-1, keepdims=True)
    xn = (x * lax.rsqrt(ms + RMS_EPS) * g_ref[...]).astype(BF16)
    usb_ref[...] = _dot(xn, wsb_ref[...]).astype(BF16)
    urw_ref[...] = _dot(xn, wrw_ref[...]).astype(BF16)


def in_proj(x2d, g, w_sb, w_rw):
    T = x2d.shape[0]
    tm = min(IN_TM, T)
    return pl.pallas_call(
        _in_proj_body,
        grid=(T // tm,),
        in_specs=[
            pl.BlockSpec((tm, D_MODEL), lambda i: (i, 0)),
            pl.BlockSpec((1, D_MODEL), lambda i: (0, 0)),
            pl.BlockSpec((D_MODEL, SB_IN), lambda i: (0, 0)),
            pl.BlockSpec((D_MODEL, RW_IN), lambda i: (0, 0)),
        ],
        out_specs=[
            pl.BlockSpec((tm, SB_IN), lambda i: (i, 0)),
            pl.BlockSpec((tm, RW_IN), lambda i: (i, 0)),
        ],
        out_shape=[
            jax.ShapeDtypeStruct((T, SB_IN), BF16),
            jax.ShapeDtypeStruct((T, RW_IN), BF16),
        ],
        compiler_params=_cparams(("arbitrary",)),
        name="in_proj",
    )(x2d, g, w_sb, w_rw)


SB_BLK = 256
SB_TILES_PER_STEP = 4
SB_SKIP = 40.0


def _decay(z):
    one = jnp.asarray(1.0, z.dtype)
    zero = jnp.asarray(0.0, z.dtype)
    return jnp.maximum(z, zero) + jnp.log(one + jnp.exp(-jnp.abs(z)))


def _sb_attn_body(q_ref, k_ref, v_ref, g_ref, later_h_ref, later_f_ref, mask_t_ref, mask_b_ref, o_ref,
                  *, blk, n_tiles):
    qb = pl.program_id(2)
    half = blk // 2
    tiles = range(n_tiles)
    lane = lax.broadcasted_iota(jnp.int32, (1, LANES), 1)
    head_masks = [lane < HEAD_DIM, lane >= HEAD_DIM]
    lanes_of = lambda t: slice(t * LANES, (t + 1) * LANES)

    def stack_heads(x):
        return jnp.concatenate([jnp.where(m, x, 0.0) for m in head_masks], axis=0).astype(BF16)

    def unstack(pv):
        n = pv.shape[0] // 2
        return jnp.where(head_masks[0], pv[:n], pv[n:])

    def row_sum(d):
        return jnp.sum(d.astype(F32), axis=-1, keepdims=True)

    q = [q_ref[0, :, lanes_of(t)].astype(F32) * (1.0 / math.sqrt(HEAD_DIM)) for t in tiles]
    q_top = [stack_heads(x[:half]) for x in q]
    q_bot = [stack_heads(x[half:]) for x in q]

    later_h = later_h_ref[...]
    later_f = later_f_ref[...]
    mask_t = mask_t_ref[...]
    mask_b = mask_b_ref[...]

    def kv_rows(index, size):
        start = pl.multiple_of(index * size, size)
        return ([k_ref[0, pl.ds(start, size), lanes_of(t)] for t in tiles],
                [v_ref[0, pl.ds(start, size), lanes_of(t)] for t in tiles])

    has_prev = qb >= 1
    kd, vd = kv_rows(qb, blk)
    kp, vp = kv_rows(jnp.maximum(2 * qb - 1, 0), half)
    z_td = [_dot_nt(x, k[:half]) for x, k in zip(q_top, kd)]
    z_bd = [_dot_nt(x, k) for x, k in zip(q_bot, kd)]
    z_tp = [_dot_nt(x, k) for x, k in zip(q_top, kp)]
    d_td = [_decay(z.astype(BF16)) * mask_t for z in z_td]
    d_bd = [_decay(z.astype(BF16)) * mask_b for z in z_bd]
    d_tp = [_decay(z.astype(BF16)) for z in z_tp]
    s_td = [_dot(d, later_h) for d in d_td]
    s_bd = [_dot(d, later_f) for d in d_bd]
    s_tp = [_dot(d, later_h) for d in d_tp]
    c_td = [row_sum(d) for d in d_td]
    w_td = [jnp.exp(jnp.minimum(z - s, 0.0)).astype(BF16) * mask_t for z, s in zip(z_td, s_td)]
    w_bd = [jnp.exp(jnp.minimum(z - s, 0.0)).astype(BF16) * mask_b for z, s in zip(z_bd, s_bd)]
    w_tp = [jnp.where(has_prev, jnp.exp(z - s - c), 0.0).astype(BF16) for z, s, c in zip(z_tp, s_tp, c_td)]
    acc_top = [unstack(_dot(a, v[:half]) + _dot(b, vv)) for a, v, b, vv in zip(w_td, vd, w_tp, vp)]
    acc_bot = [unstack(_dot(a, v)) for a, v in zip(w_bd, vd)]
    carry_top = [c + row_sum(d) for c, d in zip(c_td, d_tp)]
    carry_bot = [row_sum(d) for d in d_bd]

    def earlier_chunks(qs, first, carry, acc):
        def alive_of(carry):
            return functools.reduce(jnp.minimum, [jnp.min(c) for c in carry]) < SB_SKIP

        def cond(st):
            return jnp.logical_and(st[0] >= 0, st[1])

        def body(st):
            c, _, carry, acc = st
            kc, vc = kv_rows(c, half)
            z = [_dot_nt(x, k) for x, k in zip(qs, kc)]
            d = [_decay(x.astype(BF16)) for x in z]
            s = [_dot(x, later_h) for x in d]
            w = [jnp.exp(zz - ss - cr).astype(BF16) for zz, ss, cr in zip(z, s, carry)]
            acc = [a + unstack(_dot(x, vv)) for a, x, vv in zip(acc, w, vc)]
            carry = [cr + row_sum(dd) for cr, dd in zip(carry, d)]
            return c - 1, alive_of(carry), carry, acc

        return lax.while_loop(cond, body, (first, alive_of(carry), carry, acc))[3]

    acc_top = earlier_chunks(q_top, 2 * qb - 2, carry_top, acc_top)
    acc_bot = earlier_chunks(q_bot, 2 * qb - 1, carry_bot, acc_bot)

    for t in tiles:
        for rows, acc in ((slice(0, half), acc_top[t]), (slice(half, blk), acc_bot[t])):
            sq = acc * acc
            s_lo = jnp.sum(jnp.where(head_masks[0], sq, 0.0), axis=-1, keepdims=True)
            s_all = jnp.sum(sq, axis=-1, keepdims=True)
            ms = jnp.where(head_masks[0], s_lo, s_all - s_lo) * (1.0 / HEAD_DIM)
            o_ref[0, rows, lanes_of(t)] = (acc * lax.rsqrt(ms + RMS_EPS) * g_ref[:, lanes_of(t)]).astype(BF16)


def sb_attn(u_sb, sb_out_g, B, S):
    blk = min(SB_BLK, S)
    half = blk // 2
    u3 = u_sb.reshape(B, S, SB_IN)
    n_pairs = SB_WIDTH // LANES
    after = lambda n: (jnp.arange(n)[:, None] >= jnp.arange(n)[None, :]).astype(BF16)
    before = (jnp.arange(half)[None, :] < jnp.arange(half)[:, None]).astype(BF16)
    mask_t = jnp.tile(before, (2, 1))
    mask_b = jnp.tile(jnp.concatenate([jnp.ones((half, half), BF16), before], axis=1), (2, 1))
    n_tiles = SB_TILES_PER_STEP
    width = n_tiles * LANES
    n_groups = n_pairs // n_tiles
    const = lambda r, c: pl.BlockSpec((r, c), lambda b, p, i: (0, 0))
    return pl.pallas_call(
        functools.partial(_sb_attn_body, blk=blk, n_tiles=n_tiles),
        grid=(B, n_groups, S // blk),
        in_specs=[
            pl.BlockSpec((1, blk, width), lambda b, p, i: (b, i, p)),
            pl.BlockSpec((1, S, width), lambda b, p, i: (b, 0, n_groups + p)),
            pl.BlockSpec((1, S, width), lambda b, p, i: (b, 0, 2 * n_groups + p)),
            pl.BlockSpec((1, width), lambda b, p, i: (0, p)),
            const(half, half), const(blk, blk), const(blk, half), const(blk, blk),
        ],
        out_specs=pl.BlockSpec((1, blk, width), lambda b, p, i: (b, i, p)),
        out_shape=jax.ShapeDtypeStruct((B, S, SB_WIDTH), BF16),
        compiler_params=_cparams(("arbitrary", "arbitrary", "arbitrary")),
        name="sb_attn",
    )(u3, u3, u3, sb_out_g, after(half), after(blk), mask_t, mask_b)


RW_CHUNK = LANES
RW_SEQS_PER_STEP = 4
RW_GROUP = MXU_DIM
RW_GROUP_HEADS = RW_GROUP // HEAD_DIM


def _softplus(y):
    return jnp.maximum(y, 0.0) + jnp.log(1.0 + jnp.exp(-jnp.abs(y)))


def _sigmoid(y):
    return 1.0 / (1.0 + jnp.exp(-y))


def _split_bf16(x):
    hi = x.astype(BF16)
    lo = (x - hi.astype(F32)).astype(BF16)
    return hi, lo


def _rwkv_body(u_ref, mu_ref, w0_ref, w2_ref, a0_ref, a2_ref, g2_ref, kk_ref, ka_ref, rk_ref,
               lnw_ref, lnb_ref, o_ref, prev_ref, state_ref, *, C, n_seqs):
    c = pl.program_id(1)
    G, GH = RW_GROUP, RW_GROUP_HEADS
    n_groups = RW_WIDTH // G

    @pl.when(c == 0)
    def _():
        prev_ref[...] = jnp.zeros_like(prev_ref)
        state_ref[...] = jnp.zeros_like(state_ref)

    head_bd = (lax.broadcasted_iota(jnp.int32, (G, G), 0) // HEAD_DIM
               == lax.broadcasted_iota(jnp.int32, (G, G), 1) // HEAD_DIM)
    ones_bd = head_bd.astype(BF16)
    stack_mask = (lax.broadcasted_iota(jnp.int32, (GH * C, G), 0) // C
                  == lax.broadcasted_iota(jnp.int32, (GH * C, G), 1) // HEAD_DIM)
    low_half = lax.broadcasted_iota(jnp.int32, (1, LANES), 1) < HEAD_DIM
    tt = lax.broadcasted_iota(jnp.int32, (C, GH * C), 0)
    ss = lax.broadcasted_iota(jnp.int32, (C, GH * C), 1) % C
    strict = ss < tt
    incl = ss <= tt
    tri_incl = (lax.broadcasted_iota(jnp.int32, (C, C), 1)
                <= lax.broadcasted_iota(jnp.int32, (C, C), 0)).astype(BF16)

    def head_sum(x):
        return _dot(x.astype(BF16), ones_bd)

    def stack(x):
        return jnp.where(stack_mask, jnp.concatenate([x] * GH, axis=0), 0.0).astype(BF16)

    def swap_halves(x):
        return pltpu.roll(x, HEAD_DIM, axis=1)

    seqs = range(n_seqs)
    units = [(n, gi) for n in seqs for gi in range(n_groups)]
    lanes_of = lambda gi: slice(gi * G, (gi + 1) * G)

    ums = []
    for n in seqs:
        u = u_ref[n].astype(F32)
        row_id = lax.broadcasted_iota(jnp.int32, (C, 1), 0)
        shifted = jnp.where(row_id == 0, prev_ref[n], pltpu.roll(u, 1, axis=0))
        prev_ref[n] = u[C - 1:C, :]
        ums.append(u + (shifted - u) * mu_ref[...])
    r = [um[:, 0:RW_WIDTH] for um in ums]
    k = [um[:, RW_WIDTH:2 * RW_WIDTH] for um in ums]
    v = [um[:, 2 * RW_WIDTH:3 * RW_WIDTH] for um in ums]
    xwa = [um[:, 3 * RW_WIDTH:3 * RW_WIDTH + LORA_W + LORA_A] for um in ums]
    xg = [um[:, 3 * RW_WIDTH + LORA_W + LORA_A:] for um in ums]

    lora_w = [_dot(jnp.tanh(x).astype(BF16), w2_ref[...]) for x in xwa]
    lora_a = [_dot(x.astype(BF16), a2_ref[...]) for x in xwa]
    gate = [_dot(_sigmoid(x).astype(BF16), g2_ref[...]) for x in xg]
    logdec = [-jnp.exp(-_softplus(-(w0_ref[...] + lw)) - 0.5) for lw in lora_w]
    lr = [_sigmoid(a0_ref[...] + la) for la in lora_a]

    splits = [_split_bf16(ld) for ld in logdec]
    cum = [_dot(tri_incl, hi) + _dot(tri_incl, lo) for hi, lo in splits]
    p_incl = [jnp.exp(cm) for cm in cum]
    p_prev = [jnp.exp(cm - ld) for cm, ld in zip(cum, logdec)]
    p_inv = [jnp.exp(-cm) for cm in cum]
    p_last = [p[C - 1:C, :] for p in p_incl]

    kk = [kn * kk_ref[...] for kn in k]
    k_adj = [kn * (1.0 + (lrn - 1.0) * ka_ref[...]) for kn, lrn in zip(k, lr)]
    rk_prod = [rn * kan * rk_ref[...] for rn, kan in zip(r, k_adj)]

    kk_ssq = [head_sum(kk[n][:, lanes_of(gi)] * kk[n][:, lanes_of(gi)]) for n, gi in units]
    kkn = [kk[n][:, lanes_of(gi)] * lax.rsqrt(jnp.maximum(s, 1e-24)) for (n, gi), s in zip(units, kk_ssq)]
    v_g = [v[n][:, lanes_of(gi)] for n, gi in units]
    at = [-kn * p_prev[n][:, lanes_of(gi)] for (n, gi), kn in zip(units, kkn)]
    bt = [kn * lr[n][:, lanes_of(gi)] * p_inv[n][:, lanes_of(gi)] for (n, gi), kn in zip(units, kkn)]
    kt = [k_adj[n][:, lanes_of(gi)] * p_inv[n][:, lanes_of(gi)] for n, gi in units]
    rt = [r[n][:, lanes_of(gi)] * p_incl[n][:, lanes_of(gi)] for n, gi in units]

    lhs2 = [jnp.concatenate([a, q], axis=0).astype(BF16) for a, q in zip(at, rt)]
    ab = [_dot_nt(l2, stack(b)) for l2, b in zip(lhs2, bt)]
    ak = [_dot_nt(l2, stack(kx)) for l2, kx in zip(lhs2, kt)]
    a_ab = [jnp.where(strict, x[:C], 0.0) for x in ab]
    q_ab = [jnp.where(incl, x[C:], 0.0) for x in ab]
    a_ak = [jnp.where(strict, x[:C], 0.0) for x in ak]
    q_ak = [jnp.where(incl, x[C:], 0.0) for x in ak]
    vs = [stack(x) for x in v_g]
    akv = [_dot(a.astype(BF16), s) for a, s in zip(a_ak, vs)]

    n_steps = max(1, (C - 1).bit_length())
    tiles_per_group = G // LANES
    ys, lps = [], []
    for ui in range(len(units)):
        for p in range(tiles_per_group):
            at_p = at[ui][:, p * LANES:(p + 1) * LANES]
            akv_p = akv[ui][:, p * LANES:(p + 1) * LANES]
            ys.append(jnp.where(low_half, at_p, swap_halves(akv_p)))
            ys.append(jnp.where(low_half, swap_halves(at_p), akv_p))
            for e in range(2):
                h = 2 * p + e
                lps.append(a_ab[ui][:, h * C:(h + 1) * C])
    for step in range(n_steps):
        if step == n_steps - 1:
            ys = [y + _dot(lp.astype(BF16), y.astype(BF16)) for y, lp in zip(ys, lps)]
        else:
            prods = [_dot(lp.astype(BF16), jnp.concatenate([y, lp], axis=1).astype(BF16))
                     for y, lp in zip(ys, lps)]
            ys = [y + pr[:, 0:LANES] for y, pr in zip(ys, prods)]
            lps = [pr[:, LANES:] for pr in prods]
    y1, y2 = [], []
    for ui in range(len(units)):
        w_tiles, u0_tiles = [], []
        for p in range(tiles_per_group):
            y_even, y_odd = ys[ui * GH + 2 * p], ys[ui * GH + 2 * p + 1]
            w_tiles.append(jnp.where(low_half, y_even, swap_halves(y_odd)))
            u0_tiles.append(jnp.where(low_half, swap_halves(y_even), y_odd))
        y1.append(jnp.concatenate(w_tiles, axis=1))
        y2.append(jnp.concatenate(u0_tiles, axis=1))

    s0 = [state_ref[n * n_groups + gi] for n, gi in units]
    s0b = [s.astype(BF16) for s in s0]
    uu = [_dot_nt(a.astype(BF16), s) + b for a, s, b in zip(y1, s0b, y2)]
    o_state = [_dot_nt(q.astype(BF16), s) for q, s in zip(rt, s0b)]
    o_u = [_dot(q.astype(BF16), stack(x)) for q, x in zip(q_ab, uu)]
    o_v = [_dot(q.astype(BF16), s) for q, s in zip(q_ak, vs)]
    upd = [_dot_tn(jnp.concatenate([x, vv], axis=0).astype(BF16),
                   jnp.concatenate([b * p_last[n][:, lanes_of(gi)], kx * p_last[n][:, lanes_of(gi)]],
                                   axis=0).astype(BF16))
           for (n, gi), x, vv, b, kx in zip(units, uu, v_g, bt, kt)]
    for (n, gi), s, up in zip(units, s0, upd):
        state_ref[n * n_groups + gi] = s * p_last[n][:, lanes_of(gi)] + jnp.where(head_bd, up, 0.0)
    o_g = [a + b + c_ for a, b, c_ in zip(o_state, o_u, o_v)]

    mean = [head_sum(x) * (1.0 / HEAD_DIM) for x in o_g]
    dev = [x - m for x, m in zip(o_g, mean)]
    var = [head_sum(d * d) * (1.0 / HEAD_DIM) for d in dev]
    rk_sum = [head_sum(rk_prod[n][:, lanes_of(gi)]) for n, gi in units]
    outs = []
    for (n, gi), d, vr, rs, vv in zip(units, dev, var, rk_sum, v_g):
        sl = lanes_of(gi)
        gn = d * lax.rsqrt(vr + GN_EPS) * lnw_ref[:, sl] + lnb_ref[:, sl]
        outs.append((gn + rs * vv) * gate[n][:, sl])
    for n in seqs:
        o_ref[n] = jnp.concatenate(outs[n * n_groups:(n + 1) * n_groups], axis=1).astype(BF16)


def rwkv(u_rw, shift_mu, w0, w2_pad, a0, a2_pad, g2, k_k, k_a, r_k, ln_w, ln_b, B, S):
    C = RW_CHUNK
    n_seqs = RW_SEQS_PER_STEP if B % RW_SEQS_PER_STEP == 0 else 1
    assert S % C == 0
    u3 = u_rw.reshape(B, S, RW_IN)
    vec = lambda n: pl.BlockSpec((1, n), lambda b, c: (0, 0))
    mat = lambda m, n: pl.BlockSpec((m, n), lambda b, c: (0, 0))
    return pl.pallas_call(
        functools.partial(_rwkv_body, C=C, n_seqs=n_seqs),
        grid=(B // n_seqs, S // C),
        in_specs=[
            pl.BlockSpec((n_seqs, C, RW_IN), lambda b, c: (b, c, 0)),
            vec(RW_IN), vec(RW_WIDTH), mat(LORA_W + LORA_A, RW_WIDTH), vec(RW_WIDTH),
            mat(LORA_W + LORA_A, RW_WIDTH), mat(LORA_G, RW_WIDTH),
            vec(RW_WIDTH), vec(RW_WIDTH), vec(RW_WIDTH), vec(RW_WIDTH), vec(RW_WIDTH),
        ],
        out_specs=pl.BlockSpec((n_seqs, C, RW_WIDTH), lambda b, c: (b, c, 0)),
        out_shape=jax.ShapeDtypeStruct((B, S, RW_WIDTH), BF16),
        scratch_shapes=[
            pltpu.VMEM((n_seqs, 1, RW_IN), F32),
            pltpu.VMEM((n_seqs * (RW_WIDTH // RW_GROUP), RW_GROUP, RW_GROUP), F32),
        ],
        compiler_params=_cparams(("arbitrary", "arbitrary")),
        name="rwkv",
    )(u3, shift_mu, w0, w2_pad, a0, a2_pad, g2, k_k, k_a, r_k, ln_w, ln_b)


OR_TM = 1024
OR_PARTS = 2
N_PAIRS = EXPERTS_PER_GROUP * (EXPERTS_PER_GROUP - 1) // 2
N_CLASSES = N_GROUPS * N_PAIRS
CLS_ROWS = LANES
ROUTER_ROWS = 48
SUBLANES = 8
H_CHUNKS = D_MODEL // LANES
SLAB_IN = H_CHUNKS + 1
SLAB_OUT = H_CHUNKS
OUT_PITCH = SLAB_OUT + 1


def _first_index_of(vals, target, row_f):
    return jnp.min(jnp.where(vals == target, row_f, 1e9), axis=0, keepdims=True)


def _out_router_body(x_ref, osb_ref, orw_ref, wo_ref, g_ref, wr_ref, wrhi_ref, br_ref,
                     slab_ref, meta_ref, hist_ref, carry_ref, *, tm):
    i = pl.program_id(0)

    @pl.when(i == 0)
    def _():
        carry_ref[...] = jnp.zeros_like(carry_ref)

    parts = range(OR_PARTS)
    tp = tm // OR_PARTS
    rows_of = lambda p: slice(p * tp, (p + 1) * tp)
    hs = [x_ref[rows_of(p), :] + _dot(osb_ref[rows_of(p), :], wo_ref[0:SB_WIDTH, :])
          + _dot(orw_ref[rows_of(p), :], wo_ref[SB_WIDTH:, :]) for p in parts]
    carry = carry_ref[...]
    for p in parts:
        carry = _route_part(p, tp, hs[p], carry, g_ref, wr_ref, wrhi_ref, br_ref, slab_ref, meta_ref)
    carry_ref[...] = carry
    hist_ref[...] = carry


def _route_part(p, tm, h, carry, g_ref, wr_ref, wrhi_ref, br_ref, slab_ref, meta_ref):
    ms = jnp.mean(h * h, axis=-1, keepdims=True)
    xn = h * lax.rsqrt(ms + RMS_EPS) * g_ref[...]

    xn_hi, xn_lo = _split_bf16(xn)
    both = _dot_nt(wr_ref[...], xn_hi)
    logits = both[:ROUTER_ROWS] + both[ROUTER_ROWS:] + _dot_nt(wrhi_ref[...], xn_lo) + br_ref[...][:, 0:1]

    row_f = lax.broadcasted_iota(jnp.int32, (SUBLANES, tm), 0).astype(F32)
    lg = jnp.where(row_f < N_GROUPS, logits[N_EXPERTS:N_EXPERTS + SUBLANES], -jnp.inf)
    eg = jnp.exp(lg - jnp.max(lg, axis=0, keepdims=True))
    pg = eg / jnp.sum(eg, axis=0, keepdims=True)
    g_val = jnp.max(pg, axis=0, keepdims=True)
    g_idx = _first_index_of(pg, g_val, row_f)
    sel = jnp.zeros((SUBLANES, tm), F32)
    for g in range(N_GROUPS):
        sel = jnp.where(g_idx == g, logits[g * EXPERTS_PER_GROUP:(g + 1) * EXPERTS_PER_GROUP], sel)
    ee = jnp.exp(sel - jnp.max(sel, axis=0, keepdims=True))
    pe = ee / jnp.sum(ee, axis=0, keepdims=True)
    e1 = jnp.max(pe, axis=0, keepdims=True)
    i1 = _first_index_of(pe, e1, row_f)
    pe2 = jnp.where(row_f == i1, -1.0, pe)
    e2 = jnp.max(pe2, axis=0, keepdims=True)
    i2 = _first_index_of(pe2, e2, row_f)
    den = e1 + e2
    wt1 = g_val * e1 / den
    wt2 = g_val * e2 / den
    first_lo = i1 < i2
    lo = jnp.where(first_lo, i1, i2)
    hi = jnp.where(first_lo, i2, i1)
    w_lo = jnp.where(first_lo, wt1, wt2)
    w_hi = jnp.where(first_lo, wt2, wt1)
    pair = lo * (2 * EXPERTS_PER_GROUP - 1 - lo) * 0.5 + (hi - lo - 1.0)
    cls = g_idx * N_PAIRS + pair

    cls_row = lax.broadcasted_iota(jnp.int32, (CLS_ROWS, tm), 0).astype(F32)
    onehot = cls_row == cls
    onehot_b = jnp.where(onehot, 1.0, 0.0).astype(BF16)
    upto = (lax.broadcasted_iota(jnp.int32, (tm, tm), 0)
            <= lax.broadcasted_iota(jnp.int32, (tm, tm), 1)).astype(BF16)
    cum = _dot(onehot_b, upto)
    tot = _dot(onehot_b, jnp.ones((tm, LANES), BF16))
    before = jnp.concatenate([carry] * (tm // LANES), axis=1)
    rank = jnp.sum(jnp.where(onehot, cum - 1.0 + before, 0.0), axis=0, keepdims=True)

    meta = jnp.concatenate([cls, rank, jnp.zeros((SUBLANES - 2, tm), F32)], axis=0)
    meta_ref[0, :, p * tm:(p + 1) * tm] = meta.astype(jnp.int32)

    w_rows = jnp.concatenate([w_lo, w_hi, jnp.zeros((LANES - 2, tm), F32)], axis=0)
    slab_rows = lambda j: pl.ds(p * tm * SLAB_IN + j, tm, stride=SLAB_IN)
    for j in range(H_CHUNKS):
        slab_ref[slab_rows(j), :] = h[:, j * LANES:(j + 1) * LANES]
    slab_ref[slab_rows(H_CHUNKS), :] = jnp.transpose(w_rows)
    return carry + tot


def out_router(x2d, o_sb, o_rw, w_out, g_ffn, wr_both, wr_hi, b_r):
    T = x2d.shape[0]
    tm = min(OR_TM, T)
    nt = T // tm
    const = lambda *shape: pl.BlockSpec(shape, lambda i: (0,) * len(shape))
    return pl.pallas_call(
        functools.partial(_out_router_body, tm=tm),
        grid=(nt,),
        in_specs=[
            pl.BlockSpec((tm, D_MODEL), lambda i: (i, 0)),
            pl.BlockSpec((tm, SB_WIDTH), lambda i: (i, 0)),
            pl.BlockSpec((tm, RW_WIDTH), lambda i: (i, 0)),
            const(D_MODEL, D_MODEL), const(1, D_MODEL),
            const(2 * ROUTER_ROWS, D_MODEL), const(ROUTER_ROWS, D_MODEL), const(ROUTER_ROWS, LANES),
        ],
        out_specs=[
            pl.BlockSpec((tm * SLAB_IN, LANES), lambda i: (i, 0)),
            pl.BlockSpec((1, SUBLANES, tm), lambda i: (i, 0, 0)),
            const(CLS_ROWS, LANES),
        ],
        out_shape=[
            jax.ShapeDtypeStruct((T * SLAB_IN, LANES), F32),
            jax.ShapeDtypeStruct((nt, SUBLANES, tm), jnp.int32),
            jax.ShapeDtypeStruct((CLS_ROWS, LANES), F32),
        ],
        scratch_shapes=[pltpu.VMEM((CLS_ROWS, LANES), F32)],
        compiler_params=_cparams(("arbitrary",)),
        name="out_router",
    )(x2d, o_sb, o_rw, w_out, g_ffn, wr_both, wr_hi, b_r)


PERM_TOKENS_PER_STEP = 1024
PERM_WINDOW = 256
PERM_UNROLL = 8
DMA_PRIORITIES = 2


def _slab_scatter_body(pos_ref, src_ref, dst_ref, sem, *, tokens, slab):
    window = min(PERM_WINDOW, tokens)

    def slab_copy(j, dst_tok):
        return pltpu.make_async_copy(src_ref.at[pl.ds(j * slab, slab)],
                                     dst_ref.at[pl.ds(dst_tok * slab, slab)], sem)

    def start_batch(b, carry):
        for j in range(PERM_UNROLL):
            tok = b * PERM_UNROLL + j
            slab_copy(tok, pos_ref[0, 0, tok]).start(priority=j % DMA_PRIORITIES)
        return carry

    def retire(j, carry):
        slab_copy(0, 0).wait()
        return carry

    def start_batch_retire_batch(b, carry):
        start_batch(b, carry)
        lax.fori_loop(0, PERM_UNROLL, retire, 0, unroll=True)
        return carry

    lax.fori_loop(0, window // PERM_UNROLL, start_batch, 0)
    lax.fori_loop(window // PERM_UNROLL, tokens // PERM_UNROLL, start_batch_retire_batch, 0)
    lax.fori_loop(0, window, retire, 0, unroll=PERM_UNROLL)


def slab_scatter(src, pos, slab):
    n_tok = pos.shape[0]
    tokens = min(PERM_TOKENS_PER_STEP, n_tok)
    steps = n_tok // tokens
    return pl.pallas_call(
        functools.partial(_slab_scatter_body, tokens=tokens, slab=slab),
        grid=(steps,),
        in_specs=[
            pl.BlockSpec((1, 1, tokens), lambda s: (s, 0, 0), memory_space=pltpu.SMEM),
            pl.BlockSpec((tokens * slab, LANES), lambda s: (s, 0)),
        ],
        out_specs=pl.BlockSpec(memory_space=pl.ANY),
        out_shape=jax.ShapeDtypeStruct(src.shape, src.dtype),
        scratch_shapes=[pltpu.SemaphoreType.DMA(())],
        compiler_params=_cparams(("arbitrary",)),
        name="slab_scatter",
    )(pos.reshape(steps, 1, tokens), src)


GATHER_TM = 512


def _slab_gather_body(pos_ref, nxt_ref, src_ref, g_ref, o_ref, buf, sem, *, tm):
    i = pl.program_id(0)
    n_steps = pl.num_programs(0)
    slot = i % 2

    def slab_copy(tok, j, to_slot):
        return pltpu.make_async_copy(src_ref.at[pl.ds(tok * OUT_PITCH, SLAB_OUT)],
                                     buf.at[to_slot, pl.ds(j * SLAB_OUT, SLAB_OUT)], sem.at[to_slot])

    def fetch(idx_ref, to_slot):
        def issue(j, carry):
            slab_copy(idx_ref[0, 0, j], j, to_slot).start()
            return carry
        lax.fori_loop(0, tm, issue, 0, unroll=PERM_UNROLL)

    @pl.when(i == 0)
    def _():
        fetch(pos_ref, 0)

    @pl.when(i + 1 < n_steps)
    def _():
        fetch(nxt_ref, 1 - slot)

    def drain(j, carry):
        slab_copy(0, 0, slot).wait()
        return carry

    lax.fori_loop(0, tm, drain, 0, unroll=PERM_UNROLL)
    chunks = [buf[slot, pl.ds(j, tm, stride=SLAB_OUT), :] for j in range(H_CHUNKS)]
    squares = functools.reduce(lambda a, b: a + b, [x * x for x in chunks])
    scale = lax.rsqrt(jnp.sum(squares, axis=-1, keepdims=True) * (1.0 / D_MODEL) + RMS_EPS)
    for j in range(H_CHUNKS):
        o_ref[:, j * LANES:(j + 1) * LANES] = chunks[j] * scale * g_ref[:, j * LANES:(j + 1) * LANES]


def slab_gather(src, pos, g_final):
    n_tok = pos.shape[0]
    tm = min(GATHER_TM, n_tok)
    steps = n_tok // tm
    pos3 = pos.reshape(steps, 1, tm)
    return pl.pallas_call(
        functools.partial(_slab_gather_body, tm=tm),
        grid=(steps,),
        in_specs=[
            pl.BlockSpec((1, 1, tm), lambda i: (i, 0, 0), memory_space=pltpu.SMEM),
            pl.BlockSpec((1, 1, tm), lambda i: (jnp.minimum(i + 1, steps - 1), 0, 0), memory_space=pltpu.SMEM),
            pl.BlockSpec(memory_space=pl.ANY),
            pl.BlockSpec((1, D_MODEL), lambda i: (0, 0)),
        ],
        out_specs=pl.BlockSpec((tm, D_MODEL), lambda i: (i, 0)),
        out_shape=jax.ShapeDtypeStruct((n_tok, D_MODEL), src.dtype),
        scratch_shapes=[
            pltpu.VMEM((2, tm * SLAB_OUT, LANES), src.dtype),
            pltpu.SemaphoreType.DMA((2,)),
        ],
        compiler_params=_cparams(("arbitrary",)),
        name="slab_gather",
    )(pos3, pos3, src, g_final)


EX_TM = 256


def _experts_body(tile_ref, lo_ref, hi_ref, start_ref, end_ref,
                  hs_ref, gffn_ref, wg_lo, wu_lo, wd_lo, wg_hi, wu_hi, wd_hi, o_ref, *, tm):
    i = pl.program_id(0)
    tile = tile_ref[i]
    start = start_ref[i]
    end = end_ref[i]

    whole = jnp.logical_and(start <= tile * tm, end >= (tile + 1) * tm)
    first_of_tile = jnp.logical_or(i == 0, tile != tile_ref[jnp.maximum(i - 1, 0)])

    @pl.when(jnp.logical_and(first_of_tile, jnp.logical_not(whole)))
    def _():
        o_ref[...] = jnp.zeros_like(o_ref)

    @pl.when(end > start)
    def _():
        in_rows = lambda j: pl.ds(j, tm, stride=SLAB_IN)
        out_rows = lambda j: pl.ds(j, tm, stride=OUT_PITCH)
        h = jnp.concatenate([hs_ref[in_rows(j), :] for j in range(H_CHUNKS)], axis=1)
        w_row = hs_ref[in_rows(H_CHUNKS), :]
        w_lo = w_row[:, 0:1]
        w_hi = w_row[:, 1:2]
        ms = jnp.mean(h * h, axis=-1, keepdims=True)
        xn = (h * lax.rsqrt(ms + RMS_EPS) * gffn_ref[...]).astype(BF16)

        g_lo = _dot(xn, wg_lo[0])
        g_hi = _dot(xn, wg_hi[0])
        u_lo = _dot(xn, wu_lo[0])
        u_hi = _dot(xn, wu_hi[0])
        a_lo = (g_lo * _sigmoid(g_lo) * u_lo * w_lo).astype(BF16)
        a_hi = (g_hi * _sigmoid(g_hi) * u_hi * w_hi).astype(BF16)
        res = h + _dot(a_lo, wd_lo[0]) + _dot(a_hi, wd_hi[0])
        @pl.when(whole)
        def _():
            for j in range(H_CHUNKS):
                o_ref[out_rows(j), :] = res[:, j * LANES:(j + 1) * LANES]
            o_ref[out_rows(SLAB_OUT), :] = jnp.zeros((tm, LANES), F32)

        @pl.when(jnp.logical_not(whole))
        def _():
            rows = tile * tm + lax.broadcasted_iota(jnp.int32, (tm, 1), 0)
            mine = jnp.logical_and(rows >= start, rows < end)
            for j in range(H_CHUNKS):
                o_ref[out_rows(j), :] = jnp.where(mine, res[:, j * LANES:(j + 1) * LANES], o_ref[out_rows(j), :])


def experts(h_sorted, g_ffn, w_gate, w_up, w_down, item_tile, item_lo, item_hi, item_start, item_end):
    T = h_sorted.shape[0] // SLAB_IN
    tm = min(EX_TM, T)
    n_items = item_tile.shape[0]
    gate_spec = lambda which: pl.BlockSpec(
        (1, D_MODEL, D_EXPERT), lambda i, tl, lo, hi, st, en: ((lo, hi)[which][i], 0, 0))
    down_spec = lambda which: pl.BlockSpec(
        (1, D_EXPERT, D_MODEL), lambda i, tl, lo, hi, st, en: ((lo, hi)[which][i], 0, 0))
    vec_spec = pl.BlockSpec((1, D_MODEL), lambda i, tl, lo, hi, st, en: (0, 0))
    grid_spec = pltpu.PrefetchScalarGridSpec(
        num_scalar_prefetch=5,
        grid=(n_items,),
        in_specs=[
            pl.BlockSpec((tm * SLAB_IN, LANES), lambda i, tl, lo, hi, st, en: (tl[i], 0)),
            vec_spec,
            gate_spec(0), gate_spec(0), down_spec(0),
            gate_spec(1), gate_spec(1), down_spec(1),
        ],
        out_specs=pl.BlockSpec((tm * OUT_PITCH, LANES), lambda i, tl, lo, hi, st, en: (tl[i], 0)),
    )
    return pl.pallas_call(
        functools.partial(_experts_body, tm=tm),
        grid_spec=grid_spec,
        out_shape=jax.ShapeDtypeStruct((T * OUT_PITCH, LANES), F32),
        compiler_params=_cparams(("arbitrary",)),
        name="experts",
    )(item_tile, item_lo, item_hi, item_start, item_end,
      h_sorted, g_ffn, w_gate, w_up, w_down, w_gate, w_up, w_down)


def _pair_first(lo):
    return (lo * (2 * EXPERTS_PER_GROUP - 1 - lo)) // 2


def _work_items(hist, meta, T, tm):
    i32 = jnp.int32
    counts = hist[:N_CLASSES, 0].astype(i32)
    offs = jnp.cumsum(counts) - counts
    cls = meta[:, 0, :].reshape(T)
    rank = meta[:, 1, :].reshape(T)
    class_ids = jnp.arange(N_CLASSES, dtype=i32)
    pos = jnp.sum(jnp.where(cls[:, None] == class_ids[None, :], offs[None, :], 0), axis=1) + rank

    n_tiles = T // tm
    n_items = n_tiles + N_CLASSES
    tile_starts = jnp.arange(n_tiles, dtype=i32) * tm
    tile_slot = jnp.arange(n_tiles, dtype=i32) + jnp.sum(offs[None, :] < tile_starts[:, None], axis=1, dtype=i32)
    class_slot = class_ids + jnp.sum(tile_starts[None, :] <= offs[:, None], axis=1, dtype=i32)
    slots = jnp.arange(n_items, dtype=i32)
    starts = (jnp.sum(jnp.where(tile_slot[None, :] == slots[:, None], tile_starts[None, :], 0), axis=1)
              + jnp.sum(jnp.where(class_slot[None, :] == slots[:, None], offs[None, :], 0), axis=1))
    ends = jnp.concatenate([starts[1:], jnp.full((1,), T, i32)])
    item_tile = jnp.minimum(starts // tm, n_tiles - 1)
    item_cls = jnp.sum(offs[None, :] <= starts[:, None], axis=1, dtype=i32) - 1
    grp = item_cls // N_PAIRS
    q = item_cls % N_PAIRS
    firsts = _pair_first(jnp.arange(1, EXPERTS_PER_GROUP - 1, dtype=i32))
    lo = jnp.sum(q[:, None] >= firsts[None, :], axis=1, dtype=i32)
    hi = q - _pair_first(lo) + lo + 1
    base = grp * EXPERTS_PER_GROUP
    return pos, item_tile, base + lo, base + hi, starts, ends


def kernel(x, norm_mix_g, w_in, shift_mu, sb_out_g, rw_w0, rw_w2, rw_a0, rw_a2, rw_g2, rw_k_k, rw_k_a, rw_r_k, rw_ln_w, rw_ln_b, w_out, norm_ffn_g, router_grp_w, router_grp_b, router_exp_w, router_exp_b, exp_w_gate, exp_w_up, exp_w_down, final_norm_g):
    B, S, D = x.shape
    T = B * S
    assert D == D_MODEL and w_in.shape[0] == 1, "one layer of width 1024 is what these kernels implement"
    l = 0
    row = lambda a: a.reshape(1, -1)
    x2d = x.reshape(T, D)

    w_in_b = w_in[l].astype(BF16)
    u_sb, u_rw = in_proj(x2d, row(norm_mix_g[l]), w_in_b[:, :SB_IN], w_in_b[:, SB_IN:])
    o_sb = sb_attn(u_sb, row(sb_out_g[l]), B, S)
    w2_pad = jnp.concatenate([rw_w2[l], jnp.zeros_like(rw_a2[l])], axis=0).astype(BF16)
    a2_pad = jnp.concatenate([jnp.zeros_like(rw_w2[l]), rw_a2[l]], axis=0).astype(BF16)
    o_rw = rwkv(u_rw, row(shift_mu[l]), row(rw_w0[l]), w2_pad, row(rw_a0[l]), a2_pad, rw_g2[l].astype(BF16),
                row(rw_k_k[l]), row(rw_k_a[l]), row(rw_r_k[l]), row(rw_ln_w[l]), row(rw_ln_b[l]), B, S)

    pad_rows = ROUTER_ROWS - N_EXPERTS - N_GROUPS
    wr = jnp.concatenate([router_exp_w[l].T, router_grp_w[l].T, jnp.zeros((pad_rows, D), F32)], axis=0)
    wr_hi, wr_lo = _split_bf16(wr)
    b_r = jnp.concatenate([router_exp_b[l], router_grp_b[l], jnp.zeros((pad_rows,), F32)])
    b_r = jnp.broadcast_to(b_r[:, None], (ROUTER_ROWS, LANES))
    h_slabs, meta, hist = out_router(x2d, o_sb.reshape(T, SB_WIDTH), o_rw.reshape(T, RW_WIDTH),
                                     w_out[l].astype(BF16), row(norm_ffn_g[l]),
                                     jnp.concatenate([wr_hi, wr_lo], axis=0), wr_hi, b_r)

    tm = min(EX_TM, T)
    pos, item_tile, item_lo, item_hi, item_start, item_end = _work_items(hist, meta, T, tm)
    h_sorted = slab_scatter(h_slabs, pos, SLAB_IN)
    out_sorted = experts(h_sorted, row(norm_ffn_g[l]),
                         exp_w_gate[l].astype(BF16), exp_w_up[l].astype(BF16), exp_w_down[l].astype(BF16),
                         item_tile, item_lo, item_hi, item_start, item_end)
    out = slab_gather(out_sorted, pos, row(final_norm_g))
    return out.reshape(B, S, D)
```

```python
import functools
import math

import jax
import jax.numpy as jnp
from jax import lax
from jax.experimental import pallas as pl
from jax.experimental.pallas import tpu as pltpu

F32 = jnp.float32
BF16 = jnp.bfloat16

D_MODEL = 1024
SB_HEADS = 8
HEAD_DIM = 64
SB_WIDTH = SB_HEADS * HEAD_DIM
RW_HEADS = 8
RW_WIDTH = RW_HEADS * HEAD_DIM
LORA_W = 64
LORA_A = 64
LORA_G = 128
SB_IN = 3 * SB_WIDTH
RW_IN = 3 * RW_WIDTH + LORA_W + LORA_A + LORA_G
N_GROUPS = 4
EXPERTS_PER_GROUP = 8
N_EXPERTS = N_GROUPS * EXPERTS_PER_GROUP
D_EXPERT = 256
RMS_EPS = 1e-6
GN_EPS = 64e-5

LANES = 128
MXU_DIM = 256
VMEM_LIMIT = 48 * 1024 * 1024


def _cparams(sem):
    return pltpu.CompilerParams(dimension_semantics=sem, vmem_limit_bytes=VMEM_LIMIT)


def _dot(a, b):
    return jnp.dot(a, b, preferred_element_type=F32)


def _dot_nt(a, b):
    return lax.dot_general(a, b, (((1,), (1,)), ((), ())), preferred_element_type=F32)


def _dot_tn(a, b):
    return lax.dot_general(a, b, (((0,), (0,)), ((), ())), preferred_element_type=F32)


IN_TM = 512


def _in_proj_body(x_ref, g_ref, wsb_ref, wrw_ref, usb_ref, urw_ref):
    x = x_ref[...]
    ms = jnp.mean(x * x, axis=-1, keepdims=True)
    xn = (x * lax.rsqrt(ms + RMS_EPS) * g_ref[...]).astype(BF16)
    usb_ref[...] = _dot(xn, wsb_ref[...]).astype(BF16)
    urw_ref[...] = _dot(xn, wrw_ref[...]).astype(BF16)


def in_proj(x2d, g, w_sb, w_rw):
    T = x2d.shape[0]
    tm = min(IN_TM, T)
    return pl.pallas_call(
        _in_proj_body,
        grid=(T // tm,),
        in_specs=[
            pl.BlockSpec((tm, D_MODEL), lambda i: (i, 0)),
            pl.BlockSpec((1, D_MODEL), lambda i: (0, 0)),
            pl.BlockSpec((D_MODEL, SB_IN), lambda i: (0, 0)),
            pl.BlockSpec((D_MODEL, RW_IN), lambda i: (0, 0)),
        ],
        out_specs=[
            pl.BlockSpec((tm, SB_IN), lambda i: (i, 0)),
            pl.BlockSpec((tm, RW_IN), lambda i: (i, 0)),
        ],
        out_shape=[
            jax.ShapeDtypeStruct((T, SB_IN), BF16),
            jax.ShapeDtypeStruct((T, RW_IN), BF16),
        ],
        compiler_params=_cparams(("arbitrary",)),
        name="in_proj",
    )(x2d, g, w_sb, w_rw)


SB_BLK = 256
SB_TILES_PER_STEP = 4
SB_SKIP = 40.0


def _decay(z):
    one = jnp.asarray(1.0, z.dtype)
    zero = jnp.asarray(0.0, z.dtype)
    return jnp.maximum(z, zero) + jnp.log(one + jnp.exp(-jnp.abs(z)))


def _sb_attn_body(q_ref, k_ref, v_ref, g_ref, later_h_ref, later_f_ref, mask_t_ref, mask_b_ref, o_ref,
                  *, blk, n_tiles):
    qb = pl.program_id(2)
    half = blk // 2
    tiles = range(n_tiles)
    lane = lax.broadcasted_iota(jnp.int32, (1, LANES), 1)
    head_masks = [lane < HEAD_DIM, lane >= HEAD_DIM]
    lanes_of = lambda t: slice(t * LANES, (t + 1) * LANES)

    def stack_heads(x):
        return jnp.concatenate([jnp.where(m, x, 0.0) for m in head_masks], axis=0).astype(BF16)

    def unstack(pv):
        n = pv.shape[0] // 2
        return jnp.where(head_masks[0], pv[:n], pv[n:])

    def row_sum(d):
        return jnp.sum(d.astype(F32), axis=-1, keepdims=True)

    q = [q_ref[0, :, lanes_of(t)].astype(F32) * (1.0 / math.sqrt(HEAD_DIM)) for t in tiles]
    q_top = [stack_heads(x[:half]) for x in q]
    q_bot = [stack_heads(x[half:]) for x in q]

    later_h = later_h_ref[...]
    later_f = later_f_ref[...]
    mask_t = mask_t_ref[...]
    mask_b = mask_b_ref[...]

    def kv_rows(index, size):
        start = pl.multiple_of(index * size, size)
        return ([k_ref[0, pl.ds(start, size), lanes_of(t)] for t in tiles],
                [v_ref[0, pl.ds(start, size), lanes_of(t)] for t in tiles])

    has_prev = qb >= 1
    kd, vd = kv_rows(qb, blk)
    kp, vp = kv_rows(jnp.maximum(2 * qb - 1, 0), half)
    z_td = [_dot_nt(x, k[:half]) for x, k in zip(q_top, kd)]
    z_bd = [_dot_nt(x, k) for x, k in zip(q_bot, kd)]
    z_tp = [_dot_nt(x, k) for x, k in zip(q_top, kp)]
    d_td = [_decay(z.astype(BF16)) * mask_t for z in z_td]
    d_bd = [_decay(z.astype(BF16)) * mask_b for z in z_bd]
    d_tp = [_decay(z.astype(BF16)) for z in z_tp]
    s_td = [_dot(d, later_h) for d in d_td]
    s_bd = [_dot(d, later_f) for d in d_bd]
    s_tp = [_dot(d, later_h) for d in d_tp]
    c_td = [row_sum(d) for d in d_td]
    w_td = [jnp.exp(jnp.minimum(z - s, 0.0)).astype(BF16) * mask_t for z, s in zip(z_td, s_td)]
    w_bd = [jnp.exp(jnp.minimum(z - s, 0.0)).astype(BF16) * mask_b for z, s in zip(z_bd, s_bd)]
    w_tp = [jnp.where(has_prev, jnp.exp(z - s - c), 0.0).astype(BF16) for z, s, c in zip(z_tp, s_tp, c_td)]
    acc_top = [unstack(_dot(a, v[:half]) + _dot(b, vv)) for a, v, b, vv in zip(w_td, vd, w_tp, vp)]
    acc_bot = [unstack(_dot(a, v)) for a, v in zip(w_bd, vd)]
    carry_top = [c + row_sum(d) for c, d in zip(c_td, d_tp)]
    carry_bot = [row_sum(d) for d in d_bd]

    def earlier_chunks(qs, first, carry, acc):
        def alive_of(carry):
            return functools.reduce(jnp.minimum, [jnp.min(c) for c in carry]) < SB_SKIP

        def cond(st):
            return jnp.logical_and(st[0] >= 0, st[1])

        def body(st):
            c, _, carry, acc = st
            kc, vc = kv_rows(c, half)
            z = [_dot_nt(x, k) for x, k in zip(qs, kc)]
            d = [_decay(x.astype(BF16)) for x in z]
            s = [_dot(x, later_h) for x in d]
            w = [jnp.exp(zz - ss - cr).astype(BF16) for zz, ss, cr in zip(z, s, carry)]
            acc = [a + unstack(_dot(x, vv)) for a, x, vv in zip(acc, w, vc)]
            carry = [cr + row_sum(dd) for cr, dd in zip(carry, d)]
            return c - 1, alive_of(carry), carry, acc

        return lax.while_loop(cond, body, (first, alive_of(carry), carry, acc))[3]

    acc_top = earlier_chunks(q_top, 2 * qb - 2, carry_top, acc_top)
    acc_bot = earlier_chunks(q_bot, 2 * qb - 1, carry_bot, acc_bot)

    for t in tiles:
        for rows, acc in ((slice(0, half), acc_top[t]), (slice(half, blk), acc_bot[t])):
            sq = acc * acc
            s_lo = jnp.sum(jnp.where(head_masks[0], sq, 0.0), axis=-1, keepdims=True)
            s_all = jnp.sum(sq, axis=-1, keepdims=True)
            ms = jnp.where(head_masks[0], s_lo, s_all - s_lo) * (1.0 / HEAD_DIM)
            o_ref[0, rows, lanes_of(t)] = (acc * lax.rsqrt(ms + RMS_EPS) * g_ref[:, lanes_of(t)]).astype(BF16)


def sb_attn(u_sb, sb_out_g, B, S):
    blk = min(SB_BLK, S)
    half = blk // 2
    u3 = u_sb.reshape(B, S, SB_IN)
    n_pairs = SB_WIDTH // LANES
    after = lambda n: (jnp.arange(n)[:, None] >= jnp.arange(n)[None, :]).astype(BF16)
    before = (jnp.arange(half)[None, :] < jnp.arange(half)[:, None]).astype(BF16)
    mask_t = jnp.tile(before, (2, 1))
    mask_b = jnp.tile(jnp.concatenate([jnp.ones((half, half), BF16), before], axis=1), (2, 1))
    n_tiles = SB_TILES_PER_STEP
    width = n_tiles * LANES
    n_groups = n_pairs // n_tiles
    const = lambda r, c: pl.BlockSpec((r, c), lambda b, p, i: (0, 0))
    return pl.pallas_call(
        functools.partial(_sb_attn_body, blk=blk, n_tiles=n_tiles),
        grid=(B, n_groups, S // blk),
        in_specs=[
            pl.BlockSpec((1, blk, width), lambda b, p, i: (b, i, p)),
            pl.BlockSpec((1, S, width), lambda b, p, i: (b, 0, n_groups + p)),
            pl.BlockSpec((1, S, width), lambda b, p, i: (b, 0, 2 * n_groups + p)),
            pl.BlockSpec((1, width), lambda b, p, i: (0, p)),
            const(half, half), const(blk, blk), const(blk, half), const(blk, blk),
        ],
        out_specs=pl.BlockSpec((1, blk, width), lambda b, p, i: (b, i, p)),
        out_shape=jax.ShapeDtypeStruct((B, S, SB_WIDTH), BF16),
        compiler_params=_cparams(("arbitrary", "arbitrary", "arbitrary")),
        name="sb_attn",
    )(u3, u3, u3, sb_out_g, after(half), after(blk), mask_t, mask_b)


RW_CHUNK = LANES
RW_SEQS_PER_STEP = 4
RW_TRIM_ROWS = 16
RW_GROUP = MXU_DIM
RW_GROUP_HEADS = RW_GROUP // HEAD_DIM


def _softplus(y):
    return jnp.maximum(y, 0.0) + jnp.log(1.0 + jnp.exp(-jnp.abs(y)))


def _sigmoid(y):
    return 1.0 / (1.0 + jnp.exp(-y))


def _split_bf16(x):
    hi = x.astype(BF16)
    lo = (x - hi.astype(F32)).astype(BF16)
    return hi, lo


def _rwkv_body(u_ref, mu_ref, w0_ref, w2_ref, a0_ref, a2_ref, g2_ref, kk_ref, ka_ref, rk_ref,
               lnw_ref, lnb_ref, o_ref, prev_ref, state_ref, *, C, n_seqs):
    c = pl.program_id(1)
    G, GH = RW_GROUP, RW_GROUP_HEADS
    n_groups = RW_WIDTH // G

    @pl.when(c == 0)
    def _():
        prev_ref[...] = jnp.zeros_like(prev_ref)
        state_ref[...] = jnp.zeros_like(state_ref)

    head_bd = (lax.broadcasted_iota(jnp.int32, (G, G), 0) // HEAD_DIM
               == lax.broadcasted_iota(jnp.int32, (G, G), 1) // HEAD_DIM)
    ones_bd = head_bd.astype(BF16)
    stack_mask = (lax.broadcasted_iota(jnp.int32, (GH * C, G), 0) // C
                  == lax.broadcasted_iota(jnp.int32, (GH * C, G), 1) // HEAD_DIM)
    low_half = lax.broadcasted_iota(jnp.int32, (1, LANES), 1) < HEAD_DIM
    tt = lax.broadcasted_iota(jnp.int32, (C, GH * C), 0)
    ss = lax.broadcasted_iota(jnp.int32, (C, GH * C), 1) % C
    strict = ss < tt
    incl = ss <= tt
    tri_incl = (lax.broadcasted_iota(jnp.int32, (C, C), 1)
                <= lax.broadcasted_iota(jnp.int32, (C, C), 0)).astype(BF16)

    def head_sum(x):
        return _dot(x.astype(BF16), ones_bd)

    def stack(x):
        return jnp.where(stack_mask, jnp.concatenate([x] * GH, axis=0), 0.0).astype(BF16)

    def swap_halves(x):
        return pltpu.roll(x, HEAD_DIM, axis=1)

    seqs = range(n_seqs)
    units = [(n, gi) for n in seqs for gi in range(n_groups)]
    lanes_of = lambda gi: slice(gi * G, (gi + 1) * G)

    ums = []
    for n in seqs:
        u = u_ref[n].astype(F32)
        row_id = lax.broadcasted_iota(jnp.int32, (C, 1), 0)
        shifted = jnp.where(row_id == 0, prev_ref[n], pltpu.roll(u, 1, axis=0))
        prev_ref[n] = u[C - 1:C, :]
        ums.append(u + (shifted - u) * mu_ref[...])
    r = [um[:, 0:RW_WIDTH] for um in ums]
    k = [um[:, RW_WIDTH:2 * RW_WIDTH] for um in ums]
    v = [um[:, 2 * RW_WIDTH:3 * RW_WIDTH] for um in ums]
    xwa = [um[:, 3 * RW_WIDTH:3 * RW_WIDTH + LORA_W + LORA_A] for um in ums]
    xg = [um[:, 3 * RW_WIDTH + LORA_W + LORA_A:] for um in ums]

    lora_w = [_dot(jnp.tanh(x).astype(BF16), w2_ref[...]) for x in xwa]
    lora_a = [_dot(x.astype(BF16), a2_ref[...]) for x in xwa]
    gate = [_dot(_sigmoid(x).astype(BF16), g2_ref[...]) for x in xg]
    logdec = [-jnp.exp(-_softplus(-(w0_ref[...] + lw)) - 0.5) for lw in lora_w]
    lr = [_sigmoid(a0_ref[...] + la) for la in lora_a]

    splits = [_split_bf16(ld) for ld in logdec]
    cum = [_dot(tri_incl, hi) + _dot(tri_incl, lo) for hi, lo in splits]
    p_incl = [jnp.exp(cm) for cm in cum]
    p_prev = [jnp.exp(cm - ld) for cm, ld in zip(cum, logdec)]
    p_inv = [jnp.exp(-cm) for cm in cum]
    p_last = [p[C - 1:C, :] for p in p_incl]

    kk = [kn * kk_ref[...] for kn in k]
    k_adj = [kn * (1.0 + (lrn - 1.0) * ka_ref[...]) for kn, lrn in zip(k, lr)]
    rk_prod = [rn * kan * rk_ref[...] for rn, kan in zip(r, k_adj)]

    kk_ssq = [head_sum(kk[n][:, lanes_of(gi)] * kk[n][:, lanes_of(gi)]) for n, gi in units]
    kkn = [kk[n][:, lanes_of(gi)] * lax.rsqrt(jnp.maximum(s, 1e-24)) for (n, gi), s in zip(units, kk_ssq)]
    v_g = [v[n][:, lanes_of(gi)] for n, gi in units]
    at = [-kn * p_prev[n][:, lanes_of(gi)] for (n, gi), kn in zip(units, kkn)]
    bt = [kn * lr[n][:, lanes_of(gi)] * p_inv[n][:, lanes_of(gi)] for (n, gi), kn in zip(units, kkn)]
    kt = [k_adj[n][:, lanes_of(gi)] * p_inv[n][:, lanes_of(gi)] for n, gi in units]
    rt = [r[n][:, lanes_of(gi)] * p_incl[n][:, lanes_of(gi)] for n, gi in units]

    lhs2 = [jnp.concatenate([a, q], axis=0).astype(BF16) for a, q in zip(at, rt)]
    ab = [_dot_nt(l2, stack(b)) for l2, b in zip(lhs2, bt)]
    ak = [_dot_nt(l2, stack(kx)) for l2, kx in zip(lhs2, kt)]
    a_ab = [jnp.where(strict, x[:C], 0.0) for x in ab]
    q_ab = [jnp.where(incl, x[C:], 0.0) for x in ab]
    a_ak = [jnp.where(strict, x[:C], 0.0) for x in ak]
    q_ak = [jnp.where(incl, x[C:], 0.0) for x in ak]
    vs = [stack(x) for x in v_g]
    akv = [_dot(a.astype(BF16), s) for a, s in zip(a_ak, vs)]

    n_steps = max(1, (C - 1).bit_length())
    tiles_per_group = G // LANES
    ys, lps = [], []
    for ui in range(len(units)):
        for p in range(tiles_per_group):
            at_p = at[ui][:, p * LANES:(p + 1) * LANES]
            akv_p = akv[ui][:, p * LANES:(p + 1) * LANES]
            ys.append(jnp.where(low_half, at_p, swap_halves(akv_p)))
            ys.append(jnp.where(low_half, swap_halves(at_p), akv_p))
            for e in range(2):
                h = 2 * p + e
                lps.append(a_ab[ui][:, h * C:(h + 1) * C])
    def with_zero_rows(lp, skip):
        return lp if skip == 0 else jnp.concatenate([jnp.zeros((skip, C), F32), lp], axis=0)

    def add_rows(y, pr, skip):
        return y + pr if skip == 0 else jnp.concatenate([y[:skip], y[skip:] + pr], axis=0)

    skip = 0
    for step in range(n_steps):
        m = 1 << step
        if m >= RW_TRIM_ROWS:
            lps = [lp[m - skip:] for lp in lps]
            skip = m
        if step == n_steps - 1:
            ys = [add_rows(y, _dot(lp.astype(BF16), y.astype(BF16)), skip) for y, lp in zip(ys, lps)]
        else:
            prods = [_dot(lp.astype(BF16), jnp.concatenate([y, with_zero_rows(lp, skip)], axis=1).astype(BF16))
                     for y, lp in zip(ys, lps)]
            ys = [add_rows(y, pr[:, 0:LANES], skip) for y, pr in zip(ys, prods)]
            lps = [pr[:, LANES:] for pr in prods]
    y1, y2 = [], []
    for ui in range(len(units)):
        w_tiles, u0_tiles = [], []
        for p in range(tiles_per_group):
            y_even, y_odd = ys[ui * GH + 2 * p], ys[ui * GH + 2 * p + 1]
            w_tiles.append(jnp.where(low_half, y_even, swap_halves(y_odd)))
            u0_tiles.append(jnp.where(low_half, swap_halves(y_even), y_odd))
        y1.append(jnp.concatenate(w_tiles, axis=1))
        y2.append(jnp.concatenate(u0_tiles, axis=1))

    s0 = [state_ref[n * n_groups + gi] for n, gi in units]
    s0b = [s.astype(BF16) for s in s0]
    uu = [_dot_nt(a.astype(BF16), s) + b for a, s, b in zip(y1, s0b, y2)]
    o_state = [_dot_nt(q.astype(BF16), s) for q, s in zip(rt, s0b)]
    o_u = [_dot(q.astype(BF16), stack(x)) for q, x in zip(q_ab, uu)]
    o_v = [_dot(q.astype(BF16), s) for q, s in zip(q_ak, vs)]
    upd = [_dot_tn(jnp.concatenate([x, vv], axis=0).astype(BF16),
                   jnp.concatenate([b * p_last[n][:, lanes_of(gi)], kx * p_last[n][:, lanes_of(gi)]],
                                   axis=0).astype(BF16))
           for (n, gi), x, vv, b, kx in zip(units, uu, v_g, bt, kt)]
    for (n, gi), s, up in zip(units, s0, upd):
        state_ref[n * n_groups + gi] = s * p_last[n][:, lanes_of(gi)] + jnp.where(head_bd, up, 0.0)
    o_g = [a + b + c_ for a, b, c_ in zip(o_state, o_u, o_v)]

    mean = [head_sum(x) * (1.0 / HEAD_DIM) for x in o_g]
    dev = [x - m for x, m in zip(o_g, mean)]
    var = [head_sum(d * d) * (1.0 / HEAD_DIM) for d in dev]
    rk_sum = [head_sum(rk_prod[n][:, lanes_of(gi)]) for n, gi in units]
    outs = []
    for (n, gi), d, vr, rs, vv in zip(units, dev, var, rk_sum, v_g):
        sl = lanes_of(gi)
        gn = d * lax.rsqrt(vr + GN_EPS) * lnw_ref[:, sl] + lnb_ref[:, sl]
        outs.append((gn + rs * vv) * gate[n][:, sl])
    for n in seqs:
        o_ref[n] = jnp.concatenate(outs[n * n_groups:(n + 1) * n_groups], axis=1).astype(BF16)


def rwkv(u_rw, shift_mu, w0, w2_pad, a0, a2_pad, g2, k_k, k_a, r_k, ln_w, ln_b, B, S):
    C = RW_CHUNK
    n_seqs = RW_SEQS_PER_STEP if B % RW_SEQS_PER_STEP == 0 else 1
    assert S % C == 0
    u3 = u_rw.reshape(B, S, RW_IN)
    vec = lambda n: pl.BlockSpec((1, n), lambda b, c: (0, 0))
    mat = lambda m, n: pl.BlockSpec((m, n), lambda b, c: (0, 0))
    return pl.pallas_call(
        functools.partial(_rwkv_body, C=C, n_seqs=n_seqs),
        grid=(B // n_seqs, S // C),
        in_specs=[
            pl.BlockSpec((n_seqs, C, RW_IN), lambda b, c: (b, c, 0)),
            vec(RW_IN), vec(RW_WIDTH), mat(LORA_W + LORA_A, RW_WIDTH), vec(RW_WIDTH),
            mat(LORA_W + LORA_A, RW_WIDTH), mat(LORA_G, RW_WIDTH),
            vec(RW_WIDTH), vec(RW_WIDTH), vec(RW_WIDTH), vec(RW_WIDTH), vec(RW_WIDTH),
        ],
        out_specs=pl.BlockSpec((n_seqs, C, RW_WIDTH), lambda b, c: (b, c, 0)),
        out_shape=jax.ShapeDtypeStruct((B, S, RW_WIDTH), BF16),
        scratch_shapes=[
            pltpu.VMEM((n_seqs, 1, RW_IN), F32),
            pltpu.VMEM((n_seqs * (RW_WIDTH // RW_GROUP), RW_GROUP, RW_GROUP), F32),
        ],
        compiler_params=_cparams(("arbitrary", "arbitrary")),
        name="rwkv",
    )(u3, shift_mu, w0, w2_pad, a0, a2_pad, g2, k_k, k_a, r_k, ln_w, ln_b)


OR_TM = 1024
OR_PARTS = 2
N_PAIRS = EXPERTS_PER_GROUP * (EXPERTS_PER_GROUP - 1) // 2
N_CLASSES = N_GROUPS * N_PAIRS
CLS_ROWS = LANES
ROUTER_ROWS = 48
SUBLANES = 8
H_CHUNKS = D_MODEL // LANES
SLAB_IN = H_CHUNKS + 1
SLAB_OUT = H_CHUNKS
OUT_PITCH = SLAB_OUT + 1


def _first_index_of(vals, target, row_f):
    return jnp.min(jnp.where(vals == target, row_f, 1e9), axis=0, keepdims=True)


def _out_router_body(x_ref, osb_ref, orw_ref, wo_ref, g_ref, wr_ref, wrhi_ref, br_ref,
                     slab_ref, meta_ref, hist_ref, carry_ref, *, tm):
    i = pl.program_id(0)

    @pl.when(i == 0)
    def _():
        carry_ref[...] = jnp.zeros_like(carry_ref)

    parts = range(OR_PARTS)
    tp = tm // OR_PARTS
    rows_of = lambda p: slice(p * tp, (p + 1) * tp)
    hs = [x_ref[rows_of(p), :] + _dot(osb_ref[rows_of(p), :], wo_ref[0:SB_WIDTH, :])
          + _dot(orw_ref[rows_of(p), :], wo_ref[SB_WIDTH:, :]) for p in parts]
    carry = carry_ref[...]
    for p in parts:
        carry = _route_part(p, tp, hs[p], carry, g_ref, wr_ref, wrhi_ref, br_ref, slab_ref, meta_ref)
    carry_ref[...] = carry
    hist_ref[...] = carry


def _route_part(p, tm, h, carry, g_ref, wr_ref, wrhi_ref, br_ref, slab_ref, meta_ref):
    ms = jnp.mean(h * h, axis=-1, keepdims=True)
    xn = h * lax.rsqrt(ms + RMS_EPS) * g_ref[...]

    xn_hi, xn_lo = _split_bf16(xn)
    both = _dot_nt(wr_ref[...], xn_hi)
    logits = both[:ROUTER_ROWS] + both[ROUTER_ROWS:] + _dot_nt(wrhi_ref[...], xn_lo) + br_ref[...][:, 0:1]

    row_f = lax.broadcasted_iota(jnp.int32, (SUBLANES, tm), 0).astype(F32)
    lg = jnp.where(row_f < N_GROUPS, logits[N_EXPERTS:N_EXPERTS + SUBLANES], -jnp.inf)
    eg = jnp.exp(lg - jnp.max(lg, axis=0, keepdims=True))
    pg = eg / jnp.sum(eg, axis=0, keepdims=True)
    g_val = jnp.max(pg, axis=0, keepdims=True)
    g_idx = _first_index_of(pg, g_val, row_f)
    sel = jnp.zeros((SUBLANES, tm), F32)
    for g in range(N_GROUPS):
        sel = jnp.where(g_idx == g, logits[g * EXPERTS_PER_GROUP:(g + 1) * EXPERTS_PER_GROUP], sel)
    ee = jnp.exp(sel - jnp.max(sel, axis=0, keepdims=True))
    pe = ee / jnp.sum(ee, axis=0, keepdims=True)
    e1 = jnp.max(pe, axis=0, keepdims=True)
    i1 = _first_index_of(pe, e1, row_f)
    pe2 = jnp.where(row_f == i1, -1.0, pe)
    e2 = jnp.max(pe2, axis=0, keepdims=True)
    i2 = _first_index_of(pe2, e2, row_f)
    den = e1 + e2
    wt1 = g_val * e1 / den
    wt2 = g_val * e2 / den
    first_lo = i1 < i2
    lo = jnp.where(first_lo, i1, i2)
    hi = jnp.where(first_lo, i2, i1)
    w_lo = jnp.where(first_lo, wt1, wt2)
    w_hi = jnp.where(first_lo, wt2, wt1)
    pair = lo * (2 * EXPERTS_PER_GROUP - 1 - lo) * 0.5 + (hi - lo - 1.0)
    cls = g_idx * N_PAIRS + pair

    cls_row = lax.broadcasted_iota(jnp.int32, (CLS_ROWS, tm), 0).astype(F32)
    onehot = cls_row == cls
    onehot_b = jnp.where(onehot, 1.0, 0.0).astype(BF16)
    upto = (lax.broadcasted_iota(jnp.int32, (tm, tm), 0)
            <= lax.broadcasted_iota(jnp.int32, (tm, tm), 1)).astype(BF16)
    cum = _dot(onehot_b, upto)
    tot = _dot(onehot_b, jnp.ones((tm, LANES), BF16))
    before = jnp.concatenate([carry] * (tm // LANES), axis=1)
    rank = jnp.sum(jnp.where(onehot, cum - 1.0 + before, 0.0), axis=0, keepdims=True)

    meta = jnp.concatenate([cls, rank, jnp.zeros((SUBLANES - 2, tm), F32)], axis=0)
    meta_ref[0, :, p * tm:(p + 1) * tm] = meta.astype(jnp.int32)

    w_rows = jnp.concatenate([w_lo, w_hi, jnp.zeros((LANES - 2, tm), F32)], axis=0)
    slab_rows = lambda j: pl.ds(p * tm * SLAB_IN + j, tm, stride=SLAB_IN)
    for j in range(H_CHUNKS):
        slab_ref[slab_rows(j), :] = h[:, j * LANES:(j + 1) * LANES]
    slab_ref[slab_rows(H_CHUNKS), :] = jnp.transpose(w_rows)
    return carry + tot


def out_router(x2d, o_sb, o_rw, w_out, g_ffn, wr_both, wr_hi, b_r):
    T = x2d.shape[0]
    tm = min(OR_TM, T)
    nt = T // tm
    const = lambda *shape: pl.BlockSpec(shape, lambda i: (0,) * len(shape))
    return pl.pallas_call(
        functools.partial(_out_router_body, tm=tm),
        grid=(nt,),
        in_specs=[
            pl.BlockSpec((tm, D_MODEL), lambda i: (i, 0)),
            pl.BlockSpec((tm, SB_WIDTH), lambda i: (i, 0)),
            pl.BlockSpec((tm, RW_WIDTH), lambda i: (i, 0)),
            const(D_MODEL, D_MODEL), const(1, D_MODEL),
            const(2 * ROUTER_ROWS, D_MODEL), const(ROUTER_ROWS, D_MODEL), const(ROUTER_ROWS, LANES),
        ],
        out_specs=[
            pl.BlockSpec((tm * SLAB_IN, LANES), lambda i: (i, 0)),
            pl.BlockSpec((1, SUBLANES, tm), lambda i: (i, 0, 0)),
            const(CLS_ROWS, LANES),
        ],
        out_shape=[
            jax.ShapeDtypeStruct((T * SLAB_IN, LANES), F32),
            jax.ShapeDtypeStruct((nt, SUBLANES, tm), jnp.int32),
            jax.ShapeDtypeStruct((CLS_ROWS, LANES), F32),
        ],
        scratch_shapes=[pltpu.VMEM((CLS_ROWS, LANES), F32)],
        compiler_params=_cparams(("arbitrary",)),
        name="out_router",
    )(x2d, o_sb, o_rw, w_out, g_ffn, wr_both, wr_hi, b_r)


PERM_TOKENS_PER_STEP = 1024
PERM_WINDOW = 256
PERM_UNROLL = 8
DMA_PRIORITIES = 2


def _slab_scatter_body(pos_ref, src_ref, dst_ref, sem, *, tokens, slab):
    window = min(PERM_WINDOW, tokens)

    def slab_copy(j, dst_tok):
        return pltpu.make_async_copy(src_ref.at[pl.ds(j * slab, slab)],
                                     dst_ref.at[pl.ds(dst_tok * slab, slab)], sem)

    def start_batch(b, carry):
        for j in range(PERM_UNROLL):
            tok = b * PERM_UNROLL + j
            slab_copy(tok, pos_ref[0, 0, tok]).start(priority=j % DMA_PRIORITIES)
        return carry

    def retire(j, carry):
        slab_copy(0, 0).wait()
        return carry

    def start_batch_retire_batch(b, carry):
        start_batch(b, carry)
        lax.fori_loop(0, PERM_UNROLL, retire, 0, unroll=True)
        return carry

    lax.fori_loop(0, window // PERM_UNROLL, start_batch, 0)
    lax.fori_loop(window // PERM_UNROLL, tokens // PERM_UNROLL, start_batch_retire_batch, 0)
    lax.fori_loop(0, window, retire, 0, unroll=PERM_UNROLL)


def slab_scatter(src, pos, slab):
    n_tok = pos.shape[0]
    tokens = min(PERM_TOKENS_PER_STEP, n_tok)
    steps = n_tok // tokens
    return pl.pallas_call(
        functools.partial(_slab_scatter_body, tokens=tokens, slab=slab),
        grid=(steps,),
        in_specs=[
            pl.BlockSpec((1, 1, tokens), lambda s: (s, 0, 0), memory_space=pltpu.SMEM),
            pl.BlockSpec((tokens * slab, LANES), lambda s: (s, 0)),
        ],
        out_specs=pl.BlockSpec(memory_space=pl.ANY),
        out_shape=jax.ShapeDtypeStruct(src.shape, src.dtype),
        scratch_shapes=[pltpu.SemaphoreType.DMA(())],
        compiler_params=_cparams(("arbitrary",)),
        name="slab_scatter",
    )(pos.reshape(steps, 1, tokens), src)


GATHER_TM = 512


def _slab_gather_body(pos_ref, nxt_ref, src_ref, g_ref, o_ref, buf, sem, *, tm):
    i = pl.program_id(0)
    n_steps = pl.num_programs(0)
    slot = i % 2

    def slab_copy(tok, j, to_slot):
        return pltpu.make_async_copy(src_ref.at[pl.ds(tok * OUT_PITCH, SLAB_OUT)],
                                     buf.at[to_slot, pl.ds(j * SLAB_OUT, SLAB_OUT)], sem.at[to_slot])

    def fetch(idx_ref, to_slot):
        def issue(j, carry):
            slab_copy(idx_ref[0, 0, j], j, to_slot).start()
            return carry
        lax.fori_loop(0, tm, issue, 0, unroll=PERM_UNROLL)

    @pl.when(i == 0)
    def _():
        fetch(pos_ref, 0)

    @pl.when(i + 1 < n_steps)
    def _():
        fetch(nxt_ref, 1 - slot)

    def drain(j, carry):
        slab_copy(0, 0, slot).wait()
        return carry

    lax.fori_loop(0, tm, drain, 0, unroll=PERM_UNROLL)
    chunks = [buf[slot, pl.ds(j, tm, stride=SLAB_OUT), :] for j in range(H_CHUNKS)]
    squares = functools.reduce(lambda a, b: a + b, [x * x for x in chunks])
    scale = lax.rsqrt(jnp.sum(squares, axis=-1, keepdims=True) * (1.0 / D_MODEL) + RMS_EPS)
    for j in range(H_CHUNKS):
        o_ref[:, j * LANES:(j + 1) * LANES] = chunks[j] * scale * g_ref[:, j * LANES:(j + 1) * LANES]


def slab_gather(src, pos, g_final):
    n_tok = pos.shape[0]
    tm = min(GATHER_TM, n_tok)
    steps = n_tok // tm
    pos3 = pos.reshape(steps, 1, tm)
    return pl.pallas_call(
        functools.partial(_slab_gather_body, tm=tm),
        grid=(steps,),
        in_specs=[
            pl.BlockSpec((1, 1, tm), lambda i: (i, 0, 0), memory_space=pltpu.SMEM),
            pl.BlockSpec((1, 1, tm), lambda i: (jnp.minimum(i + 1, steps - 1), 0, 0), memory_space=pltpu.SMEM),
            pl.BlockSpec(memory_space=pl.ANY),
            pl.BlockSpec((1, D_MODEL), lambda i: (0, 0)),
        ],
        out_specs=pl.BlockSpec((tm, D_MODEL), lambda i: (i, 0)),
        out_shape=jax.ShapeDtypeStruct((n_tok, D_MODEL), src.dtype),
        scratch_shapes=[
            pltpu.VMEM((2, tm * SLAB_OUT, LANES), src.dtype),
            pltpu.SemaphoreType.DMA((2,)),
        ],
        compiler_params=_cparams(("arbitrary",)),
        name="slab_gather",
    )(pos3, pos3, src, g_final)


EX_TM = 256


def _experts_body(tile_ref, lo_ref, hi_ref, start_ref, end_ref,
                  hs_ref, gffn_ref, wg_lo, wu_lo, wd_lo, wg_hi, wu_hi, wd_hi, o_ref, *, tm):
    i = pl.program_id(0)
    tile = tile_ref[i]
    start = start_ref[i]
    end = end_ref[i]

    whole = jnp.logical_and(start <= tile * tm, end >= (tile + 1) * tm)
    first_of_tile = jnp.logical_or(i == 0, tile != tile_ref[jnp.maximum(i - 1, 0)])

    @pl.when(jnp.logical_and(first_of_tile, jnp.logical_not(whole)))
    def _():
        o_ref[...] = jnp.zeros_like(o_ref)

    @pl.when(end > start)
    def _():
        in_rows = lambda j: pl.ds(j, tm, stride=SLAB_IN)
        out_rows = lambda j: pl.ds(j, tm, stride=OUT_PITCH)
        h = jnp.concatenate([hs_ref[in_rows(j), :] for j in range(H_CHUNKS)], axis=1)
        w_row = hs_ref[in_rows(H_CHUNKS), :]
        w_lo = w_row[:, 0:1]
        w_hi = w_row[:, 1:2]
        ms = jnp.mean(h * h, axis=-1, keepdims=True)
        xn = (h * lax.rsqrt(ms + RMS_EPS) * gffn_ref[...]).astype(BF16)

        g_lo = _dot(xn, wg_lo[0])
        g_hi = _dot(xn, wg_hi[0])
        u_lo = _dot(xn, wu_lo[0])
        u_hi = _dot(xn, wu_hi[0])
        a_lo = (g_lo * _sigmoid(g_lo) * u_lo * w_lo).astype(BF16)
        a_hi = (g_hi * _sigmoid(g_hi) * u_hi * w_hi).astype(BF16)
        res = h + _dot(a_lo, wd_lo[0]) + _dot(a_hi, wd_hi[0])
        @pl.when(whole)
        def _():
            for j in range(H_CHUNKS):
                o_ref[out_rows(j), :] = res[:, j * LANES:(j + 1) * LANES]
            o_ref[out_rows(SLAB_OUT), :] = jnp.zeros((tm, LANES), F32)

        @pl.when(jnp.logical_not(whole))
        def _():
            rows = tile * tm + lax.broadcasted_iota(jnp.int32, (tm, 1), 0)
            mine = jnp.logical_and(rows >= start, rows < end)
            for j in range(H_CHUNKS):
                o_ref[out_rows(j), :] = jnp.where(mine, res[:, j * LANES:(j + 1) * LANES], o_ref[out_rows(j), :])


def experts(h_sorted, g_ffn, w_gate, w_up, w_down, item_tile, item_lo, item_hi, item_start, item_end):
    T = h_sorted.shape[0] // SLAB_IN
    tm = min(EX_TM, T)
    n_items = item_tile.shape[0]
    gate_spec = lambda which: pl.BlockSpec(
        (1, D_MODEL, D_EXPERT), lambda i, tl, lo, hi, st, en: ((lo, hi)[which][i], 0, 0))
    down_spec = lambda which: pl.BlockSpec(
        (1, D_EXPERT, D_MODEL), lambda i, tl, lo, hi, st, en: ((lo, hi)[which][i], 0, 0))
    vec_spec = pl.BlockSpec((1, D_MODEL), lambda i, tl, lo, hi, st, en: (0, 0))
    grid_spec = pltpu.PrefetchScalarGridSpec(
        num_scalar_prefetch=5,
        grid=(n_items,),
        in_specs=[
            pl.BlockSpec((tm * SLAB_IN, LANES), lambda i, tl, lo, hi, st, en: (tl[i], 0)),
            vec_spec,
            gate_spec(0), gate_spec(0), down_spec(0),
            gate_spec(1), gate_spec(1), down_spec(1),
        ],
        out_specs=pl.BlockSpec((tm * OUT_PITCH, LANES), lambda i, tl, lo, hi, st, en: (tl[i], 0)),
    )
    return pl.pallas_call(
        functools.partial(_experts_body, tm=tm),
        grid_spec=grid_spec,
        out_shape=jax.ShapeDtypeStruct((T * OUT_PITCH, LANES), F32),
        compiler_params=_cparams(("arbitrary",)),
        name="experts",
    )(item_tile, item_lo, item_hi, item_start, item_end,
      h_sorted, g_ffn, w_gate, w_up, w_down, w_gate, w_up, w_down)


def _pair_first(lo):
    return (lo * (2 * EXPERTS_PER_GROUP - 1 - lo)) // 2


def _work_items(hist, meta, T, tm):
    i32 = jnp.int32
    counts = hist[:N_CLASSES, 0].astype(i32)
    offs = jnp.cumsum(counts) - counts
    cls = meta[:, 0, :].reshape(T)
    rank = meta[:, 1, :].reshape(T)
    class_ids = jnp.arange(N_CLASSES, dtype=i32)
    pos = jnp.sum(jnp.where(cls[:, None] == class_ids[None, :], offs[None, :], 0), axis=1) + rank

    n_tiles = T // tm
    n_items = n_tiles + N_CLASSES
    tile_starts = jnp.arange(n_tiles, dtype=i32) * tm
    tile_slot = jnp.arange(n_tiles, dtype=i32) + jnp.sum(offs[None, :] < tile_starts[:, None], axis=1, dtype=i32)
    class_slot = class_ids + jnp.sum(tile_starts[None, :] <= offs[:, None], axis=1, dtype=i32)
    slots = jnp.arange(n_items, dtype=i32)
    starts = (jnp.sum(jnp.where(tile_slot[None, :] == slots[:, None], tile_starts[None, :], 0), axis=1)
              + jnp.sum(jnp.where(class_slot[None, :] == slots[:, None], offs[None, :], 0), axis=1))
    ends = jnp.concatenate([starts[1:], jnp.full((1,), T, i32)])
    item_tile = jnp.minimum(starts // tm, n_tiles - 1)
    item_cls = jnp.sum(offs[None, :] <= starts[:, None], axis=1, dtype=i32) - 1
    grp = item_cls // N_PAIRS
    q = item_cls % N_PAIRS
    firsts = _pair_first(jnp.arange(1, EXPERTS_PER_GROUP - 1, dtype=i32))
    lo = jnp.sum(q[:, None] >= firsts[None, :], axis=1, dtype=i32)
    hi = q - _pair_first(lo) + lo + 1
    base = grp * EXPERTS_PER_GROUP
    return pos, item_tile, base + lo, base + hi, starts, ends


def kernel(x, norm_mix_g, w_in, shift_mu, sb_out_g, rw_w0, rw_w2, rw_a0, rw_a2, rw_g2, rw_k_k, rw_k_a, rw_r_k, rw_ln_w, rw_ln_b, w_out, norm_ffn_g, router_grp_w, router_grp_b, router_exp_w, router_exp_b, exp_w_gate, exp_w_up, exp_w_down, final_norm_g):
    B, S, D = x.shape
    T = B * S
    assert D == D_MODEL and w_in.shape[0] == 1, "one layer of width 1024 is what these kernels implement"
    l = 0
    row = lambda a: a.reshape(1, -1)
    x2d = x.reshape(T, D)

    w_in_b = w_in[l].astype(BF16)
    u_sb, u_rw = in_proj(x2d, row(norm_mix_g[l]), w_in_b[:, :SB_IN], w_in_b[:, SB_IN:])
    o_sb = sb_attn(u_sb, row(sb_out_g[l]), B, S)
    w2_pad = jnp.concatenate([rw_w2[l], jnp.zeros_like(rw_a2[l])], axis=0).astype(BF16)
    a2_pad = jnp.concatenate([jnp.zeros_like(rw_w2[l]), rw_a2[l]], axis=0).astype(BF16)
    o_rw = rwkv(u_rw, row(shift_mu[l]), row(rw_w0[l]), w2_pad, row(rw_a0[l]), a2_pad, rw_g2[l].astype(BF16),
                row(rw_k_k[l]), row(rw_k_a[l]), row(rw_r_k[l]), row(rw_ln_w[l]), row(rw_ln_b[l]), B, S)

    pad_rows = ROUTER_ROWS - N_EXPERTS - N_GROUPS
    wr = jnp.concatenate([router_exp_w[l].T, router_grp_w[l].T, jnp.zeros((pad_rows, D), F32)], axis=0)
    wr_hi, wr_lo = _split_bf16(wr)
    b_r = jnp.concatenate([router_exp_b[l], router_grp_b[l], jnp.zeros((pad_rows,), F32)])
    b_r = jnp.broadcast_to(b_r[:, None], (ROUTER_ROWS, LANES))
    h_slabs, meta, hist = out_router(x2d, o_sb.reshape(T, SB_WIDTH), o_rw.reshape(T, RW_WIDTH),
                                     w_out[l].astype(BF16), row(norm_ffn_g[l]),
                                     jnp.concatenate([wr_hi, wr_lo], axis=0), wr_hi, b_r)

    tm = min(EX_TM, T)
    pos, item_tile, item_lo, item_hi, item_start, item_end = _work_items(hist, meta, T, tm)
    h_sorted = slab_scatter(h_slabs, pos, SLAB_IN)
    out_sorted = experts(h_sorted, row(norm_ffn_g[l]),
                         exp_w_gate[l].astype(BF16), exp_w_up[l].astype(BF16), exp_w_down[l].astype(BF16),
                         item_tile, item_lo, item_hi, item_start, item_end)
    out = slab_gather(out_sorted, pos, row(final_norm_g))
    return out.reshape(B, S, D)
```

```python
import functools
import math

import jax
import jax.numpy as jnp
from jax import lax
from jax.experimental import pallas as pl
from jax.experimental.pallas import tpu as pltpu

F32 = jnp.float32
BF16 = jnp.bfloat16

D_MODEL = 1024
SB_HEADS = 8
HEAD_DIM = 64
SB_WIDTH = SB_HEADS * HEAD_DIM
RW_HEADS = 8
RW_WIDTH = RW_HEADS * HEAD_DIM
LORA_W = 64
LORA_A = 64
LORA_G = 128
SB_IN = 3 * SB_WIDTH
RW_IN = 3 * RW_WIDTH + LORA_W + LORA_A + LORA_G
N_GROUPS = 4
EXPERTS_PER_GROUP = 8
N_EXPERTS = N_GROUPS * EXPERTS_PER_GROUP
D_EXPERT = 256
RMS_EPS = 1e-6
GN_EPS = 64e-5

LANES = 128
MXU_DIM = 256
VMEM_LIMIT = 48 * 1024 * 1024


def _cparams(sem):
    return pltpu.CompilerParams(dimension_semantics=sem, vmem_limit_bytes=VMEM_LIMIT)


def _dot(a, b):
    return jnp.dot(a, b, preferred_element_type=F32)


def _dot_nt(a, b):
    return lax.dot_general(a, b, (((1,), (1,)), ((), ())), preferred_element_type=F32)


def _dot_tn(a, b):
    return lax.dot_general(a, b, (((0,), (0,)), ((), ())), preferred_element_type=F32)


IN_TM = 1024


def _in_proj_body(x_ref, g_ref, wsb_ref, wrw_ref, usb_ref, urw_ref):
    x = x_ref[...]
    ms = jnp.mean(x * x, axis=-1, keepdims=True)
    xn = (x * lax.rsqrt(ms + RMS_EPS) * g_ref[...]).astype(BF16)
    usb_ref[...] = _dot(xn, wsb_ref[...]).astype(BF16)
    urw_ref[...] = _dot(xn, wrw_ref[...]).astype(BF16)


def in_proj(x2d, g, w_sb, w_rw):
    T = x2d.shape[0]
    tm = min(IN_TM, T)
    return pl.pallas_call(
        _in_proj_body,
        grid=(T // tm,),
        in_specs=[
            pl.BlockSpec((tm, D_MODEL), lambda i: (i, 0)),
            pl.BlockSpec((1, D_MODEL), lambda i: (0, 0)),
            pl.BlockSpec((D_MODEL, SB_IN), lambda i: (0, 0)),
            pl.BlockSpec((D_MODEL, RW_IN), lambda i: (0, 0)),
        ],
        out_specs=[
            pl.BlockSpec((tm, SB_IN), lambda i: (i, 0)),
            pl.BlockSpec((tm, RW_IN), lambda i: (i, 0)),
        ],
        out_shape=[
            jax.ShapeDtypeStruct((T, SB_IN), BF16),
            jax.ShapeDtypeStruct((T, RW_IN), BF16),
        ],
        compiler_params=_cparams(("arbitrary",)),
        name="in_proj",
    )(x2d, g, w_sb, w_rw)


SB_BLK = 256
SB_TILES_PER_STEP = 4
SB_BLOCKS_PER_STEP = 2
SB_SKIP = 40.0


def _decay(z):
    one = jnp.asarray(1.0, z.dtype)
    zero = jnp.asarray(0.0, z.dtype)
    return jnp.maximum(z, zero) + jnp.log(one + jnp.exp(-jnp.abs(z)))


def _sb_attn_body(q_ref, k_ref, v_ref, g_ref, later_h_ref, later_f_ref, mask_t_ref, mask_b_ref, o_ref,
                  *, blk, n_tiles, n_sub):
    first_qb = pl.program_id(2) * n_sub
    blocks = [_sb_block(first_qb + e, e * blk, q_ref, k_ref, v_ref, g_ref, later_h_ref, later_f_ref,
                        mask_t_ref, mask_b_ref, o_ref, blk=blk, n_tiles=n_tiles) for e in range(n_sub)]
    for _ in range(SB_STAGES):
        for block in blocks:
            next(block)
    for block in blocks:
        for _ in block:
            pass


SB_STAGES = 3


def _sb_block(qb, row0, q_ref, k_ref, v_ref, g_ref, later_h_ref, later_f_ref, mask_t_ref, mask_b_ref, o_ref,
              *, blk, n_tiles):
    half = blk // 2
    tiles = range(n_tiles)
    lane = lax.broadcasted_iota(jnp.int32, (1, LANES), 1)
    head_masks = [lane < HEAD_DIM, lane >= HEAD_DIM]
    lanes_of = lambda t: slice(t * LANES, (t + 1) * LANES)

    def stack_heads(x):
        return jnp.concatenate([jnp.where(m, x, 0.0) for m in head_masks], axis=0).astype(BF16)

    def unstack(pv):
        n = pv.shape[0] // 2
        return jnp.where(head_masks[0], pv[:n], pv[n:])

    def row_sum(d):
        return jnp.sum(d.astype(F32), axis=-1, keepdims=True)

    q = [q_ref[0, row0:row0 + blk, lanes_of(t)].astype(F32) * (1.0 / math.sqrt(HEAD_DIM)) for t in tiles]
    q_top = [stack_heads(x[:half]) for x in q]
    q_bot = [stack_heads(x[half:]) for x in q]

    later_h = later_h_ref[...]
    later_f = later_f_ref[...]
    mask_t = mask_t_ref[...]
    mask_b = mask_b_ref[...]

    def kv_rows(index, size):
        start = pl.multiple_of(index * size, size)
        return ([k_ref[0, pl.ds(start, size), lanes_of(t)] for t in tiles],
                [v_ref[0, pl.ds(start, size), lanes_of(t)] for t in tiles])

    has_prev = qb >= 1
    kd, vd = kv_rows(qb, blk)
    kp, vp = kv_rows(jnp.maximum(2 * qb - 1, 0), half)
    z_td = [_dot_nt(x, k[:half]) for x, k in zip(q_top, kd)]
    z_bd = [_dot_nt(x, k) for x, k in zip(q_bot, kd)]
    z_tp = [_dot_nt(x, k) for x, k in zip(q_top, kp)]
    yield
    d_td = [_decay(z.astype(BF16)) * mask_t for z in z_td]
    d_bd = [_decay(z.astype(BF16)) * mask_b for z in z_bd]
    d_tp = [_decay(z.astype(BF16)) for z in z_tp]
    s_td = [_dot(d, later_h) for d in d_td]
    s_bd = [_dot(d, later_f) for d in d_bd]
    s_tp = [_dot(d, later_h) for d in d_tp]
    yield
    c_td = [row_sum(d) for d in d_td]
    w_td = [jnp.exp(jnp.minimum(z - s, 0.0)).astype(BF16) * mask_t for z, s in zip(z_td, s_td)]
    w_bd = [jnp.exp(jnp.minimum(z - s, 0.0)).astype(BF16) * mask_b for z, s in zip(z_bd, s_bd)]
    w_tp = [jnp.where(has_prev, jnp.exp(z - s - c), 0.0).astype(BF16) for z, s, c in zip(z_tp, s_tp, c_td)]
    acc_top = [unstack(_dot(a, v[:half]) + _dot(b, vv)) for a, v, b, vv in zip(w_td, vd, w_tp, vp)]
    acc_bot = [unstack(_dot(a, v)) for a, v in zip(w_bd, vd)]
    carry_top = [c + row_sum(d) for c, d in zip(c_td, d_tp)]
    carry_bot = [row_sum(d) for d in d_bd]
    yield

    def earlier_chunks(qs, first, carry, acc):
        def alive_of(carry):
            return functools.reduce(jnp.minimum, [jnp.min(c) for c in carry]) < SB_SKIP

        def cond(st):
            return jnp.logical_and(st[0] >= 0, st[1])

        def body(st):
            c, _, carry, acc = st
            kc, vc = kv_rows(c, half)
            z = [_dot_nt(x, k) for x, k in zip(qs, kc)]
            d = [_decay(x.astype(BF16)) for x in z]
            s = [_dot(x, later_h) for x in d]
            w = [jnp.exp(zz - ss - cr).astype(BF16) for zz, ss, cr in zip(z, s, carry)]
            acc = [a + unstack(_dot(x, vv)) for a, x, vv in zip(acc, w, vc)]
            carry = [cr + row_sum(dd) for cr, dd in zip(carry, d)]
            return c - 1, alive_of(carry), carry, acc

        return lax.while_loop(cond, body, (first, alive_of(carry), carry, acc))[3]

    acc_top = earlier_chunks(q_top, 2 * qb - 2, carry_top, acc_top)
    acc_bot = earlier_chunks(q_bot, 2 * qb - 1, carry_bot, acc_bot)

    for t in tiles:
        for rows, acc in ((slice(row0, row0 + half), acc_top[t]), (slice(row0 + half, row0 + blk), acc_bot[t])):
            sq = acc * acc
            s_lo = jnp.sum(jnp.where(head_masks[0], sq, 0.0), axis=-1, keepdims=True)
            s_all = jnp.sum(sq, axis=-1, keepdims=True)
            ms = jnp.where(head_masks[0], s_lo, s_all - s_lo) * (1.0 / HEAD_DIM)
            o_ref[0, rows, lanes_of(t)] = (acc * lax.rsqrt(ms + RMS_EPS) * g_ref[:, lanes_of(t)]).astype(BF16)


def sb_attn(u_sb, sb_out_g, B, S):
    blk = min(SB_BLK, S)
    half = blk // 2
    u3 = u_sb.reshape(B, S, SB_IN)
    n_pairs = SB_WIDTH // LANES
    after = lambda n: (jnp.arange(n)[:, None] >= jnp.arange(n)[None, :]).astype(BF16)
    before = (jnp.arange(half)[None, :] < jnp.arange(half)[:, None]).astype(BF16)
    mask_t = jnp.tile(before, (2, 1))
    mask_b = jnp.tile(jnp.concatenate([jnp.ones((half, half), BF16), before], axis=1), (2, 1))
    n_tiles = SB_TILES_PER_STEP
    width = n_tiles * LANES
    n_groups = n_pairs // n_tiles
    const = lambda r, c: pl.BlockSpec((r, c), lambda b, p, i: (0, 0))
    n_sub = SB_BLOCKS_PER_STEP if (S // blk) % SB_BLOCKS_PER_STEP == 0 else 1
    return pl.pallas_call(
        functools.partial(_sb_attn_body, blk=blk, n_tiles=n_tiles, n_sub=n_sub),
        grid=(B, n_groups, S // (n_sub * blk)),
        in_specs=[
            pl.BlockSpec((1, n_sub * blk, width), lambda b, p, i: (b, i, p)),
            pl.BlockSpec((1, S, width), lambda b, p, i: (b, 0, n_groups + p)),
            pl.BlockSpec((1, S, width), lambda b, p, i: (b, 0, 2 * n_groups + p)),
            pl.BlockSpec((1, width), lambda b, p, i: (0, p)),
            const(half, half), const(blk, blk), const(blk, half), const(blk, blk),
        ],
        out_specs=pl.BlockSpec((1, n_sub * blk, width), lambda b, p, i: (b, i, p)),
        out_shape=jax.ShapeDtypeStruct((B, S, SB_WIDTH), BF16),
        compiler_params=_cparams(("arbitrary", "arbitrary", "arbitrary")),
        name="sb_attn",
    )(u3, u3, u3, sb_out_g, after(half), after(blk), mask_t, mask_b)


RW_CHUNK = LANES
RW_SEQS_PER_STEP = 4
RW_TRIM_ROWS = 16
RW_GROUP = MXU_DIM
RW_GROUP_HEADS = RW_GROUP // HEAD_DIM


def _softplus(y):
    return jnp.maximum(y, 0.0) + jnp.log(1.0 + jnp.exp(-jnp.abs(y)))


def _sigmoid(y):
    return 1.0 / (1.0 + jnp.exp(-y))


def _split_bf16(x):
    hi = x.astype(BF16)
    lo = (x - hi.astype(F32)).astype(BF16)
    return hi, lo


def _rwkv_body(u_ref, mu_ref, w0_ref, w2_ref, a0_ref, a2_ref, g2_ref, kk_ref, ka_ref, rk_ref,
               lnw_ref, lnb_ref, o_ref, prev_ref, state_ref, *, C, n_seqs):
    c = pl.program_id(1)
    G, GH = RW_GROUP, RW_GROUP_HEADS
    n_groups = RW_WIDTH // G

    @pl.when(c == 0)
    def _():
        prev_ref[...] = jnp.zeros_like(prev_ref)
        state_ref[...] = jnp.zeros_like(state_ref)

    head_bd = (lax.broadcasted_iota(jnp.int32, (G, G), 0) // HEAD_DIM
               == lax.broadcasted_iota(jnp.int32, (G, G), 1) // HEAD_DIM)
    ones_bd = head_bd.astype(BF16)
    stack_mask = (lax.broadcasted_iota(jnp.int32, (GH * C, G), 0) // C
                  == lax.broadcasted_iota(jnp.int32, (GH * C, G), 1) // HEAD_DIM)
    low_half = lax.broadcasted_iota(jnp.int32, (1, LANES), 1) < HEAD_DIM
    tt = lax.broadcasted_iota(jnp.int32, (C, GH * C), 0)
    ss = lax.broadcasted_iota(jnp.int32, (C, GH * C), 1) % C
    strict = ss < tt
    incl = ss <= tt
    tri_incl = (lax.broadcasted_iota(jnp.int32, (C, C), 1)
                <= lax.broadcasted_iota(jnp.int32, (C, C), 0)).astype(BF16)

    def head_sum(x):
        return _dot(x.astype(BF16), ones_bd)

    def stack(x):
        return jnp.where(stack_mask, jnp.concatenate([x] * GH, axis=0), 0.0).astype(BF16)

    def swap_halves(x):
        return pltpu.roll(x, HEAD_DIM, axis=1)

    seqs = range(n_seqs)
    units = [(n, gi) for n in seqs for gi in range(n_groups)]
    lanes_of = lambda gi: slice(gi * G, (gi + 1) * G)

    ums = []
    for n in seqs:
        u = u_ref[n].astype(F32)
        row_id = lax.broadcasted_iota(jnp.int32, (C, 1), 0)
        shifted = jnp.where(row_id == 0, prev_ref[n], pltpu.roll(u, 1, axis=0))
        prev_ref[n] = u[C - 1:C, :]
        ums.append(u + (shifted - u) * mu_ref[...])
    r = [um[:, 0:RW_WIDTH] for um in ums]
    k = [um[:, RW_WIDTH:2 * RW_WIDTH] for um in ums]
    v = [um[:, 2 * RW_WIDTH:3 * RW_WIDTH] for um in ums]
    xwa = [um[:, 3 * RW_WIDTH:3 * RW_WIDTH + LORA_W + LORA_A] for um in ums]
    xg = [um[:, 3 * RW_WIDTH + LORA_W + LORA_A:] for um in ums]

    lora_w = [_dot(jnp.tanh(x).astype(BF16), w2_ref[...]) for x in xwa]
    lora_a = [_dot(x.astype(BF16), a2_ref[...]) for x in xwa]
    gate = [_dot(_sigmoid(x).astype(BF16), g2_ref[...]) for x in xg]
    logdec = [-jnp.exp(-_softplus(-(w0_ref[...] + lw)) - 0.5) for lw in lora_w]
    lr = [_sigmoid(a0_ref[...] + la) for la in lora_a]

    splits = [_split_bf16(ld) for ld in logdec]
    cum = [_dot(tri_incl, hi) + _dot(tri_incl, lo) for hi, lo in splits]
    p_incl = [jnp.exp(cm) for cm in cum]
    p_prev = [jnp.exp(cm - ld) for cm, ld in zip(cum, logdec)]
    p_inv = [jnp.exp(-cm) for cm in cum]
    p_last = [p[C - 1:C, :] for p in p_incl]

    kk = [kn * kk_ref[...] for kn in k]
    k_adj = [kn * (1.0 + (lrn - 1.0) * ka_ref[...]) for kn, lrn in zip(k, lr)]
    rk_prod = [rn * kan * rk_ref[...] for rn, kan in zip(r, k_adj)]

    kk_ssq = [head_sum(kk[n][:, lanes_of(gi)] * kk[n][:, lanes_of(gi)]) for n, gi in units]
    kkn = [kk[n][:, lanes_of(gi)] * lax.rsqrt(jnp.maximum(s, 1e-24)) for (n, gi), s in zip(units, kk_ssq)]
    v_g = [v[n][:, lanes_of(gi)] for n, gi in units]
    at = [-kn * p_prev[n][:, lanes_of(gi)] for (n, gi), kn in zip(units, kkn)]
    bt = [kn * lr[n][:, lanes_of(gi)] * p_inv[n][:, lanes_of(gi)] for (n, gi), kn in zip(units, kkn)]
    kt = [k_adj[n][:, lanes_of(gi)] * p_inv[n][:, lanes_of(gi)] for n, gi in units]
    rt = [r[n][:, lanes_of(gi)] * p_incl[n][:, lanes_of(gi)] for n, gi in units]

    lhs2 = [jnp.concatenate([a, q], axis=0).astype(BF16) for a, q in zip(at, rt)]
    ab = [_dot_nt(l2, stack(b)) for l2, b in zip(lhs2, bt)]
    ak = [_dot_nt(l2, stack(kx)) for l2, kx in zip(lhs2, kt)]
    a_ab = [jnp.where(strict, x[:C], 0.0) for x in ab]
    q_ab = [jnp.where(incl, x[C:], 0.0) for x in ab]
    a_ak = [jnp.where(strict, x[:C], 0.0) for x in ak]
    q_ak = [jnp.where(incl, x[C:], 0.0) for x in ak]
    vs = [stack(x) for x in v_g]
    akv = [_dot(a.astype(BF16), s) for a, s in zip(a_ak, vs)]

    n_steps = max(1, (C - 1).bit_length())
    tiles_per_group = G // LANES
    ys, lps = [], []
    for ui in range(len(units)):
        for p in range(tiles_per_group):
            at_p = at[ui][:, p * LANES:(p + 1) * LANES]
            akv_p = akv[ui][:, p * LANES:(p + 1) * LANES]
            ys.append(jnp.where(low_half, at_p, swap_halves(akv_p)))
            ys.append(jnp.where(low_half, swap_halves(at_p), akv_p))
            for e in range(2):
                h = 2 * p + e
                lps.append(a_ab[ui][:, h * C:(h + 1) * C])
    def with_zero_rows(lp, skip):
        return lp if skip == 0 else jnp.concatenate([jnp.zeros((skip, C), F32), lp], axis=0)

    def add_rows(y, pr, skip):
        return y + pr if skip == 0 else jnp.concatenate([y[:skip], y[skip:] + pr], axis=0)

    skip = 0
    for step in range(n_steps):
        m = 1 << step
        if m >= RW_TRIM_ROWS:
            lps = [lp[m - skip:] for lp in lps]
            skip = m
        if step == n_steps - 1:
            ys = [add_rows(y, _dot(lp.astype(BF16), y.astype(BF16)), skip) for y, lp in zip(ys, lps)]
        else:
            prods = [_dot(lp.astype(BF16), jnp.concatenate([y, with_zero_rows(lp, skip)], axis=1).astype(BF16))
                     for y, lp in zip(ys, lps)]
            ys = [add_rows(y, pr[:, 0:LANES], skip) for y, pr in zip(ys, prods)]
            lps = [pr[:, LANES:] for pr in prods]
    y1, y2 = [], []
    for ui in range(len(units)):
        w_tiles, u0_tiles = [], []
        for p in range(tiles_per_group):
            y_even, y_odd = ys[ui * GH + 2 * p], ys[ui * GH + 2 * p + 1]
            w_tiles.append(jnp.where(low_half, y_even, swap_halves(y_odd)))
            u0_tiles.append(jnp.where(low_half, swap_halves(y_even), y_odd))
        y1.append(jnp.concatenate(w_tiles, axis=1))
        y2.append(jnp.concatenate(u0_tiles, axis=1))

    s0 = [state_ref[n * n_groups + gi] for n, gi in units]
    s0b = [s.astype(BF16) for s in s0]
    uu = [_dot_nt(a.astype(BF16), s) + b for a, s, b in zip(y1, s0b, y2)]
    o_state = [_dot_nt(q.astype(BF16), s) for q, s in zip(rt, s0b)]
    o_u = [_dot(q.astype(BF16), stack(x)) for q, x in zip(q_ab, uu)]
    o_v = [_dot(q.astype(BF16), s) for q, s in zip(q_ak, vs)]
    upd = [_dot_tn(jnp.concatenate([x, vv], axis=0).astype(BF16),
                   jnp.concatenate([b * p_last[n][:, lanes_of(gi)], kx * p_last[n][:, lanes_of(gi)]],
                                   axis=0).astype(BF16))
           for (n, gi), x, vv, b, kx in zip(units, uu, v_g, bt, kt)]
    for (n, gi), s, up in zip(units, s0, upd):
        state_ref[n * n_groups + gi] = s * p_last[n][:, lanes_of(gi)] + jnp.where(head_bd, up, 0.0)
    o_g = [a + b + c_ for a, b, c_ in zip(o_state, o_u, o_v)]

    mean = [head_sum(x) * (1.0 / HEAD_DIM) for x in o_g]
    dev = [x - m for x, m in zip(o_g, mean)]
    var = [head_sum(d * d) * (1.0 / HEAD_DIM) for d in dev]
    rk_sum = [head_sum(rk_prod[n][:, lanes_of(gi)]) for n, gi in units]
    outs = []
    for (n, gi), d, vr, rs, vv in zip(units, dev, var, rk_sum, v_g):
        sl = lanes_of(gi)
        gn = d * lax.rsqrt(vr + GN_EPS) * lnw_ref[:, sl] + lnb_ref[:, sl]
        outs.append((gn + rs * vv) * gate[n][:, sl])
    for n in seqs:
        o_ref[n] = jnp.concatenate(outs[n * n_groups:(n + 1) * n_groups], axis=1).astype(BF16)


def rwkv(u_rw, shift_mu, w0, w2_pad, a0, a2_pad, g2, k_k, k_a, r_k, ln_w, ln_b, B, S):
    C = RW_CHUNK
    n_seqs = RW_SEQS_PER_STEP if B % RW_SEQS_PER_STEP == 0 else 1
    assert S % C == 0
    u3 = u_rw.reshape(B, S, RW_IN)
    vec = lambda n: pl.BlockSpec((1, n), lambda b, c: (0, 0))
    mat = lambda m, n: pl.BlockSpec((m, n), lambda b, c: (0, 0))
    return pl.pallas_call(
        functools.partial(_rwkv_body, C=C, n_seqs=n_seqs),
        grid=(B // n_seqs, S // C),
        in_specs=[
            pl.BlockSpec((n_seqs, C, RW_IN), lambda b, c: (b, c, 0)),
            vec(RW_IN), vec(RW_WIDTH), mat(LORA_W + LORA_A, RW_WIDTH), vec(RW_WIDTH),
            mat(LORA_W + LORA_A, RW_WIDTH), mat(LORA_G, RW_WIDTH),
            vec(RW_WIDTH), vec(RW_WIDTH), vec(RW_WIDTH), vec(RW_WIDTH), vec(RW_WIDTH),
        ],
        out_specs=pl.BlockSpec((n_seqs, C, RW_WIDTH), lambda b, c: (b, c, 0)),
        out_shape=jax.ShapeDtypeStruct((B, S, RW_WIDTH), BF16),
        scratch_shapes=[
            pltpu.VMEM((n_seqs, 1, RW_IN), F32),
            pltpu.VMEM((n_seqs * (RW_WIDTH // RW_GROUP), RW_GROUP, RW_GROUP), F32),
        ],
        compiler_params=_cparams(("arbitrary", "arbitrary")),
        name="rwkv",
    )(u3, shift_mu, w0, w2_pad, a0, a2_pad, g2, k_k, k_a, r_k, ln_w, ln_b)


OR_TM = 1024
OR_PARTS = 2
N_PAIRS = EXPERTS_PER_GROUP * (EXPERTS_PER_GROUP - 1) // 2
N_CLASSES = N_GROUPS * N_PAIRS
CLS_ROWS = LANES
ROUTER_ROWS = 48
SUBLANES = 8
H_CHUNKS = D_MODEL // LANES
SLAB_IN = H_CHUNKS + 1
SLAB_OUT = H_CHUNKS
OUT_PITCH = SLAB_OUT + 1


def _first_index_of(vals, target, row_f):
    return jnp.min(jnp.where(vals == target, row_f, 1e9), axis=0, keepdims=True)


def _out_router_body(x_ref, osb_ref, orw_ref, wo_ref, g_ref, wr_ref, wrhi_ref, br_ref,
                     slab_ref, meta_ref, hist_ref, carry_ref, *, tm):
    i = pl.program_id(0)

    @pl.when(i == 0)
    def _():
        carry_ref[...] = jnp.zeros_like(carry_ref)

    parts = range(OR_PARTS)
    tp = tm // OR_PARTS
    rows_of = lambda p: slice(p * tp, (p + 1) * tp)
    hs = [x_ref[rows_of(p), :] + _dot(osb_ref[rows_of(p), :], wo_ref[0:SB_WIDTH, :])
          + _dot(orw_ref[rows_of(p), :], wo_ref[SB_WIDTH:, :]) for p in parts]
    carry = carry_ref[...]
    for p in parts:
        carry = _route_part(p, tp, hs[p], carry, g_ref, wr_ref, wrhi_ref, br_ref, slab_ref, meta_ref)
    carry_ref[...] = carry
    hist_ref[...] = carry


def _route_part(p, tm, h, carry, g_ref, wr_ref, wrhi_ref, br_ref, slab_ref, meta_ref):
    ms = jnp.mean(h * h, axis=-1, keepdims=True)
    xn = h * lax.rsqrt(ms + RMS_EPS) * g_ref[...]

    xn_hi, xn_lo = _split_bf16(xn)
    both = _dot_nt(wr_ref[...], xn_hi)
    logits = both[:ROUTER_ROWS] + both[ROUTER_ROWS:] + _dot_nt(wrhi_ref[...], xn_lo) + br_ref[...][:, 0:1]

    row_f = lax.broadcasted_iota(jnp.int32, (SUBLANES, tm), 0).astype(F32)
    lg = jnp.where(row_f < N_GROUPS, logits[N_EXPERTS:N_EXPERTS + SUBLANES], -jnp.inf)
    eg = jnp.exp(lg - jnp.max(lg, axis=0, keepdims=True))
    pg = eg / jnp.sum(eg, axis=0, keepdims=True)
    g_val = jnp.max(pg, axis=0, keepdims=True)
    g_idx = _first_index_of(pg, g_val, row_f)
    sel = jnp.zeros((SUBLANES, tm), F32)
    for g in range(N_GROUPS):
        sel = jnp.where(g_idx == g, logits[g * EXPERTS_PER_GROUP:(g + 1) * EXPERTS_PER_GROUP], sel)
    ee = jnp.exp(sel - jnp.max(sel, axis=0, keepdims=True))
    pe = ee / jnp.sum(ee, axis=0, keepdims=True)
    e1 = jnp.max(pe, axis=0, keepdims=True)
    i1 = _first_index_of(pe, e1, row_f)
    pe2 = jnp.where(row_f == i1, -1.0, pe)
    e2 = jnp.max(pe2, axis=0, keepdims=True)
    i2 = _first_index_of(pe2, e2, row_f)
    den = e1 + e2
    wt1 = g_val * e1 / den
    wt2 = g_val * e2 / den
    first_lo = i1 < i2
    lo = jnp.where(first_lo, i1, i2)
    hi = jnp.where(first_lo, i2, i1)
    w_lo = jnp.where(first_lo, wt1, wt2)
    w_hi = jnp.where(first_lo, wt2, wt1)
    pair = lo * (2 * EXPERTS_PER_GROUP - 1 - lo) * 0.5 + (hi - lo - 1.0)
    cls = g_idx * N_PAIRS + pair

    cls_row = lax.broadcasted_iota(jnp.int32, (CLS_ROWS, tm), 0).astype(F32)
    onehot = cls_row == cls
    onehot_b = jnp.where(onehot, 1.0, 0.0).astype(BF16)
    upto = (lax.broadcasted_iota(jnp.int32, (tm, tm), 0)
            <= lax.broadcasted_iota(jnp.int32, (tm, tm), 1)).astype(BF16)
    cum = _dot(onehot_b, upto)
    tot = _dot(onehot_b, jnp.ones((tm, LANES), BF16))
    before = jnp.concatenate([carry] * (tm // LANES), axis=1)
    rank = jnp.sum(jnp.where(onehot, cum - 1.0 + before, 0.0), axis=0, keepdims=True)

    meta = jnp.concatenate([cls, rank, jnp.zeros((SUBLANES - 2, tm), F32)], axis=0)
    meta_ref[0, :, p * tm:(p + 1) * tm] = meta.astype(jnp.int32)

    w_rows = jnp.concatenate([w_lo, w_hi, jnp.zeros((LANES - 2, tm), F32)], axis=0)
    slab_rows = lambda j: pl.ds(p * tm * SLAB_IN + j, tm, stride=SLAB_IN)
    for j in range(H_CHUNKS):
        slab_ref[slab_rows(j), :] = h[:, j * LANES:(j + 1) * LANES]
    slab_ref[slab_rows(H_CHUNKS), :] = jnp.transpose(w_rows)
    return carry + tot


def out_router(x2d, o_sb, o_rw, w_out, g_ffn, wr_both, wr_hi, b_r):
    T = x2d.shape[0]
    tm = min(OR_TM, T)
    nt = T // tm
    const = lambda *shape: pl.BlockSpec(shape, lambda i: (0,) * len(shape))
    return pl.pallas_call(
        functools.partial(_out_router_body, tm=tm),
        grid=(nt,),
        in_specs=[
            pl.BlockSpec((tm, D_MODEL), lambda i: (i, 0)),
            pl.BlockSpec((tm, SB_WIDTH), lambda i: (i, 0)),
            pl.BlockSpec((tm, RW_WIDTH), lambda i: (i, 0)),
            const(D_MODEL, D_MODEL), const(1, D_MODEL),
            const(2 * ROUTER_ROWS, D_MODEL), const(ROUTER_ROWS, D_MODEL), const(ROUTER_ROWS, LANES),
        ],
        out_specs=[
            pl.BlockSpec((tm * SLAB_IN, LANES), lambda i: (i, 0)),
            pl.BlockSpec((1, SUBLANES, tm), lambda i: (i, 0, 0)),
            const(CLS_ROWS, LANES),
        ],
        out_shape=[
            jax.ShapeDtypeStruct((T * SLAB_IN, LANES), F32),
            jax.ShapeDtypeStruct((nt, SUBLANES, tm), jnp.int32),
            jax.ShapeDtypeStruct((CLS_ROWS, LANES), F32),
        ],
        scratch_shapes=[pltpu.VMEM((CLS_ROWS, LANES), F32)],
        compiler_params=_cparams(("arbitrary",)),
        name="out_router",
    )(x2d, o_sb, o_rw, w_out, g_ffn, wr_both, wr_hi, b_r)


PERM_TOKENS_PER_STEP = 2048
PERM_WINDOW = 256
PERM_UNROLL = 8
DMA_PRIORITIES = 2


def _slab_scatter_body(pos_ref, src_ref, dst_ref, sem, *, tokens, slab):
    window = min(PERM_WINDOW, tokens)

    def slab_copy(j, dst_tok):
        return pltpu.make_async_copy(src_ref.at[pl.ds(j * slab, slab)],
                                     dst_ref.at[pl.ds(dst_tok * slab, slab)], sem)

    def start_batch(b, carry):
        for j in range(PERM_UNROLL):
            tok = b * PERM_UNROLL + j
            slab_copy(tok, pos_ref[0, 0, tok]).start(priority=j % DMA_PRIORITIES)
        return carry

    def retire(j, carry):
        slab_copy(0, 0).wait()
        return carry

    def start_batch_retire_batch(b, carry):
        start_batch(b, carry)
        lax.fori_loop(0, PERM_UNROLL, retire, 0, unroll=True)
        return carry

    lax.fori_loop(0, window // PERM_UNROLL, start_batch, 0)
    lax.fori_loop(window // PERM_UNROLL, tokens // PERM_UNROLL, start_batch_retire_batch, 0)
    lax.fori_loop(0, window, retire, 0, unroll=PERM_UNROLL)


def slab_scatter(src, pos, slab):
    n_tok = pos.shape[0]
    tokens = min(PERM_TOKENS_PER_STEP, n_tok)
    steps = n_tok // tokens
    return pl.pallas_call(
        functools.partial(_slab_scatter_body, tokens=tokens, slab=slab),
        grid=(steps,),
        in_specs=[
            pl.BlockSpec((1, 1, tokens), lambda s: (s, 0, 0), memory_space=pltpu.SMEM),
            pl.BlockSpec((tokens * slab, LANES), lambda s: (s, 0)),
        ],
        out_specs=pl.BlockSpec(memory_space=pl.ANY),
        out_shape=jax.ShapeDtypeStruct(src.shape, src.dtype),
        scratch_shapes=[pltpu.SemaphoreType.DMA(())],
        compiler_params=_cparams(("arbitrary",)),
        name="slab_scatter",
    )(pos.reshape(steps, 1, tokens), src)


GATHER_TM = 1024


def _slab_gather_body(pos_ref, nxt_ref, src_ref, g_ref, o_ref, buf, sem, *, tm):
    i = pl.program_id(0)
    n_steps = pl.num_programs(0)
    slot = i % 2

    def slab_copy(tok, j, to_slot):
        return pltpu.make_async_copy(src_ref.at[pl.ds(tok * OUT_PITCH, SLAB_OUT)],
                                     buf.at[to_slot, pl.ds(j * SLAB_OUT, SLAB_OUT)], sem.at[to_slot])

    def fetch(idx_ref, to_slot):
        def issue(j, carry):
            slab_copy(idx_ref[0, 0, j], j, to_slot).start()
            return carry
        lax.fori_loop(0, tm, issue, 0, unroll=PERM_UNROLL)

    @pl.when(i == 0)
    def _():
        fetch(pos_ref, 0)

    @pl.when(i + 1 < n_steps)
    def _():
        fetch(nxt_ref, 1 - slot)

    def drain(j, carry):
        slab_copy(0, 0, slot).wait()
        return carry

    lax.fori_loop(0, tm, drain, 0, unroll=PERM_UNROLL)
    chunks = [buf[slot, pl.ds(j, tm, stride=SLAB_OUT), :] for j in range(H_CHUNKS)]
    squares = functools.reduce(lambda a, b: a + b, [x * x for x in chunks])
    scale = lax.rsqrt(jnp.sum(squares, axis=-1, keepdims=True) * (1.0 / D_MODEL) + RMS_EPS)
    for j in range(H_CHUNKS):
        o_ref[:, j * LANES:(j + 1) * LANES] = chunks[j] * scale * g_ref[:, j * LANES:(j + 1) * LANES]


def slab_gather(src, pos, g_final):
    n_tok = pos.shape[0]
    tm = min(GATHER_TM, n_tok)
    steps = n_tok // tm
    pos3 = pos.reshape(steps, 1, tm)
    return pl.pallas_call(
        functools.partial(_slab_gather_body, tm=tm),
        grid=(steps,),
        in_specs=[
            pl.BlockSpec((1, 1, tm), lambda i: (i, 0, 0), memory_space=pltpu.SMEM),
            pl.BlockSpec((1, 1, tm), lambda i: (jnp.minimum(i + 1, steps - 1), 0, 0), memory_space=pltpu.SMEM),
            pl.BlockSpec(memory_space=pl.ANY),
            pl.BlockSpec((1, D_MODEL), lambda i: (0, 0)),
        ],
        out_specs=pl.BlockSpec((tm, D_MODEL), lambda i: (i, 0)),
        out_shape=jax.ShapeDtypeStruct((n_tok, D_MODEL), src.dtype),
        scratch_shapes=[
            pltpu.VMEM((2, tm * SLAB_OUT, LANES), src.dtype),
            pltpu.SemaphoreType.DMA((2,)),
        ],
        compiler_params=_cparams(("arbitrary",)),
        name="slab_gather",
    )(pos3, pos3, src, g_final)


EX_TM = 256


def _experts_body(tile_ref, lo_ref, hi_ref, start_ref, end_ref,
                  hs_ref, gffn_ref, wg_lo, wu_lo, wd_lo, wg_hi, wu_hi, wd_hi, o_ref, *, tm):
    i = pl.program_id(0)
    tile = tile_ref[i]
    start = start_ref[i]
    end = end_ref[i]

    whole = jnp.logical_and(start <= tile * tm, end >= (tile + 1) * tm)
    first_of_tile = jnp.logical_or(i == 0, tile != tile_ref[jnp.maximum(i - 1, 0)])

    @pl.when(jnp.logical_and(first_of_tile, jnp.logical_not(whole)))
    def _():
        o_ref[...] = jnp.zeros_like(o_ref)

    @pl.when(end > start)
    def _():
        in_rows = lambda j: pl.ds(j, tm, stride=SLAB_IN)
        out_rows = lambda j: pl.ds(j, tm, stride=OUT_PITCH)
        h = jnp.concatenate([hs_ref[in_rows(j), :] for j in range(H_CHUNKS)], axis=1)
        w_row = hs_ref[in_rows(H_CHUNKS), :]
        w_lo = w_row[:, 0:1]
        w_hi = w_row[:, 1:2]
        ms = jnp.mean(h * h, axis=-1, keepdims=True)
        xn = (h * lax.rsqrt(ms + RMS_EPS) * gffn_ref[...]).astype(BF16)

        g_lo = _dot(xn, wg_lo[0])
        g_hi = _dot(xn, wg_hi[0])
        u_lo = _dot(xn, wu_lo[0])
        u_hi = _dot(xn, wu_hi[0])
        a_lo = (g_lo * _sigmoid(g_lo) * u_lo * w_lo).astype(BF16)
        a_hi = (g_hi * _sigmoid(g_hi) * u_hi * w_hi).astype(BF16)
        res = h + _dot(a_lo, wd_lo[0]) + _dot(a_hi, wd_hi[0])
        @pl.when(whole)
        def _():
            for j in range(H_CHUNKS):
                o_ref[out_rows(j), :] = res[:, j * LANES:(j + 1) * LANES]
            o_ref[out_rows(SLAB_OUT), :] = jnp.zeros((tm, LANES), F32)

        @pl.when(jnp.logical_not(whole))
        def _():
            rows = tile * tm + lax.broadcasted_iota(jnp.int32, (tm, 1), 0)
            mine = jnp.logical_and(rows >= start, rows < end)
            for j in range(H_CHUNKS):
                o_ref[out_rows(j), :] = jnp.where(mine, res[:, j * LANES:(j + 1) * LANES], o_ref[out_rows(j), :])


def experts(h_sorted, g_ffn, w_gate, w_up, w_down, item_tile, item_lo, item_hi, item_start, item_end):
    T = h_sorted.shape[0] // SLAB_IN
    tm = min(EX_TM, T)
    n_items = item_tile.shape[0]
    gate_spec = lambda which: pl.BlockSpec(
        (1, D_MODEL, D_EXPERT), lambda i, tl, lo, hi, st, en: ((lo, hi)[which][i], 0, 0))
    down_spec = lambda which: pl.BlockSpec(
        (1, D_EXPERT, D_MODEL), lambda i, tl, lo, hi, st, en: ((lo, hi)[which][i], 0, 0))
    vec_spec = pl.BlockSpec((1, D_MODEL), lambda i, tl, lo, hi, st, en: (0, 0))
    grid_spec = pltpu.PrefetchScalarGridSpec(
        num_scalar_prefetch=5,
        grid=(n_items,),
        in_specs=[
            pl.BlockSpec((tm * SLAB_IN, LANES), lambda i, tl, lo, hi, st, en: (tl[i], 0)),
            vec_spec,
            gate_spec(0), gate_spec(0), down_spec(0),
            gate_spec(1), gate_spec(1), down_spec(1),
        ],
        out_specs=pl.BlockSpec((tm * OUT_PITCH, LANES), lambda i, tl, lo, hi, st, en: (tl[i], 0)),
    )
    return pl.pallas_call(
        functools.partial(_experts_body, tm=tm),
        grid_spec=grid_spec,
        out_shape=jax.ShapeDtypeStruct((T * OUT_PITCH, LANES), F32),
        compiler_params=_cparams(("arbitrary",)),
        name="experts",
    )(item_tile, item_lo, item_hi, item_start, item_end,
      h_sorted, g_ffn, w_gate, w_up, w_down, w_gate, w_up, w_down)


def _pair_first(lo):
    return (lo * (2 * EXPERTS_PER_GROUP - 1 - lo)) // 2


def _work_items(hist, meta, T, tm):
    i32 = jnp.int32
    counts = hist[:N_CLASSES, 0].astype(i32)
    offs = jnp.cumsum(counts) - counts
    cls = meta[:, 0, :].reshape(T)
    rank = meta[:, 1, :].reshape(T)
    class_ids = jnp.arange(N_CLASSES, dtype=i32)
    pos = jnp.sum(jnp.where(cls[:, None] == class_ids[None, :], offs[None, :], 0), axis=1) + rank

    n_tiles = T // tm
    n_items = n_tiles + N_CLASSES
    tile_starts = jnp.arange(n_tiles, dtype=i32) * tm
    tile_slot = jnp.arange(n_tiles, dtype=i32) + jnp.sum(offs[None, :] < tile_starts[:, None], axis=1, dtype=i32)
    class_slot = class_ids + jnp.sum(tile_starts[None, :] <= offs[:, None], axis=1, dtype=i32)
    slots = jnp.arange(n_items, dtype=i32)
    starts = (jnp.sum(jnp.where(tile_slot[None, :] == slots[:, None], tile_starts[None, :], 0), axis=1)
              + jnp.sum(jnp.where(class_slot[None, :] == slots[:, None], offs[None, :], 0), axis=1))
    ends = jnp.concatenate([starts[1:], jnp.full((1,), T, i32)])
    item_tile = jnp.minimum(starts // tm, n_tiles - 1)
    item_cls = jnp.sum(offs[None, :] <= starts[:, None], axis=1, dtype=i32) - 1
    grp = item_cls // N_PAIRS
    q = item_cls % N_PAIRS
    firsts = _pair_first(jnp.arange(1, EXPERTS_PER_GROUP - 1, dtype=i32))
    lo = jnp.sum(q[:, None] >= firsts[None, :], axis=1, dtype=i32)
    hi = q - _pair_first(lo) + lo + 1
    base = grp * EXPERTS_PER_GROUP
    return pos, item_tile, base + lo, base + hi, starts, ends


def kernel(x, norm_mix_g, w_in, shift_mu, sb_out_g, rw_w0, rw_w2, rw_a0, rw_a2, rw_g2, rw_k_k, rw_k_a, rw_r_k, rw_ln_w, rw_ln_b, w_out, norm_ffn_g, router_grp_w, router_grp_b, router_exp_w, router_exp_b, exp_w_gate, exp_w_up, exp_w_down, final_norm_g):
    B, S, D = x.shape
    T = B * S
    assert D == D_MODEL and w_in.shape[0] == 1, "one layer of width 1024 is what these kernels implement"
    l = 0
    row = lambda a: a.reshape(1, -1)
    x2d = x.reshape(T, D)

    w_in_b = w_in[l].astype(BF16)
    u_sb, u_rw = in_proj(x2d, row(norm_mix_g[l]), w_in_b[:, :SB_IN], w_in_b[:, SB_IN:])
    o_sb = sb_attn(u_sb, row(sb_out_g[l]), B, S)
    w2_pad = jnp.concatenate([rw_w2[l], jnp.zeros_like(rw_a2[l])], axis=0).astype(BF16)
    a2_pad = jnp.concatenate([jnp.zeros_like(rw_w2[l]), rw_a2[l]], axis=0).astype(BF16)
    o_rw = rwkv(u_rw, row(shift_mu[l]), row(rw_w0[l]), w2_pad, row(rw_a0[l]), a2_pad, rw_g2[l].astype(BF16),
                row(rw_k_k[l]), row(rw_k_a[l]), row(rw_r_k[l]), row(rw_ln_w[l]), row(rw_ln_b[l]), B, S)

    pad_rows = ROUTER_ROWS - N_EXPERTS - N_GROUPS
    wr = jnp.concatenate([router_exp_w[l].T, router_grp_w[l].T, jnp.zeros((pad_rows, D), F32)], axis=0)
    wr_hi, wr_lo = _split_bf16(wr)
    b_r = jnp.concatenate([router_exp_b[l], router_grp_b[l], jnp.zeros((pad_rows,), F32)])
    b_r = jnp.broadcast_to(b_r[:, None], (ROUTER_ROWS, LANES))
    h_slabs, meta, hist = out_router(x2d, o_sb.reshape(T, SB_WIDTH), o_rw.reshape(T, RW_WIDTH),
                                     w_out[l].astype(BF16), row(norm_ffn_g[l]),
                                     jnp.concatenate([wr_hi, wr_lo], axis=0), wr_hi, b_r)

    tm = min(EX_TM, T)
    pos, item_tile, item_lo, item_hi, item_start, item_end = _work_items(hist, meta, T, tm)
    h_sorted = slab_scatter(h_slabs, pos, SLAB_IN)
    out_sorted = experts(h_sorted, row(norm_ffn_g[l]),
                         exp_w_gate[l].astype(BF16), exp_w_up[l].astype(BF16), exp_w_down[l].astype(BF16),
                         item_tile, item_lo, item_hi, item_start, item_end)
    out = slab_gather(out_sorted, pos, row(final_norm_g))
    return out.reshape(B, S, D)
```

```python
import functools
import math

import jax
import jax.numpy as jnp
from jax import lax
from jax.experimental import pallas as pl
from jax.experimental.pallas import tpu as pltpu

F32 = jnp.float32
BF16 = jnp.bfloat16

D_MODEL = 1024
SB_HEADS = 8
HEAD_DIM = 64
SB_WIDTH = SB_HEADS * HEAD_DIM
RW_HEADS = 8
RW_WIDTH = RW_HEADS * HEAD_DIM
LORA_W = 64
LORA_A = 64
LORA_G = 128
SB_IN = 3 * SB_WIDTH
RW_IN = 3 * RW_WIDTH + LORA_W + LORA_A + LORA_G
N_GROUPS = 4
EXPERTS_PER_GROUP = 8
N_EXPERTS = N_GROUPS * EXPERTS_PER_GROUP
D_EXPERT = 256
RMS_EPS = 1e-6
GN_EPS = 64e-5

LANES = 128
MXU_DIM = 256
VMEM_LIMIT = 48 * 1024 * 1024


def _cparams(sem):
    return pltpu.CompilerParams(dimension_semantics=sem, vmem_limit_bytes=VMEM_LIMIT)


def _dot(a, b):
    return jnp.dot(a, b, preferred_element_type=F32)


def _dot_nt(a, b):
    return lax.dot_general(a, b, (((1,), (1,)), ((), ())), preferred_element_type=F32)


def _dot_tn(a, b):
    return lax.dot_general(a, b, (((0,), (0,)), ((), ())), preferred_element_type=F32)


IN_TM = 1024


def _in_proj_body(x_ref, g_ref, wsb_ref, wrw_ref, usb_ref, urw_ref):
    x = x_ref[...]
    ms = jnp.mean(x * x, axis=-1, keepdims=True)
    xn = (x * lax.rsqrt(ms + RMS_EPS) * g_ref[...]).astype(BF16)
    usb_ref[...] = _dot(xn, wsb_ref[...]).astype(BF16)
    urw_ref[...] = _dot(xn, wrw_ref[...]).astype(BF16)


def in_proj(x2d, g, w_sb, w_rw):
    T = x2d.shape[0]
    tm = min(IN_TM, T)
    return pl.pallas_call(
        _in_proj_body,
        grid=(T // tm,),
        in_specs=[
            pl.BlockSpec((tm, D_MODEL), lambda i: (i, 0)),
            pl.BlockSpec((1, D_MODEL), lambda i: (0, 0)),
            pl.BlockSpec((D_MODEL, SB_IN), lambda i: (0, 0)),
            pl.BlockSpec((D_MODEL, RW_IN), lambda i: (0, 0)),
        ],
        out_specs=[
            pl.BlockSpec((tm, SB_IN), lambda i: (i, 0)),
            pl.BlockSpec((tm, RW_IN), lambda i: (i, 0)),
        ],
        out_shape=[
            jax.ShapeDtypeStruct((T, SB_IN), BF16),
            jax.ShapeDtypeStruct((T, RW_IN), BF16),
        ],
        compiler_params=_cparams(("arbitrary",)),
        name="in_proj",
    )(x2d, g, w_sb, w_rw)


SB_BLK = 256
SB_TILES_PER_STEP = 4
SB_BLOCKS_PER_STEP = 4
SB_SKIP = 40.0


def _decay(z):
    one = jnp.asarray(1.0, z.dtype)
    zero = jnp.asarray(0.0, z.dtype)
    return jnp.maximum(z, zero) + jnp.log(one + jnp.exp(-jnp.abs(z)))


def _sb_attn_body(q_ref, k_ref, v_ref, g_ref, later_h_ref, later_f_ref, mask_t_ref, mask_b_ref, o_ref,
                  *, blk, n_tiles, n_sub):
    first_qb = pl.program_id(2) * n_sub
    blocks = [_sb_block(first_qb + e, e * blk, q_ref, k_ref, v_ref, g_ref, later_h_ref, later_f_ref,
                        mask_t_ref, mask_b_ref, o_ref, blk=blk, n_tiles=n_tiles) for e in range(n_sub)]
    for _ in range(SB_STAGES):
        for block in blocks:
            next(block)
    for block in blocks:
        for _ in block:
            pass


SB_STAGES = 3


def _sb_block(qb, row0, q_ref, k_ref, v_ref, g_ref, later_h_ref, later_f_ref, mask_t_ref, mask_b_ref, o_ref,
              *, blk, n_tiles):
    half = blk // 2
    tiles = range(n_tiles)
    lane = lax.broadcasted_iota(jnp.int32, (1, LANES), 1)
    head_masks = [lane < HEAD_DIM, lane >= HEAD_DIM]
    lanes_of = lambda t: slice(t * LANES, (t + 1) * LANES)

    def stack_heads(x):
        return jnp.concatenate([jnp.where(m, x, 0.0) for m in head_masks], axis=0).astype(BF16)

    def unstack(pv):
        n = pv.shape[0] // 2
        return jnp.where(head_masks[0], pv[:n], pv[n:])

    def row_sum(d):
        return jnp.sum(d.astype(F32), axis=-1, keepdims=True)

    q = [q_ref[0, row0:row0 + blk, lanes_of(t)].astype(F32) * (1.0 / math.sqrt(HEAD_DIM)) for t in tiles]
    q_top = [stack_heads(x[:half]) for x in q]
    q_bot = [stack_heads(x[half:]) for x in q]

    later_h = later_h_ref[...]
    later_f = later_f_ref[...]
    mask_t = mask_t_ref[...]
    mask_b = mask_b_ref[...]

    def kv_rows(index, size):
        start = pl.multiple_of(index * size, size)
        return ([k_ref[0, pl.ds(start, size), lanes_of(t)] for t in tiles],
                [v_ref[0, pl.ds(start, size), lanes_of(t)] for t in tiles])

    has_prev = qb >= 1
    kd, vd = kv_rows(qb, blk)
    kp, vp = kv_rows(jnp.maximum(2 * qb - 1, 0), half)
    z_td = [_dot_nt(x, k[:half]) for x, k in zip(q_top, kd)]
    z_bd = [_dot_nt(x, k) for x, k in zip(q_bot, kd)]
    z_tp = [_dot_nt(x, k) for x, k in zip(q_top, kp)]
    yield
    d_td = [_decay(z.astype(BF16)) * mask_t for z in z_td]
    d_bd = [_decay(z.astype(BF16)) * mask_b for z in z_bd]
    d_tp = [_decay(z.astype(BF16)) for z in z_tp]
    s_td = [_dot(d, later_h) for d in d_td]
    s_bd = [_dot(d, later_f) for d in d_bd]
    s_tp = [_dot(d, later_h) for d in d_tp]
    yield
    c_td = [row_sum(d) for d in d_td]
    w_td = [jnp.exp(jnp.minimum(z - s, 0.0)).astype(BF16) * mask_t for z, s in zip(z_td, s_td)]
    w_bd = [jnp.exp(jnp.minimum(z - s, 0.0)).astype(BF16) * mask_b for z, s in zip(z_bd, s_bd)]
    w_tp = [jnp.where(has_prev, jnp.exp(z - s - c), 0.0).astype(BF16) for z, s, c in zip(z_tp, s_tp, c_td)]
    acc_top = [unstack(_dot(a, v[:half]) + _dot(b, vv)) for a, v, b, vv in zip(w_td, vd, w_tp, vp)]
    acc_bot = [unstack(_dot(a, v)) for a, v in zip(w_bd, vd)]
    carry_top = [c + row_sum(d) for c, d in zip(c_td, d_tp)]
    carry_bot = [row_sum(d) for d in d_bd]
    yield

    def earlier_chunks(qs, first, carry, acc):
        def alive_of(carry):
            return functools.reduce(jnp.minimum, [jnp.min(c) for c in carry]) < SB_SKIP

        def cond(st):
            return jnp.logical_and(st[0] >= 0, st[1])

        def body(st):
            c, _, carry, acc = st
            kc, vc = kv_rows(c, half)
            z = [_dot_nt(x, k) for x, k in zip(qs, kc)]
            d = [_decay(x.astype(BF16)) for x in z]
            s = [_dot(x, later_h) for x in d]
            w = [jnp.exp(zz - ss - cr).astype(BF16) for zz, ss, cr in zip(z, s, carry)]
            acc = [a + unstack(_dot(x, vv)) for a, x, vv in zip(acc, w, vc)]
            carry = [cr + row_sum(dd) for cr, dd in zip(carry, d)]
            return c - 1, alive_of(carry), carry, acc

        return lax.while_loop(cond, body, (first, alive_of(carry), carry, acc))[3]

    acc_top = earlier_chunks(q_top, 2 * qb - 2, carry_top, acc_top)
    acc_bot = earlier_chunks(q_bot, 2 * qb - 1, carry_bot, acc_bot)

    for t in tiles:
        for rows, acc in ((slice(row0, row0 + half), acc_top[t]), (slice(row0 + half, row0 + blk), acc_bot[t])):
            sq = acc * acc
            s_lo = jnp.sum(jnp.where(head_masks[0], sq, 0.0), axis=-1, keepdims=True)
            s_all = jnp.sum(sq, axis=-1, keepdims=True)
            ms = jnp.where(head_masks[0], s_lo, s_all - s_lo) * (1.0 / HEAD_DIM)
            o_ref[0, rows, lanes_of(t)] = (acc * lax.rsqrt(ms + RMS_EPS) * g_ref[:, lanes_of(t)]).astype(BF16)


def sb_attn(u_sb, sb_out_g, B, S):
    blk = min(SB_BLK, S)
    half = blk // 2
    u3 = u_sb.reshape(B, S, SB_IN)
    n_pairs = SB_WIDTH // LANES
    after = lambda n: (jnp.arange(n)[:, None] >= jnp.arange(n)[None, :]).astype(BF16)
    before = (jnp.arange(half)[None, :] < jnp.arange(half)[:, None]).astype(BF16)
    mask_t = jnp.tile(before, (2, 1))
    mask_b = jnp.tile(jnp.concatenate([jnp.ones((half, half), BF16), before], axis=1), (2, 1))
    n_tiles = SB_TILES_PER_STEP
    width = n_tiles * LANES
    n_groups = n_pairs // n_tiles
    const = lambda r, c: pl.BlockSpec((r, c), lambda b, p, i: (0, 0))
    n_sub = SB_BLOCKS_PER_STEP if (S // blk) % SB_BLOCKS_PER_STEP == 0 else 1
    return pl.pallas_call(
        functools.partial(_sb_attn_body, blk=blk, n_tiles=n_tiles, n_sub=n_sub),
        grid=(B, n_groups, S // (n_sub * blk)),
        in_specs=[
            pl.BlockSpec((1, n_sub * blk, width), lambda b, p, i: (b, i, p)),
            pl.BlockSpec((1, S, width), lambda b, p, i: (b, 0, n_groups + p)),
            pl.BlockSpec((1, S, width), lambda b, p, i: (b, 0, 2 * n_groups + p)),
            pl.BlockSpec((1, width), lambda b, p, i: (0, p)),
            const(half, half), const(blk, blk), const(blk, half), const(blk, blk),
        ],
        out_specs=pl.BlockSpec((1, n_sub * blk, width), lambda b, p, i: (b, i, p)),
        out_shape=jax.ShapeDtypeStruct((B, S, SB_WIDTH), BF16),
        compiler_params=_cparams(("arbitrary", "arbitrary", "arbitrary")),
        name="sb_attn",
    )(u3, u3, u3, sb_out_g, after(half), after(blk), mask_t, mask_b)


RW_CHUNK = LANES
RW_SEQS_PER_STEP = 4
RW_TRIM_ROWS = 16
RW_GROUP = MXU_DIM
RW_GROUP_HEADS = RW_GROUP // HEAD_DIM


def _softplus(y):
    return jnp.maximum(y, 0.0) + jnp.log(1.0 + jnp.exp(-jnp.abs(y)))


def _sigmoid(y):
    return 1.0 / (1.0 + jnp.exp(-y))


def _split_bf16(x):
    hi = x.astype(BF16)
    lo = (x - hi.astype(F32)).astype(BF16)
    return hi, lo


def _rwkv_body(u_ref, mu_ref, w0_ref, w2_ref, a0_ref, a2_ref, g2_ref, kk_ref, ka_ref, rk_ref,
               lnw_ref, lnb_ref, o_ref, prev_ref, state_ref, *, C, n_seqs):
    c = pl.program_id(1)
    G, GH = RW_GROUP, RW_GROUP_HEADS
    n_groups = RW_WIDTH // G

    @pl.when(c == 0)
    def _():
        prev_ref[...] = jnp.zeros_like(prev_ref)
        state_ref[...] = jnp.zeros_like(state_ref)

    head_bd = (lax.broadcasted_iota(jnp.int32, (G, G), 0) // HEAD_DIM
               == lax.broadcasted_iota(jnp.int32, (G, G), 1) // HEAD_DIM)
    ones_bd = head_bd.astype(BF16)
    stack_mask = (lax.broadcasted_iota(jnp.int32, (GH * C, G), 0) // C
                  == lax.broadcasted_iota(jnp.int32, (GH * C, G), 1) // HEAD_DIM)
    low_half = lax.broadcasted_iota(jnp.int32, (1, LANES), 1) < HEAD_DIM
    tt = lax.broadcasted_iota(jnp.int32, (C, GH * C), 0)
    ss = lax.broadcasted_iota(jnp.int32, (C, GH * C), 1) % C
    strict = ss < tt
    incl = ss <= tt
    tri_incl = (lax.broadcasted_iota(jnp.int32, (C, C), 1)
                <= lax.broadcasted_iota(jnp.int32, (C, C), 0)).astype(BF16)

    def head_sum(x):
        return _dot(x.astype(BF16), ones_bd)

    def stack(x):
        return jnp.where(stack_mask, jnp.concatenate([x] * GH, axis=0), 0.0).astype(BF16)

    def swap_halves(x):
        return pltpu.roll(x, HEAD_DIM, axis=1)

    seqs = range(n_seqs)
    units = [(n, gi) for n in seqs for gi in range(n_groups)]
    lanes_of = lambda gi: slice(gi * G, (gi + 1) * G)

    ums = []
    for n in seqs:
        u = u_ref[n].astype(F32)
        row_id = lax.broadcasted_iota(jnp.int32, (C, 1), 0)
        shifted = jnp.where(row_id == 0, prev_ref[n], pltpu.roll(u, 1, axis=0))
        prev_ref[n] = u[C - 1:C, :]
        ums.append(u + (shifted - u) * mu_ref[...])
    r = [um[:, 0:RW_WIDTH] for um in ums]
    k = [um[:, RW_WIDTH:2 * RW_WIDTH] for um in ums]
    v = [um[:, 2 * RW_WIDTH:3 * RW_WIDTH] for um in ums]
    xwa = [um[:, 3 * RW_WIDTH:3 * RW_WIDTH + LORA_W + LORA_A] for um in ums]
    xg = [um[:, 3 * RW_WIDTH + LORA_W + LORA_A:] for um in ums]

    lora_w = [_dot(jnp.tanh(x).astype(BF16), w2_ref[...]) for x in xwa]
    lora_a = [_dot(x.astype(BF16), a2_ref[...]) for x in xwa]
    gate = [_dot(_sigmoid(x).astype(BF16), g2_ref[...]) for x in xg]
    logdec = [-jnp.exp(-_softplus(-(w0_ref[...] + lw)) - 0.5) for lw in lora_w]
    lr = [_sigmoid(a0_ref[...] + la) for la in lora_a]

    splits = [_split_bf16(ld) for ld in logdec]
    cum = [_dot(tri_incl, hi) + _dot(tri_incl, lo) for hi, lo in splits]
    p_incl = [jnp.exp(cm) for cm in cum]
    p_prev = [jnp.exp(cm - ld) for cm, ld in zip(cum, logdec)]
    p_inv = [jnp.exp(-cm) for cm in cum]
    p_last = [p[C - 1:C, :] for p in p_incl]

    kk = [kn * kk_ref[...] for kn in k]
    k_adj = [kn * (1.0 + (lrn - 1.0) * ka_ref[...]) for kn, lrn in zip(k, lr)]
    rk_prod = [rn * kan * rk_ref[...] for rn, kan in zip(r, k_adj)]

    kk_ssq = [head_sum(kk[n][:, lanes_of(gi)] * kk[n][:, lanes_of(gi)]) for n, gi in units]
    kkn = [kk[n][:, lanes_of(gi)] * lax.rsqrt(jnp.maximum(s, 1e-24)) for (n, gi), s in zip(units, kk_ssq)]
    v_g = [v[n][:, lanes_of(gi)] for n, gi in units]
    at = [-kn * p_prev[n][:, lanes_of(gi)] for (n, gi), kn in zip(units, kkn)]
    bt = [kn * lr[n][:, lanes_of(gi)] * p_inv[n][:, lanes_of(gi)] for (n, gi), kn in zip(units, kkn)]
    kt = [k_adj[n][:, lanes_of(gi)] * p_inv[n][:, lanes_of(gi)] for n, gi in units]
    rt = [r[n][:, lanes_of(gi)] * p_incl[n][:, lanes_of(gi)] for n, gi in units]

    lhs2 = [jnp.concatenate([a, q], axis=0).astype(BF16) for a, q in zip(at, rt)]
    ab = [_dot_nt(l2, stack(b)) for l2, b in zip(lhs2, bt)]
    ak = [_dot_nt(l2, stack(kx)) for l2, kx in zip(lhs2, kt)]
    a_ab = [jnp.where(strict, x[:C], 0.0) for x in ab]
    q_ab = [jnp.where(incl, x[C:], 0.0) for x in ab]
    a_ak = [jnp.where(strict, x[:C], 0.0) for x in ak]
    q_ak = [jnp.where(incl, x[C:], 0.0) for x in ak]
    vs = [stack(x) for x in v_g]
    akv = [_dot(a.astype(BF16), s) for a, s in zip(a_ak, vs)]

    n_steps = max(1, (C - 1).bit_length())
    tiles_per_group = G // LANES
    ys, lps = [], []
    for ui in range(len(units)):
        for p in range(tiles_per_group):
            at_p = at[ui][:, p * LANES:(p + 1) * LANES]
            akv_p = akv[ui][:, p * LANES:(p + 1) * LANES]
            ys.append(jnp.where(low_half, at_p, swap_halves(akv_p)))
            ys.append(jnp.where(low_half, swap_halves(at_p), akv_p))
            for e in range(2):
                h = 2 * p + e
                lps.append(a_ab[ui][:, h * C:(h + 1) * C])
    def with_zero_rows(lp, skip):
        return lp if skip == 0 else jnp.concatenate([jnp.zeros((skip, C), F32), lp], axis=0)

    def add_rows(y, pr, skip):
        return y + pr if skip == 0 else jnp.concatenate([y[:skip], y[skip:] + pr], axis=0)

    skip = 0
    for step in range(n_steps):
        m = 1 << step
        if m >= RW_TRIM_ROWS:
            lps = [lp[m - skip:] for lp in lps]
            skip = m
        if step == n_steps - 1:
            ys = [add_rows(y, _dot(lp.astype(BF16), y.astype(BF16)), skip) for y, lp in zip(ys, lps)]
        else:
            prods = [_dot(lp.astype(BF16), jnp.concatenate([y, with_zero_rows(lp, skip)], axis=1).astype(BF16))
                     for y, lp in zip(ys, lps)]
            ys = [add_rows(y, pr[:, 0:LANES], skip) for y, pr in zip(ys, prods)]
            lps = [pr[:, LANES:] for pr in prods]
    y1, y2 = [], []
    for ui in range(len(units)):
        w_tiles, u0_tiles = [], []
        for p in range(tiles_per_group):
            y_even, y_odd = ys[ui * GH + 2 * p], ys[ui * GH + 2 * p + 1]
            w_tiles.append(jnp.where(low_half, y_even, swap_halves(y_odd)))
            u0_tiles.append(jnp.where(low_half, swap_halves(y_even), y_odd))
        y1.append(jnp.concatenate(w_tiles, axis=1))
        y2.append(jnp.concatenate(u0_tiles, axis=1))

    s0 = [state_ref[n * n_groups + gi] for n, gi in units]
    s0b = [s.astype(BF16) for s in s0]
    uu = [_dot_nt(a.astype(BF16), s) + b for a, s, b in zip(y1, s0b, y2)]
    o_state = [_dot_nt(q.astype(BF16), s) for q, s in zip(rt, s0b)]
    o_u = [_dot(q.astype(BF16), stack(x)) for q, x in zip(q_ab, uu)]
    o_v = [_dot(q.astype(BF16), s) for q, s in zip(q_ak, vs)]
    upd = [_dot_tn(jnp.concatenate([x, vv], axis=0).astype(BF16),
                   jnp.concatenate([b * p_last[n][:, lanes_of(gi)], kx * p_last[n][:, lanes_of(gi)]],
                                   axis=0).astype(BF16))
           for (n, gi), x, vv, b, kx in zip(units, uu, v_g, bt, kt)]
    for (n, gi), s, up in zip(units, s0, upd):
        state_ref[n * n_groups + gi] = s * p_last[n][:, lanes_of(gi)] + jnp.where(head_bd, up, 0.0)
    o_g = [a + b + c_ for a, b, c_ in zip(o_state, o_u, o_v)]

    mean = [head_sum(x) * (1.0 / HEAD_DIM) for x in o_g]
    dev = [x - m for x, m in zip(o_g, mean)]
    var = [head_sum(d * d) * (1.0 / HEAD_DIM) for d in dev]
    rk_sum = [head_sum(rk_prod[n][:, lanes_of(gi)]) for n, gi in units]
    outs = []
    for (n, gi), d, vr, rs, vv in zip(units, dev, var, rk_sum, v_g):
        sl = lanes_of(gi)
        gn = d * lax.rsqrt(vr + GN_EPS) * lnw_ref[:, sl] + lnb_ref[:, sl]
        outs.append((gn + rs * vv) * gate[n][:, sl])
    for n in seqs:
        o_ref[n] = jnp.concatenate(outs[n * n_groups:(n + 1) * n_groups], axis=1).astype(BF16)


def rwkv(u_rw, shift_mu, w0, w2_pad, a0, a2_pad, g2, k_k, k_a, r_k, ln_w, ln_b, B, S):
    C = RW_CHUNK
    n_seqs = RW_SEQS_PER_STEP if B % RW_SEQS_PER_STEP == 0 else 1
    assert S % C == 0
    u3 = u_rw.reshape(B, S, RW_IN)
    vec = lambda n: pl.BlockSpec((1, n), lambda b, c: (0, 0))
    mat = lambda m, n: pl.BlockSpec((m, n), lambda b, c: (0, 0))
    return pl.pallas_call(
        functools.partial(_rwkv_body, C=C, n_seqs=n_seqs),
        grid=(B // n_seqs, S // C),
        in_specs=[
            pl.BlockSpec((n_seqs, C, RW_IN), lambda b, c: (b, c, 0)),
            vec(RW_IN), vec(RW_WIDTH), mat(LORA_W + LORA_A, RW_WIDTH), vec(RW_WIDTH),
            mat(LORA_W + LORA_A, RW_WIDTH), mat(LORA_G, RW_WIDTH),
            vec(RW_WIDTH), vec(RW_WIDTH), vec(RW_WIDTH), vec(RW_WIDTH), vec(RW_WIDTH),
        ],
        out_specs=pl.BlockSpec((n_seqs, C, RW_WIDTH), lambda b, c: (b, c, 0)),
        out_shape=jax.ShapeDtypeStruct((B, S, RW_WIDTH), BF16),
        scratch_shapes=[
            pltpu.VMEM((n_seqs, 1, RW_IN), F32),
            pltpu.VMEM((n_seqs * (RW_WIDTH // RW_GROUP), RW_GROUP, RW_GROUP), F32),
        ],
        compiler_params=_cparams(("arbitrary", "arbitrary")),
        name="rwkv",
    )(u3, shift_mu, w0, w2_pad, a0, a2_pad, g2, k_k, k_a, r_k, ln_w, ln_b)


OR_TM = 1024
OR_PARTS = 2
N_PAIRS = EXPERTS_PER_GROUP * (EXPERTS_PER_GROUP - 1) // 2
N_CLASSES = N_GROUPS * N_PAIRS
CLS_ROWS = LANES
ROUTER_ROWS = 48
SUBLANES = 8
H_CHUNKS = D_MODEL // LANES
SLAB_IN = H_CHUNKS + 1
SLAB_OUT = H_CHUNKS
OUT_PITCH = SLAB_OUT + 1


def _first_index_of(vals, target, row_f):
    return jnp.min(jnp.where(vals == target, row_f, 1e9), axis=0, keepdims=True)


def _out_router_body(x_ref, osb_ref, orw_ref, wo_ref, g_ref, wr_ref, wrhi_ref, br_ref,
                     slab_ref, meta_ref, hist_ref, carry_ref, *, tm):
    i = pl.program_id(0)

    @pl.when(i == 0)
    def _():
        carry_ref[...] = jnp.zeros_like(carry_ref)

    parts = range(OR_PARTS)
    tp = tm // OR_PARTS
    rows_of = lambda p: slice(p * tp, (p + 1) * tp)
    hs = [x_ref[rows_of(p), :] + _dot(osb_ref[rows_of(p), :], wo_ref[0:SB_WIDTH, :])
          + _dot(orw_ref[rows_of(p), :], wo_ref[SB_WIDTH:, :]) for p in parts]
    carry = carry_ref[...]
    for p in parts:
        carry = _route_part(p, tp, hs[p], carry, g_ref, wr_ref, wrhi_ref, br_ref, slab_ref, meta_ref)
    carry_ref[...] = carry
    hist_ref[...] = carry


def _route_part(p, tm, h, carry, g_ref, wr_ref, wrhi_ref, br_ref, slab_ref, meta_ref):
    ms = jnp.mean(h * h, axis=-1, keepdims=True)
    xn = h * lax.rsqrt(ms + RMS_EPS) * g_ref[...]

    xn_hi, xn_lo = _split_bf16(xn)
    both = _dot_nt(wr_ref[...], xn_hi)
    logits = both[:ROUTER_ROWS] + both[ROUTER_ROWS:] + _dot_nt(wrhi_ref[...], xn_lo) + br_ref[...][:, 0:1]

    row_f = lax.broadcasted_iota(jnp.int32, (SUBLANES, tm), 0).astype(F32)
    lg = jnp.where(row_f < N_GROUPS, logits[N_EXPERTS:N_EXPERTS + SUBLANES], -jnp.inf)
    eg = jnp.exp(lg - jnp.max(lg, axis=0, keepdims=True))
    pg = eg / jnp.sum(eg, axis=0, keepdims=True)
    g_val = jnp.max(pg, axis=0, keepdims=True)
    g_idx = _first_index_of(pg, g_val, row_f)
    sel = jnp.zeros((SUBLANES, tm), F32)
    for g in range(N_GROUPS):
        sel = jnp.where(g_idx == g, logits[g * EXPERTS_PER_GROUP:(g + 1) * EXPERTS_PER_GROUP], sel)
    ee = jnp.exp(sel - jnp.max(sel, axis=0, keepdims=True))
    pe = ee / jnp.sum(ee, axis=0, keepdims=True)
    e1 = jnp.max(pe, axis=0, keepdims=True)
    i1 = _first_index_of(pe, e1, row_f)
    pe2 = jnp.where(row_f == i1, -1.0, pe)
    e2 = jnp.max(pe2, axis=0, keepdims=True)
    i2 = _first_index_of(pe2, e2, row_f)
    den = e1 + e2
    wt1 = g_val * e1 / den
    wt2 = g_val * e2 / den
    first_lo = i1 < i2
    lo = jnp.where(first_lo, i1, i2)
    hi = jnp.where(first_lo, i2, i1)
    w_lo = jnp.where(first_lo, wt1, wt2)
    w_hi = jnp.where(first_lo, wt2, wt1)
    pair = lo * (2 * EXPERTS_PER_GROUP - 1 - lo) * 0.5 + (hi - lo - 1.0)
    cls = g_idx * N_PAIRS + pair

    cls_row = lax.broadcasted_iota(jnp.int32, (CLS_ROWS, tm), 0).astype(F32)
    onehot = cls_row == cls
    onehot_b = jnp.where(onehot, 1.0, 0.0).astype(BF16)
    upto = (lax.broadcasted_iota(jnp.int32, (tm, tm), 0)
            <= lax.broadcasted_iota(jnp.int32, (tm, tm), 1)).astype(BF16)
    cum = _dot(onehot_b, upto)
    tot = _dot(onehot_b, jnp.ones((tm, LANES), BF16))
    before = jnp.concatenate([carry] * (tm // LANES), axis=1)
    rank = jnp.sum(jnp.where(onehot, cum - 1.0 + before, 0.0), axis=0, keepdims=True)

    meta = jnp.concatenate([cls, rank, jnp.zeros((SUBLANES - 2, tm), F32)], axis=0)
    meta_ref[0, :, p * tm:(p + 1) * tm] = meta.astype(jnp.int32)

    w_rows = jnp.concatenate([w_lo, w_hi, jnp.zeros((LANES - 2, tm), F32)], axis=0)
    slab_rows = lambda j: pl.ds(p * tm * SLAB_IN + j, tm, stride=SLAB_IN)
    for j in range(H_CHUNKS):
        slab_ref[slab_rows(j), :] = h[:, j * LANES:(j + 1) * LANES]
    slab_ref[slab_rows(H_CHUNKS), :] = jnp.transpose(w_rows)
    return carry + tot


def out_router(x2d, o_sb, o_rw, w_out, g_ffn, wr_both, wr_hi, b_r):
    T = x2d.shape[0]
    tm = min(OR_TM, T)
    nt = T // tm
    const = lambda *shape: pl.BlockSpec(shape, lambda i: (0,) * len(shape))
    return pl.pallas_call(
        functools.partial(_out_router_body, tm=tm),
        grid=(nt,),
        in_specs=[
            pl.BlockSpec((tm, D_MODEL), lambda i: (i, 0)),
            pl.BlockSpec((tm, SB_WIDTH), lambda i: (i, 0)),
            pl.BlockSpec((tm, RW_WIDTH), lambda i: (i, 0)),
            const(D_MODEL, D_MODEL), const(1, D_MODEL),
            const(2 * ROUTER_ROWS, D_MODEL), const(ROUTER_ROWS, D_MODEL), const(ROUTER_ROWS, LANES),
        ],
        out_specs=[
            pl.BlockSpec((tm * SLAB_IN, LANES), lambda i: (i, 0)),
            pl.BlockSpec((1, SUBLANES, tm), lambda i: (i, 0, 0)),
            const(CLS_ROWS, LANES),
        ],
        out_shape=[
            jax.ShapeDtypeStruct((T * SLAB_IN, LANES), F32),
            jax.ShapeDtypeStruct((nt, SUBLANES, tm), jnp.int32),
            jax.ShapeDtypeStruct((CLS_ROWS, LANES), F32),
        ],
        scratch_shapes=[pltpu.VMEM((CLS_ROWS, LANES), F32)],
        compiler_params=_cparams(("arbitrary",)),
        name="out_router",
    )(x2d, o_sb, o_rw, w_out, g_ffn, wr_both, wr_hi, b_r)


PERM_TOKENS_PER_STEP = 2048
PERM_WINDOW = 256
PERM_UNROLL = 8
DMA_PRIORITIES = 2


def _slab_scatter_body(pos_ref, src_ref, dst_ref, sem, *, tokens, slab):
    window = min(PERM_WINDOW, tokens)

    def slab_copy(j, dst_tok):
        return pltpu.make_async_copy(src_ref.at[pl.ds(j * slab, slab)],
                                     dst_ref.at[pl.ds(dst_tok * slab, slab)], sem)

    def start_batch(b, carry):
        for j in range(PERM_UNROLL):
            tok = b * PERM_UNROLL + j
            slab_copy(tok, pos_ref[0, 0, tok]).start(priority=j % DMA_PRIORITIES)
        return carry

    def retire(j, carry):
        slab_copy(0, 0).wait()
        return carry

    def start_batch_retire_batch(b, carry):
        start_batch(b, carry)
        lax.fori_loop(0, PERM_UNROLL, retire, 0, unroll=True)
        return carry

    lax.fori_loop(0, window // PERM_UNROLL, start_batch, 0)
    lax.fori_loop(window // PERM_UNROLL, tokens // PERM_UNROLL, start_batch_retire_batch, 0)
    lax.fori_loop(0, window, retire, 0, unroll=PERM_UNROLL)


def slab_scatter(src, pos, slab):
    n_tok = pos.shape[0]
    tokens = min(PERM_TOKENS_PER_STEP, n_tok)
    steps = n_tok // tokens
    return pl.pallas_call(
        functools.partial(_slab_scatter_body, tokens=tokens, slab=slab),
        grid=(steps,),
        in_specs=[
            pl.BlockSpec((1, 1, tokens), lambda s: (s, 0, 0), memory_space=pltpu.SMEM),
            pl.BlockSpec((tokens * slab, LANES), lambda s: (s, 0)),
        ],
        out_specs=pl.BlockSpec(memory_space=pl.ANY),
        out_shape=jax.ShapeDtypeStruct(src.shape, src.dtype),
        scratch_shapes=[pltpu.SemaphoreType.DMA(())],
        compiler_params=_cparams(("arbitrary",)),
        name="slab_scatter",
    )(pos.reshape(steps, 1, tokens), src)


GATHER_TM = 1024


def _slab_gather_body(pos_ref, nxt_ref, src_ref, g_ref, o_ref, buf, sem, *, tm):
    i = pl.program_id(0)
    n_steps = pl.num_programs(0)
    slot = i % 2

    def slab_copy(tok, j, to_slot):
        return pltpu.make_async_copy(src_ref.at[pl.ds(tok * OUT_PITCH, SLAB_OUT)],
                                     buf.at[to_slot, pl.ds(j * SLAB_OUT, SLAB_OUT)], sem.at[to_slot])

    def fetch(idx_ref, to_slot):
        def issue(j, carry):
            slab_copy(idx_ref[0, 0, j], j, to_slot).start()
            return carry
        lax.fori_loop(0, tm, issue, 0, unroll=PERM_UNROLL)

    @pl.when(i == 0)
    def _():
        fetch(pos_ref, 0)

    @pl.when(i + 1 < n_steps)
    def _():
        fetch(nxt_ref, 1 - slot)

    def drain(j, carry):
        slab_copy(0, 0, slot).wait()
        return carry

    lax.fori_loop(0, tm, drain, 0, unroll=PERM_UNROLL)
    chunks = [buf[slot, pl.ds(j, tm, stride=SLAB_OUT), :] for j in range(H_CHUNKS)]
    squares = functools.reduce(lambda a, b: a + b, [x * x for x in chunks])
    scale = lax.rsqrt(jnp.sum(squares, axis=-1, keepdims=True) * (1.0 / D_MODEL) + RMS_EPS)
    for j in range(H_CHUNKS):
        o_ref[:, j * LANES:(j + 1) * LANES] = chunks[j] * scale * g_ref[:, j * LANES:(j + 1) * LANES]


def slab_gather(src, pos, g_final):
    n_tok = pos.shape[0]
    tm = min(GATHER_TM, n_tok)
    steps = n_tok // tm
    pos3 = pos.reshape(steps, 1, tm)
    return pl.pallas_call(
        functools.partial(_slab_gather_body, tm=tm),
        grid=(steps,),
        in_specs=[
            pl.BlockSpec((1, 1, tm), lambda i: (i, 0, 0), memory_space=pltpu.SMEM),
            pl.BlockSpec((1, 1, tm), lambda i: (jnp.minimum(i + 1, steps - 1), 0, 0), memory_space=pltpu.SMEM),
            pl.BlockSpec(memory_space=pl.ANY),
            pl.BlockSpec((1, D_MODEL), lambda i: (0, 0)),
        ],
        out_specs=pl.BlockSpec((tm, D_MODEL), lambda i: (i, 0)),
        out_shape=jax.ShapeDtypeStruct((n_tok, D_MODEL), src.dtype),
        scratch_shapes=[
            pltpu.VMEM((2, tm * SLAB_OUT, LANES), src.dtype),
            pltpu.SemaphoreType.DMA((2,)),
        ],
        compiler_params=_cparams(("arbitrary",)),
        name="slab_gather",
    )(pos3, pos3, src, g_final)


EX_TM = 256


def _experts_body(tile_ref, lo_ref, hi_ref, start_ref, end_ref,
                  hs_ref, gffn_ref, wg_lo, wu_lo, wd_lo, wg_hi, wu_hi, wd_hi, o_ref, *, tm):
    i = pl.program_id(0)
    tile = tile_ref[i]
    start = start_ref[i]
    end = end_ref[i]

    whole = jnp.logical_and(start <= tile * tm, end >= (tile + 1) * tm)
    first_of_tile = jnp.logical_or(i == 0, tile != tile_ref[jnp.maximum(i - 1, 0)])

    @pl.when(jnp.logical_and(first_of_tile, jnp.logical_not(whole)))
    def _():
        o_ref[...] = jnp.zeros_like(o_ref)

    @pl.when(end > start)
    def _():
        in_rows = lambda j: pl.ds(j, tm, stride=SLAB_IN)
        out_rows = lambda j: pl.ds(j, tm, stride=OUT_PITCH)
        h = jnp.concatenate([hs_ref[in_rows(j), :] for j in range(H_CHUNKS)], axis=1)
        w_row = hs_ref[in_rows(H_CHUNKS), :]
        w_lo = w_row[:, 0:1]
        w_hi = w_row[:, 1:2]
        ms = jnp.mean(h * h, axis=-1, keepdims=True)
        xn = (h * lax.rsqrt(ms + RMS_EPS) * gffn_ref[...]).astype(BF16)

        g_lo = _dot(xn, wg_lo[0])
        g_hi = _dot(xn, wg_hi[0])
        u_lo = _dot(xn, wu_lo[0])
        u_hi = _dot(xn, wu_hi[0])
        a_lo = (g_lo * _sigmoid(g_lo) * u_lo * w_lo).astype(BF16)
        a_hi = (g_hi * _sigmoid(g_hi) * u_hi * w_hi).astype(BF16)
        res = h + _dot(a_lo, wd_lo[0]) + _dot(a_hi, wd_hi[0])
        @pl.when(whole)
        def _():
            for j in range(H_CHUNKS):
                o_ref[out_rows(j), :] = res[:, j * LANES:(j + 1) * LANES]
            o_ref[out_rows(SLAB_OUT), :] = jnp.zeros((tm, LANES), F32)

        @pl.when(jnp.logical_not(whole))
        def _():
            rows = tile * tm + lax.broadcasted_iota(jnp.int32, (tm, 1), 0)
            mine = jnp.logical_and(rows >= start, rows < end)
            for j in range(H_CHUNKS):
                o_ref[out_rows(j), :] = jnp.where(mine, res[:, j * LANES:(j + 1) * LANES], o_ref[out_rows(j), :])


def experts(h_sorted, g_ffn, w_gate, w_up, w_down, item_tile, item_lo, item_hi, item_start, item_end):
    T = h_sorted.shape[0] // SLAB_IN
    tm = min(EX_TM, T)
    n_items = item_tile.shape[0]
    gate_spec = lambda which: pl.BlockSpec(
        (1, D_MODEL, D_EXPERT), lambda i, tl, lo, hi, st, en: ((lo, hi)[which][i], 0, 0))
    down_spec = lambda which: pl.BlockSpec(
        (1, D_EXPERT, D_MODEL), lambda i, tl, lo, hi, st, en: ((lo, hi)[which][i], 0, 0))
    vec_spec = pl.BlockSpec((1, D_MODEL), lambda i, tl, lo, hi, st, en: (0, 0))
    grid_spec = pltpu.PrefetchScalarGridSpec(
        num_scalar_prefetch=5,
        grid=(n_items,),
        in_specs=[
            pl.BlockSpec((tm * SLAB_IN, LANES), lambda i, tl, lo, hi, st, en: (tl[i], 0)),
            vec_spec,
            gate_spec(0), gate_spec(0), down_spec(0),
            gate_spec(1), gate_spec(1), down_spec(1),
        ],
        out_specs=pl.BlockSpec((tm * OUT_PITCH, LANES), lambda i, tl, lo, hi, st, en: (tl[i], 0)),
    )
    return pl.pallas_call(
        functools.partial(_experts_body, tm=tm),
        grid_spec=grid_spec,
        out_shape=jax.ShapeDtypeStruct((T * OUT_PITCH, LANES), F32),
        compiler_params=_cparams(("arbitrary",)),
        name="experts",
    )(item_tile, item_lo, item_hi, item_start, item_end,
      h_sorted, g_ffn, w_gate, w_up, w_down, w_gate, w_up, w_down)


def _pair_first(lo):
    return (lo * (2 * EXPERTS_PER_GROUP - 1 - lo)) // 2


def _work_items(hist, meta, T, tm):
    i32 = jnp.int32
    counts = hist[:N_CLASSES, 0].astype(i32)
    offs = jnp.cumsum(counts) - counts
    cls = meta[:, 0, :].reshape(T)
    rank = meta[:, 1, :].reshape(T)
    class_ids = jnp.arange(N_CLASSES, dtype=i32)
    pos = jnp.sum(jnp.where(cls[:, None] == class_ids[None, :], offs[None, :], 0), axis=1) + rank

    n_tiles = T // tm
    n_items = n_tiles + N_CLASSES
    tile_starts = jnp.arange(n_tiles, dtype=i32) * tm
    tile_slot = jnp.arange(n_tiles, dtype=i32) + jnp.sum(offs[None, :] < tile_starts[:, None], axis=1, dtype=i32)
    class_slot = class_ids + jnp.sum(tile_starts[None, :] <= offs[:, None], axis=1, dtype=i32)
    slots = jnp.arange(n_items, dtype=i32)
    starts = (jnp.sum(jnp.where(tile_slot[None, :] == slots[:, None], tile_starts[None, :], 0), axis=1)
              + jnp.sum(jnp.where(class_slot[None, :] == slots[:, None], offs[None, :], 0), axis=1))
    ends = jnp.concatenate([starts[1:], jnp.full((1,), T, i32)])
    item_tile = jnp.minimum(starts // tm, n_tiles - 1)
    item_cls = jnp.sum(offs[None, :] <= starts[:, None], axis=1, dtype=i32) - 1
    grp = item_cls // N_PAIRS
    q = item_cls % N_PAIRS
    firsts = _pair_first(jnp.arange(1, EXPERTS_PER_GROUP - 1, dtype=i32))
    lo = jnp.sum(q[:, None] >= firsts[None, :], axis=1, dtype=i32)
    hi = q - _pair_first(lo) + lo + 1
    base = grp * EXPERTS_PER_GROUP
    return pos, item_tile, base + lo, base + hi, starts, ends


def kernel(x, norm_mix_g, w_in, shift_mu, sb_out_g, rw_w0, rw_w2, rw_a0, rw_a2, rw_g2, rw_k_k, rw_k_a, rw_r_k, rw_ln_w, rw_ln_b, w_out, norm_ffn_g, router_grp_w, router_grp_b, router_exp_w, router_exp_b, exp_w_gate, exp_w_up, exp_w_down, final_norm_g):
    B, S, D = x.shape
    T = B * S
    assert D == D_MODEL and w_in.shape[0] == 1, "one layer of width 1024 is what these kernels implement"
    l = 0
    row = lambda a: a.reshape(1, -1)
    x2d = x.reshape(T, D)

    w_in_b = w_in[l].astype(BF16)
    u_sb, u_rw = in_proj(x2d, row(norm_mix_g[l]), w_in_b[:, :SB_IN], w_in_b[:, SB_IN:])
    o_sb = sb_attn(u_sb, row(sb_out_g[l]), B, S)
    w2_pad = jnp.concatenate([rw_w2[l], jnp.zeros_like(rw_a2[l])], axis=0).astype(BF16)
    a2_pad = jnp.concatenate([jnp.zeros_like(rw_w2[l]), rw_a2[l]], axis=0).astype(BF16)
    o_rw = rwkv(u_rw, row(shift_mu[l]), row(rw_w0[l]), w2_pad, row(rw_a0[l]), a2_pad, rw_g2[l].astype(BF16),
                row(rw_k_k[l]), row(rw_k_a[l]), row(rw_r_k[l]), row(rw_ln_w[l]), row(rw_ln_b[l]), B, S)

    pad_rows = ROUTER_ROWS - N_EXPERTS - N_GROUPS
    wr = jnp.concatenate([router_exp_w[l].T, router_grp_w[l].T, jnp.zeros((pad_rows, D), F32)], axis=0)
    wr_hi, wr_lo = _split_bf16(wr)
    b_r = jnp.concatenate([router_exp_b[l], router_grp_b[l], jnp.zeros((pad_rows,), F32)])
    b_r = jnp.broadcast_to(b_r[:, None], (ROUTER_ROWS, LANES))
    h_slabs, meta, hist = out_router(x2d, o_sb.reshape(T, SB_WIDTH), o_rw.reshape(T, RW_WIDTH),
                                     w_out[l].astype(BF16), row(norm_ffn_g[l]),
                                     jnp.concatenate([wr_hi, wr_lo], axis=0), wr_hi, b_r)

    tm = min(EX_TM, T)
    pos, item_tile, item_lo, item_hi, item_start, item_end = _work_items(hist, meta, T, tm)
    h_sorted = slab_scatter(h_slabs, pos, SLAB_IN)
    out_sorted = experts(h_sorted, row(norm_ffn_g[l]),
                         exp_w_gate[l].astype(BF16), exp_w_up[l].astype(BF16), exp_w_down[l].astype(BF16),
                         item_tile, item_lo, item_hi, item_start, item_end)
    out = slab_gather(out_sorted, pos, row(final_norm_g))
    return out.reshape(B, S, D)
```
